```python
import jax, jax.numpy as jnp
from jax import lax
import numpy as np

D_MODEL = 1024
BATCH = 2
SEQ = 8192
DEPTH = 1

EPS = 1e-6
D_FF = 4 * D_MODEL
M_HEADS = 4
M_DV = D_MODEL // 8
M_DQK = M_DV // 2
M_CONV = 5
M_CHUNK = 128
A_HEADS = 4
A_NOPE = D_MODEL // 8
A_ROPE = D_MODEL // 16
A_DV = D_MODEL // 8
A_DH = A_NOPE + A_ROPE
Q_LORA = D_MODEL // 4
KV_LORA = D_MODEL // 8
ROPE_THETA = 10000.0
Q_BLOCK = 128
M_QK_W = M_HEADS * M_DQK
M_V_W = M_HEADS * M_DV
N_GATES = 4 * M_HEADS
D_MIX = M_HEADS * M_DV + A_HEADS * A_DV
IN_WIDTHS = (M_QK_W, M_QK_W, M_V_W, M_V_W, N_GATES, Q_LORA, KV_LORA, A_ROPE)
D_IN = M_QK_W * 2 + M_V_W * 2 + N_GATES + Q_LORA + KV_LORA + A_ROPE

kernel_name = "hybrid_mlstm_mla_encoder_layer"


def rms_norm(x, g):
    xf = x.astype(jnp.float32)
    y = xf * lax.rsqrt(jnp.mean(xf * xf, axis=-1, keepdims=True) + EPS)
    return (y * g.astype(jnp.float32)).astype(x.dtype)


def rope(x, cos, sin):
    extra = x.ndim - 3
    c = cos.reshape(cos.shape[:2] + (1,) * extra + cos.shape[2:]).astype(jnp.float32)
    s = sin.reshape(sin.shape[:2] + (1,) * extra + sin.shape[2:]).astype(jnp.float32)
    xf = x.astype(jnp.float32)
    x1, x2 = xf[..., : A_ROPE // 2], xf[..., A_ROPE // 2:]
    return jnp.concatenate([x1 * c - x2 * s, x2 * c + x1 * s], axis=-1).astype(x.dtype)


def conv_centred(u, w, b):
    out = lax.conv_general_dilated(
        u, w[:, None, :].astype(u.dtype), window_strides=(1,),
        padding=[(M_CONV // 2, M_CONV // 2)],
        dimension_numbers=("NWC", "WIO", "NWC"), feature_group_count=u.shape[-1])
    return out + b.astype(u.dtype)


def mlstm_chunkwise(q, k, v, log_i, log_f):
    B, H, S, _ = q.shape
    L = M_CHUNK
    NC = S // L
    qc = q.reshape(B, H, NC, L, M_DQK)
    kc = k.reshape(B, H, NC, L, M_DQK)
    vc = v.reshape(B, H, NC, L, M_DV)
    li = log_i.reshape(B, H, NC, L)
    b = jnp.cumsum(log_f.reshape(B, H, NC, L), axis=-1)
    a = b[..., -1]
    w_log = a[..., None] - b + li
    g = jnp.max(w_log, axis=-1)
    w = jnp.exp(w_log - g[..., None])
    S_c = jnp.einsum('bhclv,bhcld->bhcvd', w[..., None] * vc, kc)
    n_c = jnp.einsum('bhcl,bhcld->bhcd', w, kc)

    def step(carry, inp):
        C, n, m = carry
        a_k, g_k, S_k, n_k = inp
        m_new = jnp.maximum(a_k + m, g_k)
        dec = jnp.exp(a_k + m - m_new)
        add = jnp.exp(g_k - m_new)
        C_new = dec[..., None, None] * C + add[..., None, None] * S_k
        n_new = dec[..., None] * n + add[..., None] * n_k
        return (C_new, n_new, m_new), (C, n, m)

    init = (jnp.zeros((B, H, M_DV, M_DQK), jnp.float32),
            jnp.zeros((B, H, M_DQK), jnp.float32),
            jnp.zeros((B, H), jnp.float32))
    xs = (jnp.moveaxis(a, 2, 0), jnp.moveaxis(g, 2, 0),
          jnp.moveaxis(S_c, 2, 0), jnp.moveaxis(n_c, 2, 0))
    _, (C_prev, n_prev, m_prev) = lax.scan(step, init, xs)
    C_prev = jnp.moveaxis(C_prev, 0, 2)
    n_prev = jnp.moveaxis(n_prev, 0, 2)
    m_prev = jnp.moveaxis(m_prev, 0, 2)

    D = b[..., :, None] - b[..., None, :] + li[..., None, :]
    causal_in_chunk = jnp.tril(jnp.ones((L, L), dtype=bool))
    D = jnp.where(causal_in_chunk, D, -jnp.inf)
    m_t = jnp.maximum(b + m_prev[..., None], jnp.max(D, axis=-1))
    P = jnp.exp(D - m_t[..., None]) * jnp.einsum('bhctd,bhcsd->bhcts', qc, kc)
    inter = jnp.exp(b + m_prev[..., None] - m_t)
    num = (jnp.einsum('bhcts,bhcsv->bhctv', P, vc)
           + inter[..., None] * jnp.einsum('bhctd,bhcvd->bhctv', qc, C_prev))
    den = jnp.sum(P, axis=-1) + inter * jnp.einsum('bhctd,bhcd->bhct', qc, n_prev)
    den = jnp.maximum(jnp.abs(den), jnp.exp(-m_t))
    return (num / den[..., None]).reshape(B, H, S, M_DV)


def mlstm_group(q, k, v, o, gates, b_gates, g_out):
    B, S, _ = q.shape
    f32 = jnp.float32

    def heads(t, d):
        return t.reshape(B, S, M_HEADS, d).transpose(0, 2, 1, 3).astype(f32)

    qh = heads(q, M_DQK) * (M_DQK ** -0.5)
    kh = heads(k, M_DQK)
    vh = heads(v, M_DV)
    gp = (gates.astype(f32) + b_gates.astype(f32)).reshape(B, S, 4, M_HEADS).transpose(2, 0, 3, 1)
    i_fw, f_fw, i_bw, f_bw = gp[0], gp[1], gp[2], gp[3]
    h_fw = mlstm_chunkwise(qh, kh, vh, i_fw, jax.nn.log_sigmoid(f_fw))

    def flip(t):
        return jnp.flip(t, axis=2)

    h_bw = flip(mlstm_chunkwise(flip(qh), flip(kh), flip(vh), flip(i_bw),
                                flip(jax.nn.log_sigmoid(f_bw))))
    h = (h_fw + h_bw).transpose(0, 2, 1, 3)
    h = rms_norm(h, g_out.reshape(M_HEADS, M_DV)).reshape(B, S, M_V_W)
    return (jax.nn.sigmoid(o.astype(f32)) * h).astype(q.dtype)


def mla_group(c_q, c_kv, k_pe, cos, sin, g_cq, g_ckv, w_uq, w_ukv, g_q, g_k):
    B, S, _ = c_q.shape
    q = (rms_norm(c_q, g_cq) @ w_uq).reshape(B, S, A_HEADS, A_DH)
    kv = (rms_norm(c_kv, g_ckv) @ w_ukv).reshape(B, S, A_HEADS, A_NOPE + A_DV)
    k_nope, v = kv[..., :A_NOPE], kv[..., A_NOPE:]
    k = jnp.concatenate(
        [k_nope, jnp.broadcast_to(k_pe[:, :, None, :], (B, S, A_HEADS, A_ROPE))], axis=-1)
    q = rms_norm(q, g_q)
    k = rms_norm(k, g_k)
    q = jnp.concatenate([q[..., :A_NOPE], rope(q[..., A_NOPE:], cos, sin)], axis=-1)
    k = jnp.concatenate([k[..., :A_NOPE], rope(k[..., A_NOPE:], cos, sin)], axis=-1)
    nb = S // Q_BLOCK
    qb = q.reshape(B, nb, Q_BLOCK, A_HEADS, A_DH).transpose(1, 0, 3, 2, 4)
    kh = k.transpose(0, 2, 1, 3)
    vh = v.transpose(0, 2, 1, 3)
    scale = A_DH ** -0.5

    def attend(q_blk):
        s = jnp.einsum('bhqd,bhkd->bhqk', q_blk, kh,
                       preferred_element_type=jnp.float32) * scale
        p = jax.nn.softmax(s, axis=-1)
        return jnp.einsum('bhqk,bhkv->bhqv', p.astype(vh.dtype), vh)

    out = lax.map(attend, qb)
    return out.transpose(1, 0, 3, 2, 4).reshape(B, S, A_HEADS * A_DV)


def setup_inputs(seed: int = 0) -> dict:
    key = jax.random.key(seed)
    ks = jax.random.split(key, 24)
    f32 = jnp.float32

    def nrm(k, shape, scale):
        return jax.random.normal(k, shape, f32) * scale

    def gain(k, shape):
        return 1.0 + 0.02 * jax.random.normal(k, shape, f32)

    x = jax.random.normal(ks[0], (BATCH, SEQ, D_MODEL), f32)
    offset = jax.random.randint(ks[1], (BATCH, 1), 0, 1024, dtype=jnp.int32)
    positions = (jnp.arange(SEQ, dtype=jnp.int32)[None, :] + offset).astype(jnp.int32)
    gb = jax.random.normal(ks[2], (DEPTH, 4, M_HEADS), f32)
    forget_bias = 3.0 + jnp.linspace(0.0, 3.0, M_HEADS, dtype=f32)
    b_gates = jnp.stack([0.1 * gb[:, 0], forget_bias + 0.3 * gb[:, 1],
                         0.1 * gb[:, 2], forget_bias + 0.3 * gb[:, 3]], axis=1).reshape(DEPTH, N_GATES)
    return {
        "x": x,
        "positions": positions,
        "g_mix_norm": gain(ks[3], (DEPTH, D_MODEL)),
        "w_in": nrm(ks[4], (DEPTH, D_MODEL, D_IN), D_MODEL ** -0.5),
        "conv_w": nrm(ks[5], (DEPTH, M_CONV, 2 * M_QK_W), M_CONV ** -0.5),
        "conv_b": nrm(ks[6], (DEPTH, 2 * M_QK_W), 0.01),
        "b_gates": b_gates,
        "g_mlstm_out": gain(ks[7], (DEPTH, M_V_W)),
        "g_cq": gain(ks[8], (DEPTH, Q_LORA)),
        "g_ckv": gain(ks[9], (DEPTH, KV_LORA)),
        "w_uq": nrm(ks[10], (DEPTH, Q_LORA, A_HEADS * A_DH), Q_LORA ** -0.5),
        "w_ukv": nrm(ks[11], (DEPTH, KV_LORA, A_HEADS * (A_NOPE + A_DV)), KV_LORA ** -0.5),
        "g_q": gain(ks[12], (DEPTH, A_DH)),
        "g_k": gain(ks[13], (DEPTH, A_DH)),
        "w_out": nrm(ks[14], (DEPTH, D_MIX, D_MODEL), D_MIX ** -0.5),
        "g_ffn_norm": gain(ks[15], (DEPTH, D_MODEL)),
        "w_ff1": nrm(ks[16], (DEPTH, D_MODEL, D_FF), D_MODEL ** -0.5),
        "w_ff2": nrm(ks[17], (DEPTH, D_FF, D_MODEL), D_FF ** -0.5),
    }


def reference(x, positions, g_mix_norm, w_in, conv_w, conv_b, b_gates, g_mlstm_out,
              g_cq, g_ckv, w_uq, w_ukv, g_q, g_k, w_out, g_ffn_norm, w_ff1, w_ff2):
    inv_freq = ROPE_THETA ** (-jnp.arange(0, A_ROPE, 2, dtype=jnp.float32) / A_ROPE)
    ang = positions.astype(jnp.float32)[..., None] * inv_freq
    cos, sin = jnp.cos(ang), jnp.sin(ang)
    splits = [int(s) for s in np.cumsum(IN_WIDTHS)[:-1]]
    for l in range(DEPTH):
        h = rms_norm(x, g_mix_norm[l])
        z = h @ w_in[l]
        q_m, k_m, v_m, o_m, gates, c_q, c_kv, k_pe = jnp.split(z, splits, axis=-1)
        qk = jax.nn.silu(conv_centred(jnp.concatenate([q_m, k_m], axis=-1), conv_w[l], conv_b[l]))
        q_m, k_m = qk[..., :M_QK_W], qk[..., M_QK_W:]
        y_a = mlstm_group(q_m, k_m, v_m, o_m, gates, b_gates[l], g_mlstm_out[l])
        y_b = mla_group(c_q, c_kv, k_pe, cos, sin, g_cq[l], g_ckv[l], w_uq[l], w_ukv[l],
                        g_q[l], g_k[l])
        x = x + jnp.concatenate([y_a, y_b], axis=-1) @ w_out[l]
        h = rms_norm(x, g_ffn_norm[l])
        x = x + jnp.square(jax.nn.relu(h @ w_ff1[l])) @ w_ff2[l]
    return x
```

```python
import functools
import math

import jax
import jax.numpy as jnp
from jax import lax
from jax.experimental import pallas as pl
from jax.experimental.pallas import tpu as pltpu

F32 = jnp.float32
BF16 = jnp.bfloat16

EPS = 1e-6
ROPE_THETA = 10000.0
M_HEADS = 4
M_DQK = 64
M_DV = 128
M_CONV = 5
M_CHUNK = 128
A_HEADS = 4
A_NOPE = 128
A_ROPE = 64
A_DV = 128
A_DH = A_NOPE + A_ROPE
N_GATES = 4 * M_HEADS

LANES = 128
TOKEN_TILE = 512
FF_CHUNK = 1024
VMEM_LIMIT = 56 * 1024 * 1024

LOG2E = math.log2(math.e)


def _dot(a, b):
    return jnp.dot(a, b, preferred_element_type=F32)


def _dot_nt(a, b):
    return lax.dot_general(a, b, (((1,), (1,)), ((), ())), preferred_element_type=F32)


def _dot_tn(a, b):
    return lax.dot_general(a, b, (((0,), (0,)), ((), ())), preferred_element_type=F32)


def _rms(x, g):
    ms = jnp.mean(x * x, axis=-1, keepdims=True)
    return x * lax.rsqrt(ms + EPS) * g


def _proj_kernel(x_ref, pos_ref, gmix_ref, win_ref, gcq_ref, gckv_ref, wuqT_ref, wuk_ref,
                 wuvT_ref, gqcol_ref, gk_ref, invf_ref,
                 qk_ref, v_ref, o_ref, gates_ref, qT_ref, k_ref, vT_ref):
    x = x_ref[...]
    hn = _rms(x, gmix_ref[...]).astype(BF16)
    z = _dot(hn, win_ref[...])
    qk_ref[...] = z[:, 0:512]
    v_ref[...] = z[:, 512:1024].astype(BF16)
    o_ref[...] = z[:, 1024:1536]
    cq = z[:, 1536:1792]
    ckv = z[:, 1792:1920]
    tail = z[:, 1920:2048]
    gates_ref[...] = tail[:, 96:96 + N_GATES]

    pos = pos_ref[0].astype(F32)
    angT = invf_ref[...] * pos
    cosT = jnp.cos(angT)
    sinT = jnp.sin(angT)

    cqnT = _rms(cq, gcq_ref[...]).T.astype(BF16)
    qT = _dot(wuqT_ref[...], cqnT)
    gq = gqcol_ref[...] * (A_DH ** -0.5 * LOG2E)
    for h in range(A_HEADS):
        qh = qT[h * A_DH:(h + 1) * A_DH]
        ms = jnp.sum(qh * qh, axis=0, keepdims=True) * (1.0 / A_DH)
        qn = qh * lax.rsqrt(ms + EPS) * gq
        x1 = qn[A_NOPE:A_NOPE + 32]
        x2 = qn[A_NOPE + 32:A_DH]
        qT_ref[0, h, 0, 0:A_NOPE, :] = qn[0:A_NOPE].astype(BF16)
        qT_ref[0, h, 0, A_NOPE:A_NOPE + 32, :] = (x1 * cosT - x2 * sinT).astype(BF16)
        qT_ref[0, h, 0, A_NOPE + 32:A_DH, :] = (x2 * cosT + x1 * sinT).astype(BF16)

    ckvn = _rms(ckv, gckv_ref[...])
    knope = _dot(ckvn.astype(BF16), wuk_ref[...])
    vT = _dot(wuvT_ref[...], ckvn.T.astype(BF16))
    for h in range(A_HEADS):
        vT_ref[0, h, 0] = vT[h * A_DV:(h + 1) * A_DV].astype(BF16)

    tab = jnp.concatenate([cosT, cosT, -sinT, sinT], axis=0).T
    cc = tab[:, 0:A_ROPE]
    ss = tab[:, A_ROPE:2 * A_ROPE]
    gk = gk_ref[...]
    kpe = tail[:, 0:A_ROPE]
    kpe_sw = pltpu.roll(tail, LANES - 32, axis=1)[:, 0:A_ROPE]
    ss_pe = jnp.sum(kpe * kpe, axis=-1, keepdims=True)
    krope = (kpe * gk[:, A_NOPE:A_DH] * cc + kpe_sw * gk[:, A_DH:A_DH + A_ROPE] * ss)
    for h in range(A_HEADS):
        kn = knope[:, h * A_NOPE:(h + 1) * A_NOPE]
        ms = (jnp.sum(kn * kn, axis=-1, keepdims=True) + ss_pe) * (1.0 / A_DH)
        r = lax.rsqrt(ms + EPS)
        k_ref[0, h, :, 0:A_NOPE] = (kn * r * gk[:, 0:A_NOPE]).astype(BF16)
        k_ref[0, h, :, A_NOPE:A_DH] = (krope * r).astype(BF16)


def _proj_call(xf, pos3, gmix, win, gcq, gckv, wuqT, wuk, wuvT, gqcol, gk, invf, batch, seq):
    n, d = xf.shape
    tm = TOKEN_TILE
    nt = seq // tm
    steps = n // tm

    def full(a):
        return pl.BlockSpec(a.shape, lambda i: (0,) * a.ndim)

    tok = lambda w: pl.BlockSpec((tm, w), lambda i: (i, 0))
    in_specs = [tok(d), pl.BlockSpec((1, 1, tm), lambda i: (i, 0, 0)), full(gmix), full(win),
                full(gcq), full(gckv), full(wuqT), full(wuk), full(wuvT), full(gqcol), full(gk),
                full(invf)]
    out_shape = [
        jax.ShapeDtypeStruct((n, 512), F32),
        jax.ShapeDtypeStruct((n, 512), BF16),
        jax.ShapeDtypeStruct((n, 512), F32),
        jax.ShapeDtypeStruct((n, N_GATES), F32),
        jax.ShapeDtypeStruct((batch, A_HEADS, nt, A_DH, tm), BF16),
        jax.ShapeDtypeStruct((batch, A_HEADS, seq, A_DH), BF16),
        jax.ShapeDtypeStruct((batch, A_HEADS, nt, A_DV, tm), BF16),
    ]
    out_specs = [
        tok(512), tok(512), tok(512), tok(N_GATES),
        pl.BlockSpec((1, A_HEADS, 1, A_DH, tm), lambda i: (i // nt, 0, i % nt, 0, 0)),
        pl.BlockSpec((1, A_HEADS, tm, A_DH), lambda i: (i // nt, 0, i % nt, 0)),
        pl.BlockSpec((1, A_HEADS, 1, A_DV, tm), lambda i: (i // nt, 0, i % nt, 0, 0)),
    ]
    return pl.pallas_call(
        _proj_kernel, grid=(steps,), in_specs=in_specs, out_specs=out_specs, out_shape=out_shape,
        compiler_params=pltpu.CompilerParams(dimension_semantics=("arbitrary",),
                                             vmem_limit_bytes=VMEM_LIMIT),
        name="proj",
    )(xf, pos3, gmix, win, gcq, gckv, wuqT, wuk, wuvT, gqcol, gk, invf)


def _mlstm_kernel(qk_ref, prev_ref, next_ref, v_ref, gates_ref, convw_ref, convb_ref, bg_ref,
                  h_ref, c_sc, m_sc, *, n_chunks):
    L = M_CHUNK
    d = pl.program_id(1)
    c = pl.program_id(2)
    ce = c + d * (n_chunks - 1 - 2 * c)
    is_bw = d == 1

    @pl.when(c == 0)
    def _():
        c_sc[...] = jnp.zeros_like(c_sc)
        m_sc[...] = jnp.zeros_like(m_sc)

    prev = jnp.where(ce == 0, 0.0, prev_ref[0])
    nxt = jnp.where(ce == n_chunks - 1, 0.0, next_ref[0])
    ext = jnp.concatenate([prev, qk_ref[0], nxt], axis=0)
    cw = convw_ref[...]
    u = convb_ref[...]
    for j in range(M_CONV):
        off = 8 - M_CONV // 2 + j
        u = u + cw[j:j + 1, :] * ext[off:off + L, :]
    qk = u * jax.nn.sigmoid(u)
    q_all = qk[:, 0:256] * (M_DQK ** -0.5)
    k_all = qk[:, 256:512]

    gp = gates_ref[0] + bg_ref[...]
    gd = jnp.where(is_bw, gp[:, 8:16], gp[:, 0:8])
    li = gd[:, 0:4]
    lf = jax.nn.log_sigmoid(gd[:, 4:8])

    row = lax.broadcasted_iota(jnp.int32, (L, L), 0)
    col = lax.broadcasted_iota(jnp.int32, (L, L), 1)
    causal = (col - row) * (1 - 2 * d) <= 0

    tri = jnp.where(causal, 1.0, 0.0).astype(BF16)
    hi = lf.astype(BF16)
    r1 = lf - hi.astype(F32)
    mid = r1.astype(BF16)
    lo = (r1 - mid.astype(F32)).astype(BF16)
    cum = _dot(tri, jnp.concatenate([hi, mid, lo], axis=1))
    b = cum[:, 0:4] + cum[:, 4:8] + cum[:, 8:12]
    a = jnp.sum(lf, axis=0, keepdims=True)
    w_log = a - b + li
    g = jnp.max(w_log, axis=0, keepdims=True)
    w = jnp.exp(w_log - g)
    rT = jnp.concatenate([li - b, jnp.zeros((L, LANES - 4), F32)], axis=1).T

    m_prev = m_sc[...][:, 0:4]
    m_new = jnp.maximum(a + m_prev, g)
    dec = jnp.exp(a + m_prev - m_new)
    add = jnp.exp(g - m_new)

    lane = lax.broadcasted_iota(jnp.int32, (L, LANES), 1)
    ones_col = jnp.where(lane == 0, 1.0, 0.0).astype(BF16)
    v_all = v_ref[0]

    for h in range(M_HEADS):
        qh = q_all[:, h * M_DQK:(h + 1) * M_DQK]
        kh = k_all[:, h * M_DQK:(h + 1) * M_DQK].astype(BF16)
        v_aug = jnp.concatenate([v_all[:, h * M_DV:(h + 1) * M_DV], ones_col], axis=1)
        bcol = b[:, h:h + 1]
        mp = m_prev[:, h:h + 1]
        dmat = jnp.where(causal, bcol + rT[h:h + 1, :], -jnp.inf)
        m_t = jnp.maximum(bcol + mp, jnp.max(dmat, axis=1, keepdims=True))
        gram = _dot_nt(qh.astype(BF16), kh)
        p = jnp.exp(dmat - m_t) * gram
        inter = jnp.exp(bcol + mp - m_t)
        c_prev = c_sc[h]
        lhs = jnp.concatenate([p.astype(BF16), (inter * qh).astype(BF16)], axis=1)
        rhs = jnp.concatenate([v_aug, c_prev.astype(BF16)], axis=0)
        res = _dot(lhs, rhs)
        den = jnp.maximum(jnp.abs(res[:, M_DV:M_DV + 1]), jnp.exp(-m_t))
        h_ref[0, 0, :, h * M_DV:(h + 1) * M_DV] = res[:, 0:M_DV] / den
        wv = (w[:, h:h + 1] * v_aug.astype(F32)).astype(BF16)
        s_aug = _dot_tn(kh, wv)
        c_sc[h] = dec[:, h:h + 1] * c_prev + add[:, h:h + 1] * s_aug

    m_sc[...] = jnp.concatenate([m_new, jnp.zeros((1, LANES - 4), F32)], axis=1)


def _mlstm_call(qk3, v3, gates3, convw, convb, bg):
    batch, seq, _ = qk3.shape
    L = M_CHUNK
    nc = seq // L
    r8 = L // 8

    def ce(d, c):
        return c + d * (nc - 1 - 2 * c)

    in_specs = [
        pl.BlockSpec((1, L, 512), lambda b, d, c: (b, ce(d, c), 0)),
        pl.BlockSpec((1, 8, 512), lambda b, d, c: (b, jnp.maximum(ce(d, c) * r8 - 1, 0), 0)),
        pl.BlockSpec((1, 8, 512),
                     lambda b, d, c: (b, jnp.minimum((ce(d, c) + 1) * r8, seq // 8 - 1), 0)),
        pl.BlockSpec((1, L, 512), lambda b, d, c: (b, ce(d, c), 0)),
        pl.BlockSpec((1, L, N_GATES), lambda b, d, c: (b, ce(d, c), 0)),
        pl.BlockSpec(convw.shape, lambda b, d, c: (0, 0)),
        pl.BlockSpec(convb.shape, lambda b, d, c: (0, 0)),
        pl.BlockSpec(bg.shape, lambda b, d, c: (0, 0)),
    ]
    return pl.pallas_call(
        functools.partial(_mlstm_kernel, n_chunks=nc),
        grid=(batch, 2, nc),
        in_specs=in_specs,
        out_specs=pl.BlockSpec((1, 1, L, 512), lambda b, d, c: (b, d, ce(d, c), 0)),
        out_shape=jax.ShapeDtypeStruct((batch, 2, seq, 512), F32),
        scratch_shapes=[pltpu.VMEM((M_HEADS, M_DQK, 2 * M_DV), F32), pltpu.VMEM((1, LANES), F32)],
        compiler_params=pltpu.CompilerParams(
            dimension_semantics=("arbitrary", "arbitrary", "arbitrary"),
            vmem_limit_bytes=VMEM_LIMIT),
        name="mlstm",
    )(qk3, qk3, qk3, v3, gates3, convw, convb, bg)


def _attn_kernel(qT_ref, k_ref, vT_ref, o_ref, *, n_kv):
    tk = TOKEN_TILE
    qT = qT_ref[0, 0, 0]
    tq = qT.shape[1]

    def body(j, carry):
        m, l, acc = carry
        start = pl.multiple_of(j * tk, tk)
        kt = k_ref[0, 0, pl.ds(start, tk), :]
        sT = _dot(kt, qT)
        m_new = jnp.maximum(m, jnp.max(sT, axis=0, keepdims=True))
        alpha = jnp.exp2(m - m_new)
        p = jnp.exp2(sT - m_new)
        l = alpha * l + jnp.sum(p, axis=0, keepdims=True)
        acc = alpha * acc + _dot(vT_ref[0, 0, j], p.astype(BF16))
        return m_new, l, acc

    init = (jnp.full((1, tq), -jnp.inf, F32), jnp.zeros((1, tq), F32),
            jnp.zeros((A_DV, tq), F32))
    _, l, acc = lax.fori_loop(0, n_kv, body, init)
    o_ref[0] = (acc / l).T.astype(o_ref.dtype)


def _attn_call(qT, k, vT):
    batch, heads, nt, _, tm = qT.shape
    seq = k.shape[2]
    return pl.pallas_call(
        functools.partial(_attn_kernel, n_kv=nt),
        grid=(batch, heads, nt),
        in_specs=[
            pl.BlockSpec((1, 1, 1, A_DH, tm), lambda b, h, i: (b, h, i, 0, 0)),
            pl.BlockSpec((1, 1, seq, A_DH), lambda b, h, i: (b, h, 0, 0)),
            pl.BlockSpec((1, 1, nt, A_DV, tm), lambda b, h, i: (b, h, 0, 0, 0)),
        ],
        out_specs=pl.BlockSpec((1, tm, A_DV), lambda b, h, i: (b, i, h)),
        out_shape=jax.ShapeDtypeStruct((batch, seq, heads * A_DV), BF16),
        compiler_params=pltpu.CompilerParams(
            dimension_semantics=("arbitrary", "arbitrary", "arbitrary"),
            vmem_limit_bytes=VMEM_LIMIT),
        name="attn",
    )(qT, k, vT)


def _out_kernel(x_ref, hf_ref, hb_ref, o_ref, yb_ref, gml_ref, wout_ref, gffn_ref, w1_ref, w2_ref,
                y_ref):
    x = x_ref[...]
    hsum = hf_ref[0, 0] + hb_ref[0, 0]
    gate = jax.nn.sigmoid(o_ref[...])
    gml = gml_ref[...]
    parts = []
    for h in range(M_HEADS):
        sl = slice(h * M_DV, (h + 1) * M_DV)
        parts.append((gate[:, sl] * _rms(hsum[:, sl], gml[:, sl])).astype(BF16))
    y_a = jnp.concatenate(parts, axis=1)
    n_a = y_a.shape[1]
    x1 = x + _dot(y_a, wout_ref[0:n_a, :]) + _dot(yb_ref[...], wout_ref[n_a:, :])
    hn = _rms(x1, gffn_ref[...]).astype(BF16)
    y_ref[...] = x1
    d_ff = w1_ref.shape[1]
    for j in range(d_ff // FF_CHUNK):
        sl = slice(j * FF_CHUNK, (j + 1) * FF_CHUNK)
        t = jnp.maximum(_dot(hn, w1_ref[:, sl]), 0.0)
        y_ref[...] += _dot((t * t).astype(BF16), w2_ref[sl, :])


def _out_call(xf, h4, o, yb, gml, wout, gffn, w1, w2, batch, seq):
    n, d = xf.shape
    tm = TOKEN_TILE
    nt = seq // tm

    def full(a):
        return pl.BlockSpec(a.shape, lambda i: (0,) * a.ndim)

    tok = lambda w: pl.BlockSpec((tm, w), lambda i: (i, 0))
    in_specs = [
        tok(d),
        pl.BlockSpec((1, 1, tm, 512), lambda i: (i // nt, 0, i % nt, 0)),
        pl.BlockSpec((1, 1, tm, 512), lambda i: (i // nt, 1, i % nt, 0)),
        tok(512), tok(512), full(gml), full(wout), full(gffn), full(w1), full(w2),
    ]
    h4s = h4.reshape(batch, 2, seq, 512)
    return pl.pallas_call(
        _out_kernel, grid=(n // tm,), in_specs=in_specs, out_specs=tok(d),
        out_shape=jax.ShapeDtypeStruct((n, d), F32),
        compiler_params=pltpu.CompilerParams(dimension_semantics=("arbitrary",),
                                             vmem_limit_bytes=VMEM_LIMIT),
        name="out",
    )(xf, h4s, h4s, o, yb, gml, wout, gffn, w1, w2)


def _layer(xf, pos3, invf, batch, seq, g_mix_norm, w_in, conv_w, conv_b, b_gates, g_mlstm_out,
           g_cq, g_ckv, w_uq, w_ukv, g_q, g_k, w_out, g_ffn_norm, w_ff1, w_ff2):
    d = xf.shape[1]
    c0 = 2 * 256 + 2 * 512
    gates_w = w_in[:, c0:c0 + N_GATES]
    cq_w = w_in[:, c0 + N_GATES:c0 + N_GATES + 256]
    ckv_w = w_in[:, c0 + N_GATES + 256:c0 + N_GATES + 384]
    kpe_w = w_in[:, c0 + N_GATES + 384:c0 + N_GATES + 448]
    win = jnp.concatenate(
        [w_in[:, 0:c0], cq_w, ckv_w, kpe_w, kpe_w[:, 0:32], gates_w,
         jnp.zeros((d, LANES - 96 - N_GATES), w_in.dtype)], axis=1).astype(BF16)
    wuqT = w_uq.T.astype(BF16)
    ukv = w_ukv.reshape(w_ukv.shape[0], A_HEADS, A_NOPE + A_DV)
    wuk = ukv[:, :, :A_NOPE].reshape(w_ukv.shape[0], A_HEADS * A_NOPE).astype(BF16)
    wuvT = ukv[:, :, A_NOPE:].reshape(w_ukv.shape[0], A_HEADS * A_DV).T.astype(BF16)
    gqcol = g_q.reshape(A_DH, 1)
    gk = jnp.concatenate([g_k, g_k[A_NOPE + 32:], g_k[A_NOPE:A_NOPE + 32]]).reshape(1, -1)

    qk, v, o, gates, qT, k, vT = _proj_call(
        xf, pos3, g_mix_norm.reshape(1, -1), win, g_cq.reshape(1, -1), g_ckv.reshape(1, -1),
        wuqT, wuk, wuvT, gqcol, gk, invf, batch, seq)

    h4 = _mlstm_call(qk.reshape(batch, seq, 512), v.reshape(batch, seq, 512),
                     gates.reshape(batch, seq, N_GATES), conv_w, conv_b.reshape(1, -1),
                     b_gates.reshape(1, -1))
    yb = _attn_call(qT, k, vT)
    return _out_call(xf, h4, o, yb.reshape(batch * seq, -1), g_mlstm_out.reshape(1, -1),
                     w_out.astype(BF16), g_ffn_norm.reshape(1, -1), w_ff1.astype(BF16),
                     w_ff2.astype(BF16), batch, seq)


def kernel(x, positions, g_mix_norm, w_in, conv_w, conv_b, b_gates, g_mlstm_out, g_cq, g_ckv,
           w_uq, w_ukv, g_q, g_k, w_out, g_ffn_norm, w_ff1, w_ff2):
    batch, seq, d = x.shape
    assert seq % TOKEN_TILE == 0 and seq % M_CHUNK == 0
    inv_freq = ROPE_THETA ** (-jnp.arange(0, A_ROPE, 2, dtype=F32) / A_ROPE)
    invf = inv_freq.reshape(A_ROPE // 2, 1)
    pos3 = positions.reshape(batch * seq // TOKEN_TILE, 1, TOKEN_TILE)
    xf = x.reshape(batch * seq, d)
    for l in range(w_in.shape[0]):
        xf = _layer(xf, pos3, invf, batch, seq, g_mix_norm[l], w_in[l], conv_w[l], conv_b[l],
                    b_gates[l], g_mlstm_out[l], g_cq[l], g_ckv[l], w_uq[l], w_ukv[l], g_q[l],
                    g_k[l], w_out[l], g_ffn_norm[l], w_ff1[l], w_ff2[l])
    return xf.reshape(batch, seq, d)
```

```python
import functools
import math

import jax
import jax.numpy as jnp
from jax import lax
from jax.experimental import pallas as pl
from jax.experimental.pallas import tpu as pltpu

F32 = jnp.float32
BF16 = jnp.bfloat16

EPS = 1e-6
ROPE_THETA = 10000.0
M_HEADS = 4
M_DQK = 64
M_DV = 128
M_CONV = 5
M_CHUNK = 128
A_HEADS = 4
A_NOPE = 128
A_ROPE = 64
A_DV = 128
A_DH = A_NOPE + A_ROPE
N_GATES = 4 * M_HEADS

LANES = 128
TOKEN_TILE = 512
KV_TILE = 1024
FF_CHUNK = 1024
VMEM_LIMIT = 56 * 1024 * 1024

LOG2E = math.log2(math.e)


def _dot(a, b):
    return jnp.dot(a, b, preferred_element_type=F32)


def _dot_nt(a, b):
    return lax.dot_general(a, b, (((1,), (1,)), ((), ())), preferred_element_type=F32)


def _dot_tn(a, b):
    return lax.dot_general(a, b, (((0,), (0,)), ((), ())), preferred_element_type=F32)


def _rms(x, g):
    ms = jnp.mean(x * x, axis=-1, keepdims=True)
    return x * lax.rsqrt(ms + EPS) * g


def _proj_kernel(x_ref, pos_ref, gmix_ref, win_ref, gcq_ref, gckv_ref, wuqT_ref, wuk_ref,
                 wuvT_ref, gqcol_ref, gk_ref, invf_ref,
                 qk_ref, v_ref, o_ref, gates_ref, qT_ref, k_ref, vT_ref):
    x = x_ref[...]
    hn = _rms(x, gmix_ref[...]).astype(BF16)
    z = _dot(hn, win_ref[...])
    qk_ref[...] = z[:, 0:512]
    v_ref[...] = z[:, 512:1024].astype(BF16)
    o_ref[...] = z[:, 1024:1536]
    cq = z[:, 1536:1792]
    ckv = z[:, 1792:1920]
    tail = z[:, 1920:2048]
    gates_ref[...] = tail[:, 96:96 + N_GATES]

    pos = pos_ref[0].astype(F32)
    angT = invf_ref[...] * pos
    cosT = jnp.cos(angT)
    sinT = jnp.sin(angT)

    cqnT = _rms(cq, gcq_ref[...]).T.astype(BF16)
    qT = _dot(wuqT_ref[...], cqnT)
    gq = gqcol_ref[...] * (A_DH ** -0.5 * LOG2E)
    for h in range(A_HEADS):
        qh = qT[h * A_DH:(h + 1) * A_DH]
        ms = jnp.sum(qh * qh, axis=0, keepdims=True) * (1.0 / A_DH)
        qn = qh * lax.rsqrt(ms + EPS) * gq
        x1 = qn[A_NOPE:A_NOPE + 32]
        x2 = qn[A_NOPE + 32:A_DH]
        qT_ref[0, h, 0, 0:A_NOPE, :] = qn[0:A_NOPE].astype(BF16)
        qT_ref[0, h, 0, A_NOPE:A_NOPE + 32, :] = (x1 * cosT - x2 * sinT).astype(BF16)
        qT_ref[0, h, 0, A_NOPE + 32:A_DH, :] = (x2 * cosT + x1 * sinT).astype(BF16)

    ckvn = _rms(ckv, gckv_ref[...])
    knope = _dot(ckvn.astype(BF16), wuk_ref[...])
    vT = _dot(wuvT_ref[...], ckvn.T.astype(BF16))
    for h in range(A_HEADS):
        vT_ref[0, h, 0] = vT[h * A_DV:(h + 1) * A_DV].astype(BF16)

    tab = jnp.concatenate([cosT, cosT, -sinT, sinT], axis=0).T
    cc = tab[:, 0:A_ROPE]
    ss = tab[:, A_ROPE:2 * A_ROPE]
    gk = gk_ref[...]
    kpe = tail[:, 0:A_ROPE]
    kpe_sw = pltpu.roll(tail, LANES - 32, axis=1)[:, 0:A_ROPE]
    ss_pe = jnp.sum(kpe * kpe, axis=-1, keepdims=True)
    krope = (kpe * gk[:, A_NOPE:A_DH] * cc + kpe_sw * gk[:, A_DH:A_DH + A_ROPE] * ss)
    for h in range(A_HEADS):
        kn = knope[:, h * A_NOPE:(h + 1) * A_NOPE]
        ms = (jnp.sum(kn * kn, axis=-1, keepdims=True) + ss_pe) * (1.0 / A_DH)
        r = lax.rsqrt(ms + EPS)
        k_ref[0, h, :, 0:A_NOPE] = (kn * r * gk[:, 0:A_NOPE]).astype(BF16)
        k_ref[0, h, :, A_NOPE:A_DH] = (krope * r).astype(BF16)


def _proj_call(xf, pos3, gmix, win, gcq, gckv, wuqT, wuk, wuvT, gqcol, gk, invf, batch, seq):
    n, d = xf.shape
    tm = TOKEN_TILE
    nt = seq // tm
    steps = n // tm

    def full(a):
        return pl.BlockSpec(a.shape, lambda i: (0,) * a.ndim)

    tok = lambda w: pl.BlockSpec((tm, w), lambda i: (i, 0))
    in_specs = [tok(d), pl.BlockSpec((1, 1, tm), lambda i: (i, 0, 0)), full(gmix), full(win),
                full(gcq), full(gckv), full(wuqT), full(wuk), full(wuvT), full(gqcol), full(gk),
                full(invf)]
    out_shape = [
        jax.ShapeDtypeStruct((n, 512), F32),
        jax.ShapeDtypeStruct((n, 512), BF16),
        jax.ShapeDtypeStruct((n, 512), F32),
        jax.ShapeDtypeStruct((n, N_GATES), F32),
        jax.ShapeDtypeStruct((batch, A_HEADS, nt, A_DH, tm), BF16),
        jax.ShapeDtypeStruct((batch, A_HEADS, seq, A_DH), BF16),
        jax.ShapeDtypeStruct((batch, A_HEADS, nt, A_DV, tm), BF16),
    ]
    out_specs = [
        tok(512), tok(512), tok(512), tok(N_GATES),
        pl.BlockSpec((1, A_HEADS, 1, A_DH, tm), lambda i: (i // nt, 0, i % nt, 0, 0)),
        pl.BlockSpec((1, A_HEADS, tm, A_DH), lambda i: (i // nt, 0, i % nt, 0)),
        pl.BlockSpec((1, A_HEADS, 1, A_DV, tm), lambda i: (i // nt, 0, i % nt, 0, 0)),
    ]
    return pl.pallas_call(
        _proj_kernel, grid=(steps,), in_specs=in_specs, out_specs=out_specs, out_shape=out_shape,
        compiler_params=pltpu.CompilerParams(dimension_semantics=("arbitrary",),
                                             vmem_limit_bytes=VMEM_LIMIT),
        name="proj",
    )(xf, pos3, gmix, win, gcq, gckv, wuqT, wuk, wuvT, gqcol, gk, invf)


def _mlstm_kernel(qk_ref, prev_ref, next_ref, v_ref, gates_ref, convw_ref, convb_ref, bg_ref,
                  h_ref, c_sc, m_sc, *, n_chunks):
    L = M_CHUNK
    d = pl.program_id(1)
    c = pl.program_id(2)
    ce = c + d * (n_chunks - 1 - 2 * c)
    is_bw = d == 1

    @pl.when(c == 0)
    def _():
        c_sc[...] = jnp.zeros_like(c_sc)
        m_sc[...] = jnp.zeros_like(m_sc)

    prev = jnp.where(ce == 0, 0.0, prev_ref[0])
    nxt = jnp.where(ce == n_chunks - 1, 0.0, next_ref[0])
    ext = jnp.concatenate([prev, qk_ref[0], nxt], axis=0)
    cw = convw_ref[...]
    u = convb_ref[...]
    for j in range(M_CONV):
        off = 8 - M_CONV // 2 + j
        u = u + cw[j:j + 1, :] * ext[off:off + L, :]
    qk = u * jax.nn.sigmoid(u)
    q_all = qk[:, 0:256] * (M_DQK ** -0.5)
    k_all = qk[:, 256:512]

    gp = gates_ref[0] + bg_ref[...]
    gd = jnp.where(is_bw, gp[:, 8:16], gp[:, 0:8])
    li = gd[:, 0:4]
    lf = jax.nn.log_sigmoid(gd[:, 4:8])

    row = lax.broadcasted_iota(jnp.int32, (L, L), 0)
    col = lax.broadcasted_iota(jnp.int32, (L, L), 1)
    causal = (col - row) * (1 - 2 * d) <= 0

    tri = jnp.where(causal, 1.0, 0.0).astype(BF16)
    hi = lf.astype(BF16)
    r1 = lf - hi.astype(F32)
    mid = r1.astype(BF16)
    lo = (r1 - mid.astype(F32)).astype(BF16)
    cum = _dot(tri, jnp.concatenate([hi, mid, lo], axis=1))
    b = cum[:, 0:4] + cum[:, 4:8] + cum[:, 8:12]
    a = jnp.sum(lf, axis=0, keepdims=True)
    w_log = a - b + li
    g = jnp.max(w_log, axis=0, keepdims=True)
    w = jnp.exp(w_log - g)
    rT = jnp.concatenate([li - b, jnp.zeros((L, LANES - 4), F32)], axis=1).T

    m_prev = m_sc[...][:, 0:4]
    m_new = jnp.maximum(a + m_prev, g)
    dec = jnp.exp(a + m_prev - m_new)
    add = jnp.exp(g - m_new)

    lane = lax.broadcasted_iota(jnp.int32, (L, LANES), 1)
    ones_col = jnp.where(lane == 0, 1.0, 0.0).astype(BF16)
    v_all = v_ref[0]

    for h in range(M_HEADS):
        qh = q_all[:, h * M_DQK:(h + 1) * M_DQK]
        kh = k_all[:, h * M_DQK:(h + 1) * M_DQK].astype(BF16)
        v_aug = jnp.concatenate([v_all[:, h * M_DV:(h + 1) * M_DV], ones_col], axis=1)
        bcol = b[:, h:h + 1]
        mp = m_prev[:, h:h + 1]
        dmat = jnp.where(causal, bcol + rT[h:h + 1, :], -jnp.inf)
        m_t = jnp.maximum(bcol + mp, jnp.max(dmat, axis=1, keepdims=True))
        gram = _dot_nt(qh.astype(BF16), kh)
        p = jnp.exp(dmat - m_t) * gram
        inter = jnp.exp(bcol + mp - m_t)
        c_prev = c_sc[h]
        lhs = jnp.concatenate([p.astype(BF16), (inter * qh).astype(BF16)], axis=1)
        rhs = jnp.concatenate([v_aug, c_prev.astype(BF16)], axis=0)
        res = _dot(lhs, rhs)
        den = jnp.maximum(jnp.abs(res[:, M_DV:M_DV + 1]), jnp.exp(-m_t))
        h_ref[0, 0, :, h * M_DV:(h + 1) * M_DV] = res[:, 0:M_DV] / den
        wv = (w[:, h:h + 1] * v_aug.astype(F32)).astype(BF16)
        s_aug = _dot_tn(kh, wv)
        c_sc[h] = dec[:, h:h + 1] * c_prev + add[:, h:h + 1] * s_aug

    m_sc[...] = jnp.concatenate([m_new, jnp.zeros((1, LANES - 4), F32)], axis=1)


def _mlstm_call(qk3, v3, gates3, convw, convb, bg):
    batch, seq, _ = qk3.shape
    L = M_CHUNK
    nc = seq // L
    r8 = L // 8

    def ce(d, c):
        return c + d * (nc - 1 - 2 * c)

    in_specs = [
        pl.BlockSpec((1, L, 512), lambda b, d, c: (b, ce(d, c), 0)),
        pl.BlockSpec((1, 8, 512), lambda b, d, c: (b, jnp.maximum(ce(d, c) * r8 - 1, 0), 0)),
        pl.BlockSpec((1, 8, 512),
                     lambda b, d, c: (b, jnp.minimum((ce(d, c) + 1) * r8, seq // 8 - 1), 0)),
        pl.BlockSpec((1, L, 512), lambda b, d, c: (b, ce(d, c), 0)),
        pl.BlockSpec((1, L, N_GATES), lambda b, d, c: (b, ce(d, c), 0)),
        pl.BlockSpec(convw.shape, lambda b, d, c: (0, 0)),
        pl.BlockSpec(convb.shape, lambda b, d, c: (0, 0)),
        pl.BlockSpec(bg.shape, lambda b, d, c: (0, 0)),
    ]
    return pl.pallas_call(
        functools.partial(_mlstm_kernel, n_chunks=nc),
        grid=(batch, 2, nc),
        in_specs=in_specs,
        out_specs=pl.BlockSpec((1, 1, L, 512), lambda b, d, c: (b, d, ce(d, c), 0)),
        out_shape=jax.ShapeDtypeStruct((batch, 2, seq, 512), F32),
        scratch_shapes=[pltpu.VMEM((M_HEADS, M_DQK, 2 * M_DV), F32), pltpu.VMEM((1, LANES), F32)],
        compiler_params=pltpu.CompilerParams(
            dimension_semantics=("arbitrary", "arbitrary", "arbitrary"),
            vmem_limit_bytes=VMEM_LIMIT),
        name="mlstm",
    )(qk3, qk3, qk3, v3, gates3, convw, convb, bg)


def _attn_kernel(qT_ref, k_ref, vT_ref, o_ref, s_sc, cm_sc, acc_sc, *, n_kv):
    tk = KV_TILE
    sub = tk // TOKEN_TILE
    qT = qT_ref[0, 0, 0]
    tq = qT.shape[1]

    def scores(j, slot):
        sT = _dot(k_ref[0, 0, j * tk:(j + 1) * tk, :], qT)
        s_sc[slot] = sT
        cm_sc[slot] = jnp.max(sT, axis=0, keepdims=True)

    m = jnp.full((1, tq), -jnp.inf, F32)
    l = jnp.zeros((1, tq), F32)
    scores(0, 0)
    for j in range(n_kv):
        slot = j % 2
        if j + 1 < n_kv:
            scores(j + 1, 1 - slot)
        m_new = jnp.maximum(m, cm_sc[slot])
        alpha = jnp.exp2(m - m_new)
        p = jnp.exp2(s_sc[slot] - m_new)
        l = alpha * l + jnp.sum(p, axis=0, keepdims=True)
        vt = jnp.concatenate([vT_ref[0, 0, j * sub + i] for i in range(sub)], axis=1)
        pv = _dot(vt, p.astype(BF16))
        if j == 0:
            acc_sc[...] = pv
        else:
            acc_sc[...] = alpha * acc_sc[...] + pv
        m = m_new
    o_ref[0] = (acc_sc[...] / l).T.astype(o_ref.dtype)


def _attn_call(qT, k, vT):
    batch, heads, nt, _, tm = qT.shape
    seq = k.shape[2]
    return pl.pallas_call(
        functools.partial(_attn_kernel, n_kv=seq // KV_TILE),
        grid=(batch, heads, nt),
        in_specs=[
            pl.BlockSpec((1, 1, 1, A_DH, tm), lambda b, h, i: (b, h, i, 0, 0)),
            pl.BlockSpec((1, 1, seq, A_DH), lambda b, h, i: (b, h, 0, 0)),
            pl.BlockSpec((1, 1, nt, A_DV, tm), lambda b, h, i: (b, h, 0, 0, 0)),
        ],
        out_specs=pl.BlockSpec((1, tm, A_DV), lambda b, h, i: (b, i, h)),
        out_shape=jax.ShapeDtypeStruct((batch, seq, heads * A_DV), BF16),
        scratch_shapes=[pltpu.VMEM((2, KV_TILE, tm), F32), pltpu.VMEM((2, 1, tm), F32),
                        pltpu.VMEM((A_DV, tm), F32)],
        compiler_params=pltpu.CompilerParams(
            dimension_semantics=("arbitrary", "arbitrary", "arbitrary"),
            vmem_limit_bytes=VMEM_LIMIT),
        name="attn",
    )(qT, k, vT)


def _out_kernel(x_ref, hf_ref, hb_ref, o_ref, yb_ref, gml_ref, wout_ref, gffn_ref, w1_ref, w2_ref,
                y_ref):
    x = x_ref[...]
    hsum = hf_ref[0, 0] + hb_ref[0, 0]
    gate = jax.nn.sigmoid(o_ref[...])
    gml = gml_ref[...]
    parts = []
    for h in range(M_HEADS):
        sl = slice(h * M_DV, (h + 1) * M_DV)
        parts.append((gate[:, sl] * _rms(hsum[:, sl], gml[:, sl])).astype(BF16))
    y_a = jnp.concatenate(parts, axis=1)
    n_a = y_a.shape[1]
    x1 = x + _dot(y_a, wout_ref[0:n_a, :]) + _dot(yb_ref[...], wout_ref[n_a:, :])
    hn = _rms(x1, gffn_ref[...]).astype(BF16)
    y_ref[...] = x1
    d_ff = w1_ref.shape[1]
    for j in range(d_ff // FF_CHUNK):
        sl = slice(j * FF_CHUNK, (j + 1) * FF_CHUNK)
        t = jnp.maximum(_dot(hn, w1_ref[:, sl]), 0.0)
        y_ref[...] += _dot((t * t).astype(BF16), w2_ref[sl, :])


def _out_call(xf, h4, o, yb, gml, wout, gffn, w1, w2, batch, seq):
    n, d = xf.shape
    tm = TOKEN_TILE
    nt = seq // tm

    def full(a):
        return pl.BlockSpec(a.shape, lambda i: (0,) * a.ndim)

    tok = lambda w: pl.BlockSpec((tm, w), lambda i: (i, 0))
    in_specs = [
        tok(d),
        pl.BlockSpec((1, 1, tm, 512), lambda i: (i // nt, 0, i % nt, 0)),
        pl.BlockSpec((1, 1, tm, 512), lambda i: (i // nt, 1, i % nt, 0)),
        tok(512), tok(512), full(gml), full(wout), full(gffn), full(w1), full(w2),
    ]
    h4s = h4.reshape(batch, 2, seq, 512)
    return pl.pallas_call(
        _out_kernel, grid=(n // tm,), in_specs=in_specs, out_specs=tok(d),
        out_shape=jax.ShapeDtypeStruct((n, d), F32),
        compiler_params=pltpu.CompilerParams(dimension_semantics=("arbitrary",),
                                             vmem_limit_bytes=VMEM_LIMIT),
        name="out",
    )(xf, h4s, h4s, o, yb, gml, wout, gffn, w1, w2)


def _layer(xf, pos3, invf, batch, seq, g_mix_norm, w_in, conv_w, conv_b, b_gates, g_mlstm_out,
           g_cq, g_ckv, w_uq, w_ukv, g_q, g_k, w_out, g_ffn_norm, w_ff1, w_ff2):
    d = xf.shape[1]
    c0 = 2 * 256 + 2 * 512
    gates_w = w_in[:, c0:c0 + N_GATES]
    cq_w = w_in[:, c0 + N_GATES:c0 + N_GATES + 256]
    ckv_w = w_in[:, c0 + N_GATES + 256:c0 + N_GATES + 384]
    kpe_w = w_in[:, c0 + N_GATES + 384:c0 + N_GATES + 448]
    win = jnp.concatenate(
        [w_in[:, 0:c0], cq_w, ckv_w, kpe_w, kpe_w[:, 0:32], gates_w,
         jnp.zeros((d, LANES - 96 - N_GATES), w_in.dtype)], axis=1).astype(BF16)
    wuqT = w_uq.T.astype(BF16)
    ukv = w_ukv.reshape(w_ukv.shape[0], A_HEADS, A_NOPE + A_DV)
    wuk = ukv[:, :, :A_NOPE].reshape(w_ukv.shape[0], A_HEADS * A_NOPE).astype(BF16)
    wuvT = ukv[:, :, A_NOPE:].reshape(w_ukv.shape[0], A_HEADS * A_DV).T.astype(BF16)
    gqcol = g_q.reshape(A_DH, 1)
    gk = jnp.concatenate([g_k, g_k[A_NOPE + 32:], g_k[A_NOPE:A_NOPE + 32]]).reshape(1, -1)

    qk, v, o, gates, qT, k, vT = _proj_call(
        xf, pos3, g_mix_norm.reshape(1, -1), win, g_cq.reshape(1, -1), g_ckv.reshape(1, -1),
        wuqT, wuk, wuvT, gqcol, gk, invf, batch, seq)

    h4 = _mlstm_call(qk.reshape(batch, seq, 512), v.reshape(batch, seq, 512),
                     gates.reshape(batch, seq, N_GATES), conv_w, conv_b.reshape(1, -1),
                     b_gates.reshape(1, -1))
    yb = _attn_call(qT, k, vT)
    return _out_call(xf, h4, o, yb.reshape(batch * seq, -1), g_mlstm_out.reshape(1, -1),
                     w_out.astype(BF16), g_ffn_norm.reshape(1, -1), w_ff1.astype(BF16),
                     w_ff2.astype(BF16), batch, seq)


def kernel(x, positions, g_mix_norm, w_in, conv_w, conv_b, b_gates, g_mlstm_out, g_cq, g_ckv,
           w_uq, w_ukv, g_q, g_k, w_out, g_ffn_norm, w_ff1, w_ff2):
    batch, seq, d = x.shape
    assert seq % TOKEN_TILE == 0 and seq % M_CHUNK == 0
    inv_freq = ROPE_THETA ** (-jnp.arange(0, A_ROPE, 2, dtype=F32) / A_ROPE)
    invf = inv_freq.reshape(A_ROPE // 2, 1)
    pos3 = positions.reshape(batch * seq // TOKEN_TILE, 1, TOKEN_TILE)
    xf = x.reshape(batch * seq, d)
    for l in range(w_in.shape[0]):
        xf = _layer(xf, pos3, invf, batch, seq, g_mix_norm[l], w_in[l], conv_w[l], conv_b[l],
                    b_gates[l], g_mlstm_out[l], g_cq[l], g_ckv[l], w_uq[l], w_ukv[l], g_q[l],
                    g_k[l], w_out[l], g_ffn_norm[l], w_ff1[l], w_ff2[l])
    return xf.reshape(batch, seq, d)
```

```python
import functools
import math

import jax
import jax.numpy as jnp
from jax import lax
from jax.experimental import pallas as pl
from jax.experimental.pallas import tpu as pltpu

F32 = jnp.float32
BF16 = jnp.bfloat16

EPS = 1e-6
ROPE_THETA = 10000.0
M_HEADS = 4
M_DQK = 64
M_DV = 128
M_CONV = 5
M_CHUNK = 128
A_HEADS = 4
A_NOPE = 128
A_ROPE = 64
A_DV = 128
A_DH = A_NOPE + A_ROPE
N_GATES = 4 * M_HEADS

LANES = 128
TOKEN_TILE = 512
KV_TILE = 1024
FF_CHUNK = 1024
VMEM_LIMIT = 56 * 1024 * 1024

LOG2E = math.log2(math.e)


def _dot(a, b):
    return jnp.dot(a, b, preferred_element_type=F32)


def _dot_nt(a, b):
    return lax.dot_general(a, b, (((1,), (1,)), ((), ())), preferred_element_type=F32)


def _dot_tn(a, b):
    return lax.dot_general(a, b, (((0,), (0,)), ((), ())), preferred_element_type=F32)


def _rms(x, g):
    ms = jnp.mean(x * x, axis=-1, keepdims=True)
    return x * lax.rsqrt(ms + EPS) * g


def _split3(y):
    hi = y.astype(BF16).astype(F32)
    mid = (y - hi).astype(BF16).astype(F32)
    lo = y - hi - mid
    return jnp.concatenate([hi, mid, lo], axis=0).astype(BF16)


def _chunk_decay(x, *, backward):
    L = M_CHUNK
    row = lax.broadcasted_iota(jnp.int32, (L, L), 0)
    col = lax.broadcasted_iota(jnp.int32, (L, L), 1)
    umat = jnp.where((row >= col) if backward else (row <= col), 1.0, 0.0).astype(BF16)
    lf = jax.nn.log_sigmoid(x)
    cum3 = _dot(_split3(lf), umat)
    cum = cum3[0:8] + cum3[8:16] + cum3[16:24]
    a = jnp.broadcast_to(jnp.sum(lf, axis=1, keepdims=True), (8, L))
    b = pltpu.roll(cum, 4, axis=0)
    a = pltpu.roll(a, 4, axis=0)
    w_log = a - b + x
    g = jnp.broadcast_to(jnp.max(w_log, axis=1, keepdims=True), (8, L))
    stats = jnp.concatenate([jnp.exp(w_log - g), x - b, a, g], axis=0)
    b3 = jnp.concatenate([_split3(b), jnp.zeros((8, L), BF16)], axis=0)
    return stats, b3


def _proj_kernel(x_ref, pos_ref, gmix_ref, win_ref, gcq_ref, gckv_ref, wuqT_ref, wuk_ref,
                 wuvT_ref, gqcol_ref, gk_ref, invf_ref, bg_ref,
                 qk_ref, v_ref, o_ref, sf_ref, b3f_ref, sb_ref, b3b_ref, qT_ref, k_ref, vT_ref):
    x = x_ref[...]
    hn = _rms(x, gmix_ref[...]).astype(BF16)
    z = _dot(hn, win_ref[...])
    qk_ref[...] = z[:, 0:512]
    v_ref[...] = z[:, 512:1024].astype(BF16)
    o_ref[...] = z[:, 1024:1536]
    cq = z[:, 1536:1792]
    ckv = z[:, 1792:1920]
    tail = z[:, 1920:2048]

    tailT = tail.T
    for d, (s_ref, b3_ref) in enumerate(((sf_ref, b3f_ref), (sb_ref, b3b_ref))):
        gx = tailT[96 + 8 * d:104 + 8 * d] + bg_ref[d]
        for j in range(gx.shape[1] // M_CHUNK):
            sl = slice(j * M_CHUNK, (j + 1) * M_CHUNK)
            stats, b3 = _chunk_decay(gx[:, sl], backward=d == 1)
            s_ref[0, :, sl] = stats
            b3_ref[0, :, sl] = b3

    pos = pos_ref[0].astype(F32)
    angT = invf_ref[...] * pos
    cosT = jnp.cos(angT)
    sinT = jnp.sin(angT)

    cqnT = _rms(cq, gcq_ref[...]).T.astype(BF16)
    qT = _dot(wuqT_ref[...], cqnT)
    gq = gqcol_ref[...] * (A_DH ** -0.5 * LOG2E)
    for h in range(A_HEADS):
        qh = qT[h * A_DH:(h + 1) * A_DH]
        ms = jnp.sum(qh * qh, axis=0, keepdims=True) * (1.0 / A_DH)
        qn = qh * lax.rsqrt(ms + EPS) * gq
        x1 = qn[A_NOPE:A_NOPE + 32]
        x2 = qn[A_NOPE + 32:A_DH]
        qT_ref[0, h, 0, 0:A_NOPE, :] = qn[0:A_NOPE].astype(BF16)
        qT_ref[0, h, 0, A_NOPE:A_NOPE + 32, :] = (x1 * cosT - x2 * sinT).astype(BF16)
        qT_ref[0, h, 0, A_NOPE + 32:A_DH, :] = (x2 * cosT + x1 * sinT).astype(BF16)

    ckvn = _rms(ckv, gckv_ref[...])
    knope = _dot(ckvn.astype(BF16), wuk_ref[...])
    vT = _dot(wuvT_ref[...], ckvn.T.astype(BF16))
    for h in range(A_HEADS):
        vT_ref[0, h, 0] = vT[h * A_DV:(h + 1) * A_DV].astype(BF16)

    tab = jnp.concatenate([cosT, cosT, -sinT, sinT], axis=0).T
    cc = tab[:, 0:A_ROPE]
    ss = tab[:, A_ROPE:2 * A_ROPE]
    gk = gk_ref[...]
    kpe = tail[:, 0:A_ROPE]
    kpe_sw = pltpu.roll(tail, LANES - 32, axis=1)[:, 0:A_ROPE]
    ss_pe = jnp.sum(kpe * kpe, axis=-1, keepdims=True)
    krope = (kpe * gk[:, A_NOPE:A_DH] * cc + kpe_sw * gk[:, A_DH:A_DH + A_ROPE] * ss)
    for h in range(A_HEADS):
        kn = knope[:, h * A_NOPE:(h + 1) * A_NOPE]
        ms = (jnp.sum(kn * kn, axis=-1, keepdims=True) + ss_pe) * (1.0 / A_DH)
        r = lax.rsqrt(ms + EPS)
        k_ref[0, h, :, 0:A_NOPE] = (kn * r * gk[:, 0:A_NOPE]).astype(BF16)
        k_ref[0, h, :, A_NOPE:A_DH] = (krope * r).astype(BF16)


def _proj_call(xf, pos3, gmix, win, gcq, gckv, wuqT, wuk, wuvT, gqcol, gk, invf, bg, batch, seq):
    n, d = xf.shape
    tm = TOKEN_TILE
    nt = seq // tm
    steps = n // tm

    def full(a):
        return pl.BlockSpec(a.shape, lambda i: (0,) * a.ndim)

    tok = lambda w: pl.BlockSpec((tm, w), lambda i: (i, 0))
    in_specs = [tok(d), pl.BlockSpec((1, 1, tm), lambda i: (i, 0, 0)), full(gmix), full(win),
                full(gcq), full(gckv), full(wuqT), full(wuk), full(wuvT), full(gqcol), full(gk),
                full(invf), full(bg)]
    stat_spec = pl.BlockSpec((1, 32, tm), lambda i: (i // nt, 0, i % nt))
    out_shape = [
        jax.ShapeDtypeStruct((n, 512), F32),
        jax.ShapeDtypeStruct((n, 512), BF16),
        jax.ShapeDtypeStruct((n, 512), F32),
        jax.ShapeDtypeStruct((batch, 32, seq), F32),
        jax.ShapeDtypeStruct((batch, 32, seq), BF16),
        jax.ShapeDtypeStruct((batch, 32, seq), F32),
        jax.ShapeDtypeStruct((batch, 32, seq), BF16),
        jax.ShapeDtypeStruct((batch, A_HEADS, nt, A_DH, tm), BF16),
        jax.ShapeDtypeStruct((batch, A_HEADS, seq, A_DH), BF16),
        jax.ShapeDtypeStruct((batch, A_HEADS, nt, A_DV, tm), BF16),
    ]
    out_specs = [
        tok(512), tok(512), tok(512), stat_spec, stat_spec, stat_spec, stat_spec,
        pl.BlockSpec((1, A_HEADS, 1, A_DH, tm), lambda i: (i // nt, 0, i % nt, 0, 0)),
        pl.BlockSpec((1, A_HEADS, tm, A_DH), lambda i: (i // nt, 0, i % nt, 0)),
        pl.BlockSpec((1, A_HEADS, 1, A_DV, tm), lambda i: (i // nt, 0, i % nt, 0, 0)),
    ]
    return pl.pallas_call(
        _proj_kernel, grid=(steps,), in_specs=in_specs, out_specs=out_specs, out_shape=out_shape,
        compiler_params=pltpu.CompilerParams(dimension_semantics=("arbitrary",),
                                             vmem_limit_bytes=VMEM_LIMIT),
        name="proj",
    )(xf, pos3, gmix, win, gcq, gckv, wuqT, wuk, wuvT, gqcol, gk, invf, bg)


def _conv_kernel(cur_ref, prev_ref, next_ref, w_ref, b_ref, qk_ref, kT_ref, *, tiles_per_seq):
    t = pl.program_id(0) % tiles_per_seq
    tm = cur_ref.shape[0]
    prev = jnp.where(t == 0, 0.0, prev_ref[...])
    nxt = jnp.where(t == tiles_per_seq - 1, 0.0, next_ref[...])
    ext = jnp.concatenate([prev, cur_ref[...], nxt], axis=0)
    cw = w_ref[...]
    u = b_ref[...]
    for j in range(M_CONV):
        off = 8 - M_CONV // 2 + j
        u = u + cw[j:j + 1, :] * ext[off:off + tm, :]
    qk = u * jax.nn.sigmoid(u)
    k = qk[:, 256:512]
    qk_ref[:, 0:256] = (qk[:, 0:256] * (M_DQK ** -0.5)).astype(BF16)
    qk_ref[:, 256:512] = k.astype(BF16)
    kT_ref[0] = k.T.astype(BF16)


def _conv_call(qk, convw, convb, batch, seq):
    n = qk.shape[0]
    tm = TOKEN_TILE
    nt = seq // tm
    r8 = tm // 8
    return pl.pallas_call(
        functools.partial(_conv_kernel, tiles_per_seq=nt),
        grid=(n // tm,),
        in_specs=[
            pl.BlockSpec((tm, 512), lambda i: (i, 0)),
            pl.BlockSpec((8, 512), lambda i: (jnp.maximum(i * r8 - 1, 0), 0)),
            pl.BlockSpec((8, 512), lambda i: (jnp.minimum((i + 1) * r8, n // 8 - 1), 0)),
            pl.BlockSpec(convw.shape, lambda i: (0, 0)),
            pl.BlockSpec(convb.shape, lambda i: (0, 0)),
        ],
        out_specs=[pl.BlockSpec((tm, 512), lambda i: (i, 0)),
                   pl.BlockSpec((1, 256, tm), lambda i: (i // nt, 0, i % nt))],
        out_shape=[jax.ShapeDtypeStruct((n, 512), BF16),
                   jax.ShapeDtypeStruct((batch, 256, seq), BF16)],
        compiler_params=pltpu.CompilerParams(dimension_semantics=("arbitrary",),
                                             vmem_limit_bytes=VMEM_LIMIT),
        name="conv",
    )(qk, qk, qk, convw, convb)


def _mlstm_gates(s_ref, b3_ref, esel, m_sc):
    stats = s_ref[0]
    a, g = stats[16:24], stats[24:32]
    m_prev = m_sc[...]
    m_new = jnp.maximum(a + m_prev, g)
    m_sc[...] = m_new
    return dict(
        w=stats[0:8], r=stats[8:16], m_prev=m_prev,
        dec=jnp.exp(a + m_prev - m_new), add=jnp.exp(g - m_new),
        brep=_dot_tn(b3_ref[0], esel),
    )


def _mlstm_kernel(qkf_ref, kTf_ref, vf_ref, sf_ref, b3f_ref, qkb_ref, kTb_ref, vb_ref, sb_ref,
                  b3b_ref, esel_ref, hf_ref, hb_ref, cf_sc, cb_sc, mf_sc, mb_sc):
    L = M_CHUNK

    @pl.when(pl.program_id(1) == 0)
    def _():
        cf_sc[...] = jnp.zeros_like(cf_sc)
        cb_sc[...] = jnp.zeros_like(cb_sc)
        mf_sc[...] = jnp.zeros_like(mf_sc)
        mb_sc[...] = jnp.zeros_like(mb_sc)

    esel = esel_ref[...]
    row = lax.broadcasted_iota(jnp.int32, (L, L), 0)
    col = lax.broadcasted_iota(jnp.int32, (L, L), 1)
    ones_blk = jnp.ones((L, M_DV), BF16)
    dirs = [
        dict(qk=qkf_ref, kT=kTf_ref, v=vf_ref, out=hf_ref, c=cf_sc, visible=col <= row,
             **_mlstm_gates(sf_ref, b3f_ref, esel, mf_sc)),
        dict(qk=qkb_ref, kT=kTb_ref, v=vb_ref, out=hb_ref, c=cb_sc, visible=col >= row,
             **_mlstm_gates(sb_ref, b3b_ref, esel, mb_sc)),
    ]
    units = [(d, h) for h in range(M_HEADS) for d in dirs]

    grams, v_augs, s_augs = [], [], []
    for d, h in units:
        q_h = d["qk"][0, :, h * M_DQK:(h + 1) * M_DQK]
        k_h = d["qk"][0, :, 256 + h * M_DQK:256 + (h + 1) * M_DQK]
        grams.append(_dot_nt(q_h, k_h))
        v_augs.append(jnp.concatenate([d["v"][0, :, h * M_DV:(h + 1) * M_DV], ones_blk], axis=1))
    for (d, h), v_aug in zip(units, v_augs):
        kTw = (d["kT"][0, h * M_DQK:(h + 1) * M_DQK, :].astype(F32) * d["w"][h:h + 1, :])
        s_augs.append(_dot(kTw.astype(BF16), v_aug))

    for (d, h), gram, v_aug, s_aug in zip(units, grams, v_augs, s_augs):
        q_h = d["qk"][0, :, h * M_DQK:(h + 1) * M_DQK]
        brep = d["brep"][:, h * LANES:(h + 1) * LANES]
        bm = brep + d["m_prev"][h:h + 1, :]
        dmat = jnp.where(d["visible"], brep + d["r"][h:h + 1, :], -jnp.inf)
        m_t = jnp.maximum(bm, jnp.max(dmat, axis=1, keepdims=True))
        p = jnp.exp(dmat - m_t) * gram
        inter = jnp.exp(bm - m_t)
        c_prev = d["c"][h]
        lhs = jnp.concatenate(
            [p.astype(BF16), (inter[:, 0:M_DQK] * q_h.astype(F32)).astype(BF16)], axis=1)
        rhs = jnp.concatenate([v_aug, c_prev.astype(BF16)], axis=0)
        res = _dot(lhs, rhs)
        den = jnp.maximum(jnp.abs(res[:, M_DV:2 * M_DV]), jnp.exp(-m_t))
        d["out"][0, :, h * M_DV:(h + 1) * M_DV] = res[:, 0:M_DV] / den
        dec_h = jnp.concatenate([d["dec"][h:h + 1, :]] * 2, axis=1)
        add_h = jnp.concatenate([d["add"][h:h + 1, :]] * 2, axis=1)
        d["c"][h] = dec_h * c_prev + add_h * s_aug


def _mlstm_call(qk3, kT3, v3, sf, b3f, sb, b3b, esel):
    batch, seq, _ = qk3.shape
    L = M_CHUNK
    nc = seq // L
    fw_rows = lambda w: pl.BlockSpec((1, L, w), lambda b, c: (b, c, 0))
    bw_rows = lambda w: pl.BlockSpec((1, L, w), lambda b, c: (b, nc - 1 - c, 0))
    fw_lanes = lambda r: pl.BlockSpec((1, r, L), lambda b, c: (b, 0, c))
    bw_lanes = lambda r: pl.BlockSpec((1, r, L), lambda b, c: (b, 0, nc - 1 - c))
    state = pltpu.VMEM((M_HEADS, M_DQK, 2 * M_DV), F32)
    return pl.pallas_call(
        _mlstm_kernel,
        grid=(batch, nc),
        in_specs=[fw_rows(512), fw_lanes(256), fw_rows(512), fw_lanes(32), fw_lanes(32),
                  bw_rows(512), bw_lanes(256), bw_rows(512), bw_lanes(32), bw_lanes(32),
                  pl.BlockSpec(esel.shape, lambda b, c: (0, 0))],
        out_specs=[fw_rows(512), bw_rows(512)],
        out_shape=[jax.ShapeDtypeStruct((batch, seq, 512), F32)] * 2,
        scratch_shapes=[state, state, pltpu.VMEM((8, L), F32), pltpu.VMEM((8, L), F32)],
        compiler_params=pltpu.CompilerParams(dimension_semantics=("arbitrary", "arbitrary"),
                                             vmem_limit_bytes=VMEM_LIMIT),
        name="mlstm",
    )(qk3, kT3, v3, sf, b3f, qk3, kT3, v3, sb, b3b, esel)


def _attn_kernel(qT_ref, k_ref, vT_ref, o_ref, s_sc, cm_sc, acc_sc, *, n_kv):
    tk = KV_TILE
    sub = tk // TOKEN_TILE
    qT = qT_ref[0, 0, 0]
    tq = qT.shape[1]

    def scores(j, slot):
        sT = _dot(k_ref[0, 0, j * tk:(j + 1) * tk, :], qT)
        s_sc[slot] = sT
        cm_sc[slot] = jnp.max(sT, axis=0, keepdims=True)

    m = jnp.full((1, tq), -jnp.inf, F32)
    l = jnp.zeros((1, tq), F32)
    scores(0, 0)
    for j in range(n_kv):
        slot = j % 2
        if j + 1 < n_kv:
            scores(j + 1, 1 - slot)
        m_new = jnp.maximum(m, cm_sc[slot])
        alpha = jnp.exp2(m - m_new)
        p = jnp.exp2(s_sc[slot] - m_new)
        l = alpha * l + jnp.sum(p, axis=0, keepdims=True)
        vt = jnp.concatenate([vT_ref[0, 0, j * sub + i] for i in range(sub)], axis=1)
        pv = _dot(vt, p.astype(BF16))
        if j == 0:
            acc_sc[...] = pv
        else:
            acc_sc[...] = alpha * acc_sc[...] + pv
        m = m_new
    o_ref[0] = (acc_sc[...] / l).T.astype(o_ref.dtype)


def _attn_call(qT, k, vT):
    batch, heads, nt, _, tm = qT.shape
    seq = k.shape[2]
    return pl.pallas_call(
        functools.partial(_attn_kernel, n_kv=seq // KV_TILE),
        grid=(batch, heads, nt),
        in_specs=[
            pl.BlockSpec((1, 1, 1, A_DH, tm), lambda b, h, i: (b, h, i, 0, 0)),
            pl.BlockSpec((1, 1, seq, A_DH), lambda b, h, i: (b, h, 0, 0)),
            pl.BlockSpec((1, 1, nt, A_DV, tm), lambda b, h, i: (b, h, 0, 0, 0)),
        ],
        out_specs=pl.BlockSpec((1, tm, A_DV), lambda b, h, i: (b, i, h)),
        out_shape=jax.ShapeDtypeStruct((batch, seq, heads * A_DV), BF16),
        scratch_shapes=[pltpu.VMEM((2, KV_TILE, tm), F32), pltpu.VMEM((2, 1, tm), F32),
                        pltpu.VMEM((A_DV, tm), F32)],
        compiler_params=pltpu.CompilerParams(
            dimension_semantics=("arbitrary", "arbitrary", "arbitrary"),
            vmem_limit_bytes=VMEM_LIMIT),
        name="attn",
    )(qT, k, vT)


def _out_kernel(x_ref, hf_ref, hb_ref, o_ref, yb_ref, gml_ref, wout_ref, gffn_ref, w1_ref, w2_ref,
                y_ref):
    x = x_ref[...]
    hsum = hf_ref[...] + hb_ref[...]
    gate = jax.nn.sigmoid(o_ref[...])
    gml = gml_ref[...]
    parts = []
    for h in range(M_HEADS):
        sl = slice(h * M_DV, (h + 1) * M_DV)
        parts.append((gate[:, sl] * _rms(hsum[:, sl], gml[:, sl])).astype(BF16))
    y_a = jnp.concatenate(parts, axis=1)
    n_a = y_a.shape[1]
    x1 = x + _dot(y_a, wout_ref[0:n_a, :]) + _dot(yb_ref[...], wout_ref[n_a:, :])
    hn = _rms(x1, gffn_ref[...]).astype(BF16)
    y_ref[...] = x1
    d_ff = w1_ref.shape[1]
    for j in range(d_ff // FF_CHUNK):
        sl = slice(j * FF_CHUNK, (j + 1) * FF_CHUNK)
        t = jnp.maximum(_dot(hn, w1_ref[:, sl]), 0.0)
        y_ref[...] += _dot((t * t).astype(BF16), w2_ref[sl, :])


def _out_call(xf, hf, hb, o, yb, gml, wout, gffn, w1, w2):
    n, d = xf.shape
    tm = TOKEN_TILE

    def full(a):
        return pl.BlockSpec(a.shape, lambda i: (0,) * a.ndim)

    tok = lambda w: pl.BlockSpec((tm, w), lambda i: (i, 0))
    in_specs = [tok(d), tok(512), tok(512), tok(512), tok(512), full(gml), full(wout), full(gffn),
                full(w1), full(w2)]
    return pl.pallas_call(
        _out_kernel, grid=(n // tm,), in_specs=in_specs, out_specs=tok(d),
        out_shape=jax.ShapeDtypeStruct((n, d), F32),
        compiler_params=pltpu.CompilerParams(dimension_semantics=("arbitrary",),
                                             vmem_limit_bytes=VMEM_LIMIT),
        name="out",
    )(xf, hf, hb, o, yb, gml, wout, gffn, w1, w2)


def _layer(xf, pos3, invf, batch, seq, g_mix_norm, w_in, conv_w, conv_b, b_gates, g_mlstm_out,
           g_cq, g_ckv, w_uq, w_ukv, g_q, g_k, w_out, g_ffn_norm, w_ff1, w_ff2):
    d = xf.shape[1]
    c0 = 2 * 256 + 2 * 512
    gates_w = w_in[:, c0:c0 + N_GATES]
    cq_w = w_in[:, c0 + N_GATES:c0 + N_GATES + 256]
    ckv_w = w_in[:, c0 + N_GATES + 256:c0 + N_GATES + 384]
    kpe_w = w_in[:, c0 + N_GATES + 384:c0 + N_GATES + 448]
    win = jnp.concatenate(
        [w_in[:, 0:c0], cq_w, ckv_w, kpe_w, kpe_w[:, 0:32], gates_w,
         jnp.zeros((d, LANES - 96 - N_GATES), w_in.dtype)], axis=1).astype(BF16)
    wuqT = w_uq.T.astype(BF16)
    ukv = w_ukv.reshape(w_ukv.shape[0], A_HEADS, A_NOPE + A_DV)
    wuk = ukv[:, :, :A_NOPE].reshape(w_ukv.shape[0], A_HEADS * A_NOPE).astype(BF16)
    wuvT = ukv[:, :, A_NOPE:].reshape(w_ukv.shape[0], A_HEADS * A_DV).T.astype(BF16)
    gqcol = g_q.reshape(A_DH, 1)
    gk = jnp.concatenate([g_k, g_k[A_NOPE + 32:], g_k[A_NOPE:A_NOPE + 32]]).reshape(1, -1)

    qk, v, o, sf, b3f, sb, b3b, qT, k, vT = _proj_call(
        xf, pos3, g_mix_norm.reshape(1, -1), win, g_cq.reshape(1, -1), g_ckv.reshape(1, -1),
        wuqT, wuk, wuvT, gqcol, gk, invf, b_gates.reshape(2, 8, 1), batch, seq)

    qkc, kT = _conv_call(qk, conv_w, conv_b.reshape(1, -1), batch, seq)
    esel = jnp.tile(jnp.repeat(jnp.eye(8, M_HEADS, dtype=BF16), LANES, axis=1), (4, 1))
    hf, hb = _mlstm_call(qkc.reshape(batch, seq, 512), kT, v.reshape(batch, seq, 512),
                         sf, b3f, sb, b3b, esel)
    yb = _attn_call(qT, k, vT)
    return _out_call(xf, hf.reshape(batch * seq, -1), hb.reshape(batch * seq, -1), o,
                     yb.reshape(batch * seq, -1), g_mlstm_out.reshape(1, -1), w_out.astype(BF16),
                     g_ffn_norm.reshape(1, -1), w_ff1.astype(BF16), w_ff2.astype(BF16))


def kernel(x, positions, g_mix_norm, w_in, conv_w, conv_b, b_gates, g_mlstm_out, g_cq, g_ckv,
           w_uq, w_ukv, g_q, g_k, w_out, g_ffn_norm, w_ff1, w_ff2):
    batch, seq, d = x.shape
    assert seq % TOKEN_TILE == 0 and seq % M_CHUNK == 0
    inv_freq = ROPE_THETA ** (-jnp.arange(0, A_ROPE, 2, dtype=F32) / A_ROPE)
    invf = inv_freq.reshape(A_ROPE // 2, 1)
    pos3 = positions.reshape(batch * seq // TOKEN_TILE, 1, TOKEN_TILE)
    xf = x.reshape(batch * seq, d)
    for l in range(w_in.shape[0]):
        xf = _layer(xf, pos3, invf, batch, seq, g_mix_norm[l], w_in[l], conv_w[l], conv_b[l],
                    b_gates[l], g_mlstm_out[l], g_cq[l], g_ckv[l], w_uq[l], w_ukv[l], g_q[l],
                    g_k[l], w_out[l], g_ffn_norm[l], w_ff1[l], w_ff2[l])
    return xf.reshape(batch, seq, d)
```

```python
import functools
import math

import jax
import jax.numpy as jnp
from jax import lax
from jax.experimental import pallas as pl
from jax.experimental.pallas import tpu as pltpu

F32 = jnp.float32
BF16 = jnp.bfloat16

EPS = 1e-6
ROPE_THETA = 10000.0
M_HEADS = 4
M_DQK = 64
M_DV = 128
M_CONV = 5
M_CHUNK = 128
A_HEADS = 4
A_NOPE = 128
A_ROPE = 64
A_DV = 128
A_DH = A_NOPE + A_ROPE
N_GATES = 4 * M_HEADS

LANES = 128
TOKEN_TILE = 512
KV_TILE = 1024
Q_TILE = 1024
FF_CHUNK = 1024
VMEM_LIMIT = 56 * 1024 * 1024

LOG2E = math.log2(math.e)


def _dot(a, b):
    return jnp.dot(a, b, preferred_element_type=F32)


def _dot_nt(a, b):
    return lax.dot_general(a, b, (((1,), (1,)), ((), ())), preferred_element_type=F32)


def _dot_tn(a, b):
    return lax.dot_general(a, b, (((0,), (0,)), ((), ())), preferred_element_type=F32)


def _rms(x, g):
    ms = jnp.mean(x * x, axis=-1, keepdims=True)
    return x * lax.rsqrt(ms + EPS) * g


def _split3(y):
    hi = y.astype(BF16).astype(F32)
    mid = (y - hi).astype(BF16).astype(F32)
    lo = y - hi - mid
    return jnp.concatenate([hi, mid, lo], axis=0).astype(BF16)


def _chunk_decay(x, *, backward):
    L = M_CHUNK
    row = lax.broadcasted_iota(jnp.int32, (L, L), 0)
    col = lax.broadcasted_iota(jnp.int32, (L, L), 1)
    umat = jnp.where((row >= col) if backward else (row <= col), 1.0, 0.0).astype(BF16)
    lf = jax.nn.log_sigmoid(x)
    cum3 = _dot(_split3(lf), umat)
    cum = cum3[0:8] + cum3[8:16] + cum3[16:24]
    a = jnp.broadcast_to(jnp.sum(lf, axis=1, keepdims=True), (8, L))
    b = pltpu.roll(cum, 4, axis=0)
    a = pltpu.roll(a, 4, axis=0)
    w_log = a - b + x
    g = jnp.broadcast_to(jnp.max(w_log, axis=1, keepdims=True), (8, L))
    stats = jnp.concatenate([jnp.exp(w_log - g), x - b, a, g], axis=0)
    b3 = jnp.concatenate([_split3(b), jnp.zeros((8, L), BF16)], axis=0)
    return stats, b3


def _proj_kernel(x_ref, pos_ref, gmix_ref, win_ref, gcq_ref, gckv_ref, wuqT_ref, wuk_ref,
                 wuvT_ref, gqcol_ref, gk_ref, invf_ref, bg_ref,
                 qk_ref, v_ref, o_ref, sf_ref, b3f_ref, sb_ref, b3b_ref, qT_ref, k_ref, vT_ref):
    x = x_ref[...]
    hn = _rms(x, gmix_ref[...]).astype(BF16)
    z = _dot(hn, win_ref[...])
    qk_ref[...] = z[:, 0:512]
    v_ref[...] = z[:, 512:1024].astype(BF16)
    o_ref[...] = z[:, 1024:1536]
    cq = z[:, 1536:1792]
    ckv = z[:, 1792:1920]
    tail = z[:, 1920:2048]

    tailT = tail.T
    for d, (s_ref, b3_ref) in enumerate(((sf_ref, b3f_ref), (sb_ref, b3b_ref))):
        gx = tailT[96 + 8 * d:104 + 8 * d] + bg_ref[d]
        for j in range(gx.shape[1] // M_CHUNK):
            sl = slice(j * M_CHUNK, (j + 1) * M_CHUNK)
            stats, b3 = _chunk_decay(gx[:, sl], backward=d == 1)
            s_ref[0, :, sl] = stats
            b3_ref[0, :, sl] = b3

    pos = pos_ref[0].astype(F32)
    angT = invf_ref[...] * pos
    cosT = jnp.cos(angT)
    sinT = jnp.sin(angT)

    cqnT = _rms(cq, gcq_ref[...]).T.astype(BF16)
    qT = _dot(wuqT_ref[...], cqnT)
    gq = gqcol_ref[...] * (A_DH ** -0.5 * LOG2E)
    for h in range(A_HEADS):
        qh = qT[h * A_DH:(h + 1) * A_DH]
        ms = jnp.sum(qh * qh, axis=0, keepdims=True) * (1.0 / A_DH)
        qn = qh * lax.rsqrt(ms + EPS) * gq
        x1 = qn[A_NOPE:A_NOPE + 32]
        x2 = qn[A_NOPE + 32:A_DH]
        qT_ref[0, h, 0, 0:A_NOPE, :] = qn[0:A_NOPE].astype(BF16)
        qT_ref[0, h, 0, A_NOPE:A_NOPE + 32, :] = (x1 * cosT - x2 * sinT).astype(BF16)
        qT_ref[0, h, 0, A_NOPE + 32:A_DH, :] = (x2 * cosT + x1 * sinT).astype(BF16)

    ckvn = _rms(ckv, gckv_ref[...])
    knope = _dot(ckvn.astype(BF16), wuk_ref[...])
    vT = _dot(wuvT_ref[...], ckvn.T.astype(BF16))
    for h in range(A_HEADS):
        vT_ref[0, h, 0] = vT[h * A_DV:(h + 1) * A_DV].astype(BF16)

    tab = jnp.concatenate([cosT, cosT, -sinT, sinT], axis=0).T
    cc = tab[:, 0:A_ROPE]
    ss = tab[:, A_ROPE:2 * A_ROPE]
    gk = gk_ref[...]
    kpe = tail[:, 0:A_ROPE]
    kpe_sw = pltpu.roll(tail, LANES - 32, axis=1)[:, 0:A_ROPE]
    ss_pe = jnp.sum(kpe * kpe, axis=-1, keepdims=True)
    krope = (kpe * gk[:, A_NOPE:A_DH] * cc + kpe_sw * gk[:, A_DH:A_DH + A_ROPE] * ss)
    for h in range(A_HEADS):
        kn = knope[:, h * A_NOPE:(h + 1) * A_NOPE]
        ms = (jnp.sum(kn * kn, axis=-1, keepdims=True) + ss_pe) * (1.0 / A_DH)
        r = lax.rsqrt(ms + EPS)
        k_ref[0, h, :, 0:A_NOPE] = (kn * r * gk[:, 0:A_NOPE]).astype(BF16)
        k_ref[0, h, :, A_NOPE:A_DH] = (krope * r).astype(BF16)


def _proj_call(xf, pos3, gmix, win, gcq, gckv, wuqT, wuk, wuvT, gqcol, gk, invf, bg, batch, seq):
    n, d = xf.shape
    tm = TOKEN_TILE
    nt = seq // tm
    steps = n // tm

    def full(a):
        return pl.BlockSpec(a.shape, lambda i: (0,) * a.ndim)

    tok = lambda w: pl.BlockSpec((tm, w), lambda i: (i, 0))
    in_specs = [tok(d), pl.BlockSpec((1, 1, tm), lambda i: (i, 0, 0)), full(gmix), full(win),
                full(gcq), full(gckv), full(wuqT), full(wuk), full(wuvT), full(gqcol), full(gk),
                full(invf), full(bg)]
    stat_spec = pl.BlockSpec((1, 32, tm), lambda i: (i // nt, 0, i % nt))
    out_shape = [
        jax.ShapeDtypeStruct((n, 512), F32),
        jax.ShapeDtypeStruct((n, 512), BF16),
        jax.ShapeDtypeStruct((n, 512), F32),
        jax.ShapeDtypeStruct((batch, 32, seq), F32),
        jax.ShapeDtypeStruct((batch, 32, seq), BF16),
        jax.ShapeDtypeStruct((batch, 32, seq), F32),
        jax.ShapeDtypeStruct((batch, 32, seq), BF16),
        jax.ShapeDtypeStruct((batch, A_HEADS, nt, A_DH, tm), BF16),
        jax.ShapeDtypeStruct((batch, A_HEADS, seq, A_DH), BF16),
        jax.ShapeDtypeStruct((batch, A_HEADS, nt, A_DV, tm), BF16),
    ]
    out_specs = [
        tok(512), tok(512), tok(512), stat_spec, stat_spec, stat_spec, stat_spec,
        pl.BlockSpec((1, A_HEADS, 1, A_DH, tm), lambda i: (i // nt, 0, i % nt, 0, 0)),
        pl.BlockSpec((1, A_HEADS, tm, A_DH), lambda i: (i // nt, 0, i % nt, 0)),
        pl.BlockSpec((1, A_HEADS, 1, A_DV, tm), lambda i: (i // nt, 0, i % nt, 0, 0)),
    ]
    return pl.pallas_call(
        _proj_kernel, grid=(steps,), in_specs=in_specs, out_specs=out_specs, out_shape=out_shape,
        compiler_params=pltpu.CompilerParams(dimension_semantics=("arbitrary",),
                                             vmem_limit_bytes=VMEM_LIMIT),
        name="proj",
    )(xf, pos3, gmix, win, gcq, gckv, wuqT, wuk, wuvT, gqcol, gk, invf, bg)


def _conv_kernel(cur_ref, prev_ref, next_ref, w_ref, b_ref, qk_ref, kT_ref, *, tiles_per_seq):
    t = pl.program_id(0) % tiles_per_seq
    tm = cur_ref.shape[0]
    prev = jnp.where(t == 0, 0.0, prev_ref[...])
    nxt = jnp.where(t == tiles_per_seq - 1, 0.0, next_ref[...])
    ext = jnp.concatenate([prev, cur_ref[...], nxt], axis=0)
    cw = w_ref[...]
    u = b_ref[...]
    for j in range(M_CONV):
        off = 8 - M_CONV // 2 + j
        u = u + cw[j:j + 1, :] * ext[off:off + tm, :]
    qk = u * jax.nn.sigmoid(u)
    k = qk[:, 256:512]
    qk_ref[:, 0:256] = (qk[:, 0:256] * (M_DQK ** -0.5)).astype(BF16)
    qk_ref[:, 256:512] = k.astype(BF16)
    kT_ref[0] = k.T.astype(BF16)


def _conv_call(qk, convw, convb, batch, seq):
    n = qk.shape[0]
    tm = TOKEN_TILE
    nt = seq // tm
    r8 = tm // 8
    return pl.pallas_call(
        functools.partial(_conv_kernel, tiles_per_seq=nt),
        grid=(n // tm,),
        in_specs=[
            pl.BlockSpec((tm, 512), lambda i: (i, 0)),
            pl.BlockSpec((8, 512), lambda i: (jnp.maximum(i * r8 - 1, 0), 0)),
            pl.BlockSpec((8, 512), lambda i: (jnp.minimum((i + 1) * r8, n // 8 - 1), 0)),
            pl.BlockSpec(convw.shape, lambda i: (0, 0)),
            pl.BlockSpec(convb.shape, lambda i: (0, 0)),
        ],
        out_specs=[pl.BlockSpec((tm, 512), lambda i: (i, 0)),
                   pl.BlockSpec((1, 256, tm), lambda i: (i // nt, 0, i % nt))],
        out_shape=[jax.ShapeDtypeStruct((n, 512), BF16),
                   jax.ShapeDtypeStruct((batch, 256, seq), BF16)],
        compiler_params=pltpu.CompilerParams(dimension_semantics=("arbitrary",),
                                             vmem_limit_bytes=VMEM_LIMIT),
        name="conv",
    )(qk, qk, qk, convw, convb)


def _mlstm_gates(s_ref, b3_ref, esel, m_sc):
    stats = s_ref[0]
    a, g = stats[16:24], stats[24:32]
    m_prev = m_sc[...]
    m_new = jnp.maximum(a + m_prev, g)
    m_sc[...] = m_new
    return dict(
        w=stats[0:8], r=stats[8:16], m_prev=m_prev,
        dec=jnp.exp(a + m_prev - m_new), add=jnp.exp(g - m_new),
        brep=_dot_tn(b3_ref[0], esel),
    )


def _mlstm_kernel(qkf_ref, kTf_ref, vf_ref, sf_ref, b3f_ref, qkb_ref, kTb_ref, vb_ref, sb_ref,
                  b3b_ref, esel_ref, hf_ref, hb_ref, cf_sc, cb_sc, mf_sc, mb_sc):
    L = M_CHUNK

    @pl.when(pl.program_id(1) == 0)
    def _():
        cf_sc[...] = jnp.zeros_like(cf_sc)
        cb_sc[...] = jnp.zeros_like(cb_sc)
        mf_sc[...] = jnp.zeros_like(mf_sc)
        mb_sc[...] = jnp.zeros_like(mb_sc)

    esel = esel_ref[...]
    row = lax.broadcasted_iota(jnp.int32, (L, L), 0)
    col = lax.broadcasted_iota(jnp.int32, (L, L), 1)
    ones_blk = jnp.ones((L, M_DV), BF16)
    dirs = [
        dict(qk=qkf_ref, kT=kTf_ref, v=vf_ref, out=hf_ref, c=cf_sc, visible=col <= row,
             **_mlstm_gates(sf_ref, b3f_ref, esel, mf_sc)),
        dict(qk=qkb_ref, kT=kTb_ref, v=vb_ref, out=hb_ref, c=cb_sc, visible=col >= row,
             **_mlstm_gates(sb_ref, b3b_ref, esel, mb_sc)),
    ]
    units = [(d, h) for h in range(M_HEADS) for d in dirs]

    grams, v_augs, s_augs = [], [], []
    for d, h in units:
        q_h = d["qk"][0, :, h * M_DQK:(h + 1) * M_DQK]
        k_h = d["qk"][0, :, 256 + h * M_DQK:256 + (h + 1) * M_DQK]
        grams.append(_dot_nt(q_h, k_h))
        v_augs.append(jnp.concatenate([d["v"][0, :, h * M_DV:(h + 1) * M_DV], ones_blk], axis=1))
    for (d, h), v_aug in zip(units, v_augs):
        kTw = (d["kT"][0, h * M_DQK:(h + 1) * M_DQK, :].astype(F32) * d["w"][h:h + 1, :])
        s_augs.append(_dot(kTw.astype(BF16), v_aug))

    for (d, h), gram, v_aug, s_aug in zip(units, grams, v_augs, s_augs):
        q_h = d["qk"][0, :, h * M_DQK:(h + 1) * M_DQK]
        brep = d["brep"][:, h * LANES:(h + 1) * LANES]
        bm = brep + d["m_prev"][h:h + 1, :]
        dmat = jnp.where(d["visible"], brep + d["r"][h:h + 1, :], -jnp.inf)
        m_t = jnp.maximum(bm, jnp.max(dmat, axis=1, keepdims=True))
        p = jnp.exp(dmat - m_t) * gram
        inter = jnp.exp(bm - m_t)
        c_prev = d["c"][h]
        lhs = jnp.concatenate(
            [p.astype(BF16), (inter[:, 0:M_DQK] * q_h.astype(F32)).astype(BF16)], axis=1)
        rhs = jnp.concatenate([v_aug, c_prev.astype(BF16)], axis=0)
        res = _dot(lhs, rhs)
        den = jnp.maximum(jnp.abs(res[:, M_DV:2 * M_DV]), jnp.exp(-m_t))
        d["out"][0, :, h * M_DV:(h + 1) * M_DV] = res[:, 0:M_DV] / den
        dec_h = jnp.concatenate([d["dec"][h:h + 1, :]] * 2, axis=1)
        add_h = jnp.concatenate([d["add"][h:h + 1, :]] * 2, axis=1)
        d["c"][h] = dec_h * c_prev + add_h * s_aug


def _mlstm_call(qk3, kT3, v3, sf, b3f, sb, b3b, esel):
    batch, seq, _ = qk3.shape
    L = M_CHUNK
    nc = seq // L
    fw_rows = lambda w: pl.BlockSpec((1, L, w), lambda b, c: (b, c, 0))
    bw_rows = lambda w: pl.BlockSpec((1, L, w), lambda b, c: (b, nc - 1 - c, 0))
    fw_lanes = lambda r: pl.BlockSpec((1, r, L), lambda b, c: (b, 0, c))
    bw_lanes = lambda r: pl.BlockSpec((1, r, L), lambda b, c: (b, 0, nc - 1 - c))
    state = pltpu.VMEM((M_HEADS, M_DQK, 2 * M_DV), F32)
    return pl.pallas_call(
        _mlstm_kernel,
        grid=(batch, nc),
        in_specs=[fw_rows(512), fw_lanes(256), fw_rows(512), fw_lanes(32), fw_lanes(32),
                  bw_rows(512), bw_lanes(256), bw_rows(512), bw_lanes(32), bw_lanes(32),
                  pl.BlockSpec(esel.shape, lambda b, c: (0, 0))],
        out_specs=[fw_rows(512), bw_rows(512)],
        out_shape=[jax.ShapeDtypeStruct((batch, seq, 512), F32)] * 2,
        scratch_shapes=[state, state, pltpu.VMEM((8, L), F32), pltpu.VMEM((8, L), F32)],
        compiler_params=pltpu.CompilerParams(dimension_semantics=("arbitrary", "arbitrary"),
                                             vmem_limit_bytes=VMEM_LIMIT),
        name="mlstm",
    )(qk3, kT3, v3, sf, b3f, qk3, kT3, v3, sb, b3b, esel)


def _attn_kernel(qT_ref, k_ref, vT_ref, o_ref, s_sc, cm_sc, acc_sc, *, n_kv):
    tk = KV_TILE
    sub = tk // TOKEN_TILE
    qT = jnp.concatenate([qT_ref[0, 0, i] for i in range(qT_ref.shape[2])], axis=1)
    tq = qT.shape[1]

    def scores(j, slot):
        sT = _dot(k_ref[0, 0, j * tk:(j + 1) * tk, :], qT)
        s_sc[slot] = sT
        cm_sc[slot] = jnp.max(sT, axis=0, keepdims=True)

    m = jnp.full((1, tq), -jnp.inf, F32)
    l = jnp.zeros((1, tq), F32)
    scores(0, 0)
    for j in range(n_kv):
        slot = j % 2
        if j + 1 < n_kv:
            scores(j + 1, 1 - slot)
        m_new = jnp.maximum(m, cm_sc[slot])
        alpha = jnp.exp2(m - m_new)
        p = jnp.exp2(s_sc[slot] - m_new)
        l = alpha * l + jnp.sum(p, axis=0, keepdims=True)
        vt = jnp.concatenate([vT_ref[0, 0, j * sub + i] for i in range(sub)], axis=1)
        pv = _dot(vt, p.astype(BF16))
        if j == 0:
            acc_sc[...] = pv
        else:
            acc_sc[...] = alpha * acc_sc[...] + pv
        m = m_new
    o_ref[0] = (acc_sc[...] / l).T.astype(o_ref.dtype)


def _attn_call(qT, k, vT):
    batch, heads, nt, _, tm = qT.shape
    seq = k.shape[2]
    tq = Q_TILE
    if tq <= tm:
        per = tm // tq
        q_spec = pl.BlockSpec((1, 1, 1, A_DH, tq), lambda b, h, i: (b, h, i // per, 0, i % per))
    else:
        q_spec = pl.BlockSpec((1, 1, tq // tm, A_DH, tm), lambda b, h, i: (b, h, i, 0, 0))
    return pl.pallas_call(
        functools.partial(_attn_kernel, n_kv=seq // KV_TILE),
        grid=(batch, heads, seq // tq),
        in_specs=[
            q_spec,
            pl.BlockSpec((1, 1, seq, A_DH), lambda b, h, i: (b, h, 0, 0)),
            pl.BlockSpec((1, 1, nt, A_DV, tm), lambda b, h, i: (b, h, 0, 0, 0)),
        ],
        out_specs=pl.BlockSpec((1, tq, A_DV), lambda b, h, i: (b, i, h)),
        out_shape=jax.ShapeDtypeStruct((batch, seq, heads * A_DV), BF16),
        scratch_shapes=[pltpu.VMEM((2, KV_TILE, tq), F32), pltpu.VMEM((2, 1, tq), F32),
                        pltpu.VMEM((A_DV, tq), F32)],
        compiler_params=pltpu.CompilerParams(
            dimension_semantics=("arbitrary", "arbitrary", "arbitrary"),
            vmem_limit_bytes=VMEM_LIMIT),
        name="attn",
    )(qT, k, vT)


def _out_kernel(x_ref, hf_ref, hb_ref, o_ref, yb_ref, gml_ref, wout_ref, gffn_ref, w1_ref, w2_ref,
                y_ref):
    x = x_ref[...]
    hsum = hf_ref[...] + hb_ref[...]
    gate = jax.nn.sigmoid(o_ref[...])
    gml = gml_ref[...]
    parts = []
    for h in range(M_HEADS):
        sl = slice(h * M_DV, (h + 1) * M_DV)
        parts.append((gate[:, sl] * _rms(hsum[:, sl], gml[:, sl])).astype(BF16))
    y_a = jnp.concatenate(parts, axis=1)
    n_a = y_a.shape[1]
    x1 = x + _dot(y_a, wout_ref[0:n_a, :]) + _dot(yb_ref[...], wout_ref[n_a:, :])
    hn = _rms(x1, gffn_ref[...]).astype(BF16)
    y_ref[...] = x1
    d_ff = w1_ref.shape[1]
    for j in range(d_ff // FF_CHUNK):
        sl = slice(j * FF_CHUNK, (j + 1) * FF_CHUNK)
        t = jnp.maximum(_dot(hn, w1_ref[:, sl]), 0.0)
        y_ref[...] += _dot((t * t).astype(BF16), w2_ref[sl, :])


def _out_call(xf, hf, hb, o, yb, gml, wout, gffn, w1, w2):
    n, d = xf.shape
    tm = TOKEN_TILE

    def full(a):
        return pl.BlockSpec(a.shape, lambda i: (0,) * a.ndim)

    tok = lambda w: pl.BlockSpec((tm, w), lambda i: (i, 0))
    in_specs = [tok(d), tok(512), tok(512), tok(512), tok(512), full(gml), full(wout), full(gffn),
                full(w1), full(w2)]
    return pl.pallas_call(
        _out_kernel, grid=(n // tm,), in_specs=in_specs, out_specs=tok(d),
        out_shape=jax.ShapeDtypeStruct((n, d), F32),
        compiler_params=pltpu.CompilerParams(dimension_semantics=("arbitrary",),
                                             vmem_limit_bytes=VMEM_LIMIT),
        name="out",
    )(xf, hf, hb, o, yb, gml, wout, gffn, w1, w2)


def _layer(xf, pos3, invf, batch, seq, g_mix_norm, w_in, conv_w, conv_b, b_gates, g_mlstm_out,
           g_cq, g_ckv, w_uq, w_ukv, g_q, g_k, w_out, g_ffn_norm, w_ff1, w_ff2):
    d = xf.shape[1]
    c0 = 2 * 256 + 2 * 512
    gates_w = w_in[:, c0:c0 + N_GATES]
    cq_w = w_in[:, c0 + N_GATES:c0 + N_GATES + 256]
    ckv_w = w_in[:, c0 + N_GATES + 256:c0 + N_GATES + 384]
    kpe_w = w_in[:, c0 + N_GATES + 384:c0 + N_GATES + 448]
    win = jnp.concatenate(
        [w_in[:, 0:c0], cq_w, ckv_w, kpe_w, kpe_w[:, 0:32], gates_w,
         jnp.zeros((d, LANES - 96 - N_GATES), w_in.dtype)], axis=1).astype(BF16)
    wuqT = w_uq.T.astype(BF16)
    ukv = w_ukv.reshape(w_ukv.shape[0], A_HEADS, A_NOPE + A_DV)
    wuk = ukv[:, :, :A_NOPE].reshape(w_ukv.shape[0], A_HEADS * A_NOPE).astype(BF16)
    wuvT = ukv[:, :, A_NOPE:].reshape(w_ukv.shape[0], A_HEADS * A_DV).T.astype(BF16)
    gqcol = g_q.reshape(A_DH, 1)
    gk = jnp.concatenate([g_k, g_k[A_NOPE + 32:], g_k[A_NOPE:A_NOPE + 32]]).reshape(1, -1)

    qk, v, o, sf, b3f, sb, b3b, qT, k, vT = _proj_call(
        xf, pos3, g_mix_norm.reshape(1, -1), win, g_cq.reshape(1, -1), g_ckv.reshape(1, -1),
        wuqT, wuk, wuvT, gqcol, gk, invf, b_gates.reshape(2, 8, 1), batch, seq)

    qkc, kT = _conv_call(qk, conv_w, conv_b.reshape(1, -1), batch, seq)
    esel = jnp.tile(jnp.repeat(jnp.eye(8, M_HEADS, dtype=BF16), LANES, axis=1), (4, 1))
    hf, hb = _mlstm_call(qkc.reshape(batch, seq, 512), kT, v.reshape(batch, seq, 512),
                         sf, b3f, sb, b3b, esel)
    yb = _attn_call(qT, k, vT)
    return _out_call(xf, hf.reshape(batch * seq, -1), hb.reshape(batch * seq, -1), o,
                     yb.reshape(batch * seq, -1), g_mlstm_out.reshape(1, -1), w_out.astype(BF16),
                     g_ffn_norm.reshape(1, -1), w_ff1.astype(BF16), w_ff2.astype(BF16))


def kernel(x, positions, g_mix_norm, w_in, conv_w, conv_b, b_gates, g_mlstm_out, g_cq, g_ckv,
           w_uq, w_ukv, g_q, g_k, w_out, g_ffn_norm, w_ff1, w_ff2):
    batch, seq, d = x.shape
    assert seq % TOKEN_TILE == 0 and seq % M_CHUNK == 0
    inv_freq = ROPE_THETA ** (-jnp.arange(0, A_ROPE, 2, dtype=F32) / A_ROPE)
    invf = inv_freq.reshape(A_ROPE // 2, 1)
    pos3 = positions.reshape(batch * seq // TOKEN_TILE, 1, TOKEN_TILE)
    xf = x.reshape(batch * seq, d)
    for l in range(w_in.shape[0]):
        xf = _layer(xf, pos3, invf, batch, seq, g_mix_norm[l], w_in[l], conv_w[l], conv_b[l],
                    b_gates[l], g_mlstm_out[l], g_cq[l], g_ckv[l], w_uq[l], w_ukv[l], g_q[l],
                    g_k[l], w_out[l], g_ffn_norm[l], w_ff1[l], w_ff2[l])
    return xf.reshape(batch, seq, d)
```

```python
import functools
import math

import jax
import jax.numpy as jnp
from jax import lax
from jax.experimental import pallas as pl
from jax.experimental.pallas import tpu as pltpu

F32 = jnp.float32
BF16 = jnp.bfloat16

EPS = 1e-6
ROPE_THETA = 10000.0
M_HEADS = 4
M_DQK = 64
M_DV = 128
M_CONV = 5
M_CHUNK = 128
A_HEADS = 4
A_NOPE = 128
A_ROPE = 64
A_DV = 128
A_DH = A_NOPE + A_ROPE
N_GATES = 4 * M_HEADS

LANES = 128
TOKEN_TILE = 512
KV_TILE = 1024
Q_TILE = 1024
MLSTM_CHUNKS_PER_STEP = 4
FF_CHUNK = 1024
VMEM_LIMIT = 56 * 1024 * 1024

LOG2E = math.log2(math.e)


def _dot(a, b):
    return jnp.dot(a, b, preferred_element_type=F32)


def _dot_nt(a, b):
    return lax.dot_general(a, b, (((1,), (1,)), ((), ())), preferred_element_type=F32)


def _dot_tn(a, b):
    return lax.dot_general(a, b, (((0,), (0,)), ((), ())), preferred_element_type=F32)


def _rms(x, g):
    ms = jnp.mean(x * x, axis=-1, keepdims=True)
    return x * lax.rsqrt(ms + EPS) * g


def _split3(y):
    hi = y.astype(BF16).astype(F32)
    mid = (y - hi).astype(BF16).astype(F32)
    lo = y - hi - mid
    return jnp.concatenate([hi, mid, lo], axis=0).astype(BF16)


def _chunk_decay(x, *, backward):
    L = M_CHUNK
    row = lax.broadcasted_iota(jnp.int32, (L, L), 0)
    col = lax.broadcasted_iota(jnp.int32, (L, L), 1)
    umat = jnp.where((row >= col) if backward else (row <= col), 1.0, 0.0).astype(BF16)
    lf = jax.nn.log_sigmoid(x)
    cum3 = _dot(_split3(lf), umat)
    cum = cum3[0:8] + cum3[8:16] + cum3[16:24]
    a = jnp.broadcast_to(jnp.sum(lf, axis=1, keepdims=True), (8, L))
    b = pltpu.roll(cum, 4, axis=0)
    a = pltpu.roll(a, 4, axis=0)
    w_log = a - b + x
    g = jnp.broadcast_to(jnp.max(w_log, axis=1, keepdims=True), (8, L))
    stats = jnp.concatenate([jnp.exp(w_log - g), (x - b) * LOG2E, a * LOG2E, g * LOG2E], axis=0)
    b3 = jnp.concatenate([_split3(b * LOG2E), jnp.zeros((8, L), BF16)], axis=0)
    return stats, b3


def _proj_kernel(x_ref, pos_ref, gmix_ref, win_ref, gcq_ref, gckv_ref, wuqT_ref, wuk_ref,
                 wuvT_ref, gqcol_ref, gk_ref, invf_ref, bg_ref,
                 qk_ref, v_ref, o_ref, sf_ref, b3f_ref, sb_ref, b3b_ref, qT_ref, k_ref, vT_ref):
    x = x_ref[...]
    hn = _rms(x, gmix_ref[...]).astype(BF16)
    z = _dot(hn, win_ref[...])
    qk_ref[...] = z[:, 0:512]
    v_ref[...] = z[:, 512:1024].astype(BF16)
    o_ref[...] = z[:, 1024:1536]
    cq = z[:, 1536:1792]
    ckv = z[:, 1792:1920]
    tail = z[:, 1920:2048]

    tailT = tail.T
    for d, (s_ref, b3_ref) in enumerate(((sf_ref, b3f_ref), (sb_ref, b3b_ref))):
        gx = tailT[96 + 8 * d:104 + 8 * d] + bg_ref[d]
        for j in range(gx.shape[1] // M_CHUNK):
            sl = slice(j * M_CHUNK, (j + 1) * M_CHUNK)
            stats, b3 = _chunk_decay(gx[:, sl], backward=d == 1)
            s_ref[0, :, sl] = stats
            b3_ref[0, :, sl] = b3

    pos = pos_ref[0].astype(F32)
    angT = invf_ref[...] * pos
    cosT = jnp.cos(angT)
    sinT = jnp.sin(angT)

    cqnT = _rms(cq, gcq_ref[...]).T.astype(BF16)
    qT = _dot(wuqT_ref[...], cqnT)
    gq = gqcol_ref[...] * (A_DH ** -0.5 * LOG2E)
    for h in range(A_HEADS):
        qh = qT[h * A_DH:(h + 1) * A_DH]
        ms = jnp.sum(qh * qh, axis=0, keepdims=True) * (1.0 / A_DH)
        qn = qh * lax.rsqrt(ms + EPS) * gq
        x1 = qn[A_NOPE:A_NOPE + 32]
        x2 = qn[A_NOPE + 32:A_DH]
        qT_ref[0, h, 0, 0:A_NOPE, :] = qn[0:A_NOPE].astype(BF16)
        qT_ref[0, h, 0, A_NOPE:A_NOPE + 32, :] = (x1 * cosT - x2 * sinT).astype(BF16)
        qT_ref[0, h, 0, A_NOPE + 32:A_DH, :] = (x2 * cosT + x1 * sinT).astype(BF16)

    ckvn = _rms(ckv, gckv_ref[...])
    knope = _dot(ckvn.astype(BF16), wuk_ref[...])
    vT = _dot(wuvT_ref[...], ckvn.T.astype(BF16))
    for h in range(A_HEADS):
        vT_ref[0, h, 0] = vT[h * A_DV:(h + 1) * A_DV].astype(BF16)

    tab = jnp.concatenate([cosT, cosT, -sinT, sinT], axis=0).T
    cc = tab[:, 0:A_ROPE]
    ss = tab[:, A_ROPE:2 * A_ROPE]
    gk = gk_ref[...]
    kpe = tail[:, 0:A_ROPE]
    kpe_sw = pltpu.roll(tail, LANES - 32, axis=1)[:, 0:A_ROPE]
    ss_pe = jnp.sum(kpe * kpe, axis=-1, keepdims=True)
    krope = (kpe * gk[:, A_NOPE:A_DH] * cc + kpe_sw * gk[:, A_DH:A_DH + A_ROPE] * ss)
    for h in range(A_HEADS):
        kn = knope[:, h * A_NOPE:(h + 1) * A_NOPE]
        ms = (jnp.sum(kn * kn, axis=-1, keepdims=True) + ss_pe) * (1.0 / A_DH)
        r = lax.rsqrt(ms + EPS)
        k_ref[0, h, :, 0:A_NOPE] = (kn * r * gk[:, 0:A_NOPE]).astype(BF16)
        k_ref[0, h, :, A_NOPE:A_DH] = (krope * r).astype(BF16)


def _proj_call(xf, pos3, gmix, win, gcq, gckv, wuqT, wuk, wuvT, gqcol, gk, invf, bg, batch, seq):
    n, d = xf.shape
    tm = TOKEN_TILE
    nt = seq // tm
    steps = n // tm

    def full(a):
        return pl.BlockSpec(a.shape, lambda i: (0,) * a.ndim)

    tok = lambda w: pl.BlockSpec((tm, w), lambda i: (i, 0))
    in_specs = [tok(d), pl.BlockSpec((1, 1, tm), lambda i: (i, 0, 0)), full(gmix), full(win),
                full(gcq), full(gckv), full(wuqT), full(wuk), full(wuvT), full(gqcol), full(gk),
                full(invf), full(bg)]
    stat_spec = pl.BlockSpec((1, 32, tm), lambda i: (i // nt, 0, i % nt))
    out_shape = [
        jax.ShapeDtypeStruct((n, 512), F32),
        jax.ShapeDtypeStruct((n, 512), BF16),
        jax.ShapeDtypeStruct((n, 512), F32),
        jax.ShapeDtypeStruct((batch, 32, seq), F32),
        jax.ShapeDtypeStruct((batch, 32, seq), BF16),
        jax.ShapeDtypeStruct((batch, 32, seq), F32),
        jax.ShapeDtypeStruct((batch, 32, seq), BF16),
        jax.ShapeDtypeStruct((batch, A_HEADS, nt, A_DH, tm), BF16),
        jax.ShapeDtypeStruct((batch, A_HEADS, seq, A_DH), BF16),
        jax.ShapeDtypeStruct((batch, A_HEADS, nt, A_DV, tm), BF16),
    ]
    out_specs = [
        tok(512), tok(512), tok(512), stat_spec, stat_spec, stat_spec, stat_spec,
        pl.BlockSpec((1, A_HEADS, 1, A_DH, tm), lambda i: (i // nt, 0, i % nt, 0, 0)),
        pl.BlockSpec((1, A_HEADS, tm, A_DH), lambda i: (i // nt, 0, i % nt, 0)),
        pl.BlockSpec((1, A_HEADS, 1, A_DV, tm), lambda i: (i // nt, 0, i % nt, 0, 0)),
    ]
    return pl.pallas_call(
        _proj_kernel, grid=(steps,), in_specs=in_specs, out_specs=out_specs, out_shape=out_shape,
        compiler_params=pltpu.CompilerParams(dimension_semantics=("arbitrary",),
                                             vmem_limit_bytes=VMEM_LIMIT),
        name="proj",
    )(xf, pos3, gmix, win, gcq, gckv, wuqT, wuk, wuvT, gqcol, gk, invf, bg)


def _conv_kernel(cur_ref, prev_ref, next_ref, w_ref, b_ref, qk_ref, kT_ref, *, tiles_per_seq):
    t = pl.program_id(0) % tiles_per_seq
    tm = cur_ref.shape[0]
    prev = jnp.where(t == 0, 0.0, prev_ref[...])
    nxt = jnp.where(t == tiles_per_seq - 1, 0.0, next_ref[...])
    ext = jnp.concatenate([prev, cur_ref[...], nxt], axis=0)
    cw = w_ref[...]
    u = b_ref[...]
    for j in range(M_CONV):
        off = 8 - M_CONV // 2 + j
        u = u + cw[j:j + 1, :] * ext[off:off + tm, :]
    qk = u * jax.nn.sigmoid(u)
    k = qk[:, 256:512]
    qk_ref[:, 0:256] = (qk[:, 0:256] * (M_DQK ** -0.5)).astype(BF16)
    qk_ref[:, 256:512] = k.astype(BF16)
    kT_ref[0] = k.T.astype(BF16)


def _conv_call(qk, convw, convb, batch, seq):
    n = qk.shape[0]
    tm = TOKEN_TILE
    nt = seq // tm
    r8 = tm // 8
    return pl.pallas_call(
        functools.partial(_conv_kernel, tiles_per_seq=nt),
        grid=(n // tm,),
        in_specs=[
            pl.BlockSpec((tm, 512), lambda i: (i, 0)),
            pl.BlockSpec((8, 512), lambda i: (jnp.maximum(i * r8 - 1, 0), 0)),
            pl.BlockSpec((8, 512), lambda i: (jnp.minimum((i + 1) * r8, n // 8 - 1), 0)),
            pl.BlockSpec(convw.shape, lambda i: (0, 0)),
            pl.BlockSpec(convb.shape, lambda i: (0, 0)),
        ],
        out_specs=[pl.BlockSpec((tm, 512), lambda i: (i, 0)),
                   pl.BlockSpec((1, 256, tm), lambda i: (i // nt, 0, i % nt))],
        out_shape=[jax.ShapeDtypeStruct((n, 512), BF16),
                   jax.ShapeDtypeStruct((batch, 256, seq), BF16)],
        compiler_params=pltpu.CompilerParams(dimension_semantics=("arbitrary",),
                                             vmem_limit_bytes=VMEM_LIMIT),
        name="conv",
    )(qk, qk, qk, convw, convb)


def _mlstm_kernel(qkf_ref, kTf_ref, vf_ref, sf_ref, b3f_ref, qkb_ref, kTb_ref, vb_ref, sb_ref,
                  b3b_ref, esel_ref, hf_ref, hb_ref, cf_sc, cb_sc, mf_sc, mb_sc):
    L = M_CHUNK
    n_sub = qkf_ref.shape[1] // L

    @pl.when(pl.program_id(1) == 0)
    def _():
        cf_sc[...] = jnp.zeros_like(cf_sc)
        cb_sc[...] = jnp.zeros_like(cb_sc)
        mf_sc[...] = jnp.zeros_like(mf_sc)
        mb_sc[...] = jnp.zeros_like(mb_sc)

    esel = esel_ref[...]
    row = lax.broadcasted_iota(jnp.int32, (L, L), 0)
    col = lax.broadcasted_iota(jnp.int32, (L, L), 1)
    ones_blk = jnp.ones((L, M_DV), BF16)
    fw = dict(qk=qkf_ref, kT=kTf_ref, v=vf_ref, s=sf_ref, b3=b3f_ref, out=hf_ref, c_sc=cf_sc,
              m_sc=mf_sc, visible=col <= row)
    bw = dict(qk=qkb_ref, kT=kTb_ref, v=vb_ref, s=sb_ref, b3=b3b_ref, out=hb_ref, c_sc=cb_sc,
              m_sc=mb_sc, visible=col >= row)

    work = []
    for d in (fw, bw):
        m = d["m_sc"][...]
        order = range(n_sub) if d is fw else range(n_sub - 1, -1, -1)
        for i in order:
            sl = slice(i * L, (i + 1) * L)
            stats = d["s"][0, :, sl]
            a, g = stats[16:24], stats[24:32]
            m_new = jnp.maximum(a + m, g)
            work.append((d, sl, dict(
                w=stats[0:8], r=stats[8:16], m_prev=m,
                dec=jnp.exp2(a + m - m_new), add=jnp.exp2(g - m_new),
                brep=_dot_tn(d["b3"][0, :, sl], esel))))
            m = m_new
        d["m_sc"][...] = m
    work = [work[j + k * n_sub] for j in range(n_sub) for k in range(2)]
    units = [(d, sl, gt, h) for d, sl, gt in work for h in range(M_HEADS)]

    grams, v_augs, s_augs = [], [], []
    for d, sl, gt, h in units:
        q_h = d["qk"][0, sl, h * M_DQK:(h + 1) * M_DQK]
        k_h = d["qk"][0, sl, 256 + h * M_DQK:256 + (h + 1) * M_DQK]
        grams.append(_dot_nt(q_h, k_h))
        v_augs.append(jnp.concatenate([d["v"][0, sl, h * M_DV:(h + 1) * M_DV], ones_blk], axis=1))
    for (d, sl, gt, h), v_aug in zip(units, v_augs):
        kTw = d["kT"][0, h * M_DQK:(h + 1) * M_DQK, sl].astype(F32) * gt["w"][h:h + 1, :]
        s_augs.append(_dot(kTw.astype(BF16), v_aug))

    states = {id(fw): [cf_sc[h] for h in range(M_HEADS)],
              id(bw): [cb_sc[h] for h in range(M_HEADS)]}
    for (d, sl, gt, h), gram, v_aug, s_aug in zip(units, grams, v_augs, s_augs):
        q_h = d["qk"][0, sl, h * M_DQK:(h + 1) * M_DQK]
        brep = gt["brep"][:, h * LANES:(h + 1) * LANES]
        bm = brep + gt["m_prev"][h:h + 1, :]
        dmat = jnp.where(d["visible"], brep + gt["r"][h:h + 1, :], -jnp.inf)
        m_t = jnp.maximum(bm, jnp.max(dmat, axis=1, keepdims=True))
        p = jnp.exp2(dmat - m_t) * gram
        inter = jnp.exp2(bm - m_t)
        c_prev = states[id(d)][h]
        lhs = jnp.concatenate(
            [p.astype(BF16), (inter[:, 0:M_DQK] * q_h.astype(F32)).astype(BF16)], axis=1)
        rhs = jnp.concatenate([v_aug, c_prev.astype(BF16)], axis=0)
        res = _dot(lhs, rhs)
        den = jnp.maximum(jnp.abs(res[:, M_DV:2 * M_DV]), jnp.exp2(-m_t))
        d["out"][0, sl, h * M_DV:(h + 1) * M_DV] = res[:, 0:M_DV] / den
        dec_h = jnp.concatenate([gt["dec"][h:h + 1, :]] * 2, axis=1)
        add_h = jnp.concatenate([gt["add"][h:h + 1, :]] * 2, axis=1)
        states[id(d)][h] = dec_h * c_prev + add_h * s_aug
    for h in range(M_HEADS):
        cf_sc[h] = states[id(fw)][h]
        cb_sc[h] = states[id(bw)][h]


def _mlstm_call(qk3, kT3, v3, sf, b3f, sb, b3b, esel):
    batch, seq, _ = qk3.shape
    L = M_CHUNK * MLSTM_CHUNKS_PER_STEP
    nc = seq // L
    fw_rows = lambda w: pl.BlockSpec((1, L, w), lambda b, c: (b, c, 0))
    bw_rows = lambda w: pl.BlockSpec((1, L, w), lambda b, c: (b, nc - 1 - c, 0))
    fw_lanes = lambda r: pl.BlockSpec((1, r, L), lambda b, c: (b, 0, c))
    bw_lanes = lambda r: pl.BlockSpec((1, r, L), lambda b, c: (b, 0, nc - 1 - c))
    state = pltpu.VMEM((M_HEADS, M_DQK, 2 * M_DV), F32)
    return pl.pallas_call(
        _mlstm_kernel,
        grid=(batch, nc),
        in_specs=[fw_rows(512), fw_lanes(256), fw_rows(512), fw_lanes(32), fw_lanes(32),
                  bw_rows(512), bw_lanes(256), bw_rows(512), bw_lanes(32), bw_lanes(32),
                  pl.BlockSpec(esel.shape, lambda b, c: (0, 0))],
        out_specs=[fw_rows(512), bw_rows(512)],
        out_shape=[jax.ShapeDtypeStruct((batch, seq, 512), F32)] * 2,
        scratch_shapes=[state, state, pltpu.VMEM((8, M_CHUNK), F32), pltpu.VMEM((8, M_CHUNK), F32)],
        compiler_params=pltpu.CompilerParams(dimension_semantics=("arbitrary", "arbitrary"),
                                             vmem_limit_bytes=VMEM_LIMIT),
        name="mlstm",
    )(qk3, kT3, v3, sf, b3f, qk3, kT3, v3, sb, b3b, esel)


def _attn_kernel(qT_ref, k_ref, vT_ref, o_ref, s_sc, cm_sc, acc_sc, *, n_kv):
    tk = KV_TILE
    sub = tk // TOKEN_TILE
    qT = jnp.concatenate([qT_ref[0, 0, i] for i in range(qT_ref.shape[2])], axis=1)
    tq = qT.shape[1]

    def scores(j, slot):
        sT = _dot(k_ref[0, 0, j * tk:(j + 1) * tk, :], qT)
        s_sc[slot] = sT
        cm_sc[slot] = jnp.max(sT, axis=0, keepdims=True)

    m = jnp.full((1, tq), -jnp.inf, F32)
    l = jnp.zeros((1, tq), F32)
    scores(0, 0)
    for j in range(n_kv):
        slot = j % 2
        if j + 1 < n_kv:
            scores(j + 1, 1 - slot)
        m_new = jnp.maximum(m, cm_sc[slot])
        alpha = jnp.exp2(m - m_new)
        p = jnp.exp2(s_sc[slot] - m_new)
        l = alpha * l + jnp.sum(p, axis=0, keepdims=True)
        vt = jnp.concatenate([vT_ref[0, 0, j * sub + i] for i in range(sub)], axis=1)
        pv = _dot(vt, p.astype(BF16))
        if j == 0:
            acc_sc[...] = pv
        else:
            acc_sc[...] = alpha * acc_sc[...] + pv
        m = m_new
    o_ref[0] = (acc_sc[...] / l).T.astype(o_ref.dtype)


def _attn_call(qT, k, vT):
    batch, heads, nt, _, tm = qT.shape
    seq = k.shape[2]
    tq = Q_TILE
    if tq <= tm:
        per = tm // tq
        q_spec = pl.BlockSpec((1, 1, 1, A_DH, tq), lambda b, h, i: (b, h, i // per, 0, i % per))
    else:
        q_spec = pl.BlockSpec((1, 1, tq // tm, A_DH, tm), lambda b, h, i: (b, h, i, 0, 0))
    return pl.pallas_call(
        functools.partial(_attn_kernel, n_kv=seq // KV_TILE),
        grid=(batch, heads, seq // tq),
        in_specs=[
            q_spec,
            pl.BlockSpec((1, 1, seq, A_DH), lambda b, h, i: (b, h, 0, 0)),
            pl.BlockSpec((1, 1, nt, A_DV, tm), lambda b, h, i: (b, h, 0, 0, 0)),
        ],
        out_specs=pl.BlockSpec((1, tq, A_DV), lambda b, h, i: (b, i, h)),
        out_shape=jax.ShapeDtypeStruct((batch, seq, heads * A_DV), BF16),
        scratch_shapes=[pltpu.VMEM((2, KV_TILE, tq), F32), pltpu.VMEM((2, 1, tq), F32),
                        pltpu.VMEM((A_DV, tq), F32)],
        compiler_params=pltpu.CompilerParams(
            dimension_semantics=("arbitrary", "arbitrary", "arbitrary"),
            vmem_limit_bytes=VMEM_LIMIT),
        name="attn",
    )(qT, k, vT)


def _out_kernel(x_ref, hf_ref, hb_ref, o_ref, yb_ref, gml_ref, wout_ref, gffn_ref, w1_ref, w2_ref,
                y_ref):
    x = x_ref[...]
    hsum = hf_ref[...] + hb_ref[...]
    gate = jax.nn.sigmoid(o_ref[...])
    gml = gml_ref[...]
    parts = []
    for h in range(M_HEADS):
        sl = slice(h * M_DV, (h + 1) * M_DV)
        parts.append((gate[:, sl] * _rms(hsum[:, sl], gml[:, sl])).astype(BF16))
    y_a = jnp.concatenate(parts, axis=1)
    n_a = y_a.shape[1]
    x1 = x + _dot(y_a, wout_ref[0:n_a, :]) + _dot(yb_ref[...], wout_ref[n_a:, :])
    hn = _rms(x1, gffn_ref[...]).astype(BF16)
    y_ref[...] = x1
    d_ff = w1_ref.shape[1]
    for j in range(d_ff // FF_CHUNK):
        sl = slice(j * FF_CHUNK, (j + 1) * FF_CHUNK)
        t = jnp.maximum(_dot(hn, w1_ref[:, sl]), 0.0)
        y_ref[...] += _dot((t * t).astype(BF16), w2_ref[sl, :])


def _out_call(xf, hf, hb, o, yb, gml, wout, gffn, w1, w2):
    n, d = xf.shape
    tm = TOKEN_TILE

    def full(a):
        return pl.BlockSpec(a.shape, lambda i: (0,) * a.ndim)

    tok = lambda w: pl.BlockSpec((tm, w), lambda i: (i, 0))
    in_specs = [tok(d), tok(512), tok(512), tok(512), tok(512), full(gml), full(wout), full(gffn),
                full(w1), full(w2)]
    return pl.pallas_call(
        _out_kernel, grid=(n // tm,), in_specs=in_specs, out_specs=tok(d),
        out_shape=jax.ShapeDtypeStruct((n, d), F32),
        compiler_params=pltpu.CompilerParams(dimension_semantics=("arbitrary",),
                                             vmem_limit_bytes=VMEM_LIMIT),
        name="out",
    )(xf, hf, hb, o, yb, gml, wout, gffn, w1, w2)


def _layer(xf, pos3, invf, batch, seq, g_mix_norm, w_in, conv_w, conv_b, b_gates, g_mlstm_out,
           g_cq, g_ckv, w_uq, w_ukv, g_q, g_k, w_out, g_ffn_norm, w_ff1, w_ff2):
    d = xf.shape[1]
    c0 = 2 * 256 + 2 * 512
    gates_w = w_in[:, c0:c0 + N_GATES]
    cq_w = w_in[:, c0 + N_GATES:c0 + N_GATES + 256]
    ckv_w = w_in[:, c0 + N_GATES + 256:c0 + N_GATES + 384]
    kpe_w = w_in[:, c0 + N_GATES + 384:c0 + N_GATES + 448]
    win = jnp.concatenate(
        [w_in[:, 0:c0], cq_w, ckv_w, kpe_w, kpe_w[:, 0:32], gates_w,
         jnp.zeros((d, LANES - 96 - N_GATES), w_in.dtype)], axis=1).astype(BF16)
    wuqT = w_uq.T.astype(BF16)
    ukv = w_ukv.reshape(w_ukv.shape[0], A_HEADS, A_NOPE + A_DV)
    wuk = ukv[:, :, :A_NOPE].reshape(w_ukv.shape[0], A_HEADS * A_NOPE).astype(BF16)
    wuvT = ukv[:, :, A_NOPE:].reshape(w_ukv.shape[0], A_HEADS * A_DV).T.astype(BF16)
    gqcol = g_q.reshape(A_DH, 1)
    gk = jnp.concatenate([g_k, g_k[A_NOPE + 32:], g_k[A_NOPE:A_NOPE + 32]]).reshape(1, -1)

    qk, v, o, sf, b3f, sb, b3b, qT, k, vT = _proj_call(
        xf, pos3, g_mix_norm.reshape(1, -1), win, g_cq.reshape(1, -1), g_ckv.reshape(1, -1),
        wuqT, wuk, wuvT, gqcol, gk, invf, b_gates.reshape(2, 8, 1), batch, seq)

    qkc, kT = _conv_call(qk, conv_w, conv_b.reshape(1, -1), batch, seq)
    esel = jnp.tile(jnp.repeat(jnp.eye(8, M_HEADS, dtype=BF16), LANES, axis=1), (4, 1))
    hf, hb = _mlstm_call(qkc.reshape(batch, seq, 512), kT, v.reshape(batch, seq, 512),
                         sf, b3f, sb, b3b, esel)
    yb = _attn_call(qT, k, vT)
    return _out_call(xf, hf.reshape(batch * seq, -1), hb.reshape(batch * seq, -1), o,
                     yb.reshape(batch * seq, -1), g_mlstm_out.reshape(1, -1), w_out.astype(BF16),
                     g_ffn_norm.reshape(1, -1), w_ff1.astype(BF16), w_ff2.astype(BF16))


def kernel(x, positions, g_mix_norm, w_in, conv_w, conv_b, b_gates, g_mlstm_out, g_cq, g_ckv,
           w_uq, w_ukv, g_q, g_k, w_out, g_ffn_norm, w_ff1, w_ff2):
    batch, seq, d = x.shape
    assert seq % TOKEN_TILE == 0 and seq % M_CHUNK == 0
    inv_freq = ROPE_THETA ** (-jnp.arange(0, A_ROPE, 2, dtype=F32) / A_ROPE)
    invf = inv_freq.reshape(A_ROPE // 2, 1)
    pos3 = positions.reshape(batch * seq // TOKEN_TILE, 1, TOKEN_TILE)
    xf = x.reshape(batch * seq, d)
    for l in range(w_in.shape[0]):
        xf = _layer(xf, pos3, invf, batch, seq, g_mix_norm[l], w_in[l], conv_w[l], conv_b[l],
                    b_gates[l], g_mlstm_out[l], g_cq[l], g_ckv[l], w_uq[l], w_ukv[l], g_q[l],
                    g_k[l], w_out[l], g_ffn_norm[l], w_ff1[l], w_ff2[l])
    return xf.reshape(batch, seq, d)
```

```python
import functools
import math

import jax
import jax.numpy as jnp
from jax import lax
from jax.experimental import pallas as pl
from jax.experimental.pallas import tpu as pltpu

F32 = jnp.float32
BF16 = jnp.bfloat16

EPS = 1e-6
ROPE_THETA = 10000.0
M_HEADS = 4
M_DQK = 64
M_DV = 128
M_CONV = 5
M_CHUNK = 128
A_HEADS = 4
A_NOPE = 128
A_ROPE = 64
A_DV = 128
A_DH = A_NOPE + A_ROPE
N_GATES = 4 * M_HEADS

LANES = 128
TOKEN_TILE = 512
KV_TILE = 1024
Q_TILE = 1024
MLSTM_CHUNKS_PER_STEP = 4
SAFE_SCORE_LOG2 = 64.0
FF_CHUNK = 1024
VMEM_LIMIT = 56 * 1024 * 1024

LOG2E = math.log2(math.e)


def _dot(a, b):
    return jnp.dot(a, b, preferred_element_type=F32)


def _dot_nt(a, b):
    return lax.dot_general(a, b, (((1,), (1,)), ((), ())), preferred_element_type=F32)


def _dot_tn(a, b):
    return lax.dot_general(a, b, (((0,), (0,)), ((), ())), preferred_element_type=F32)


def _rms(x, g):
    ms = jnp.mean(x * x, axis=-1, keepdims=True)
    return x * lax.rsqrt(ms + EPS) * g


def _split3(y):
    hi = y.astype(BF16).astype(F32)
    mid = (y - hi).astype(BF16).astype(F32)
    lo = y - hi - mid
    return jnp.concatenate([hi, mid, lo], axis=0).astype(BF16)


def _chunk_decay(x, *, backward):
    L = M_CHUNK
    row = lax.broadcasted_iota(jnp.int32, (L, L), 0)
    col = lax.broadcasted_iota(jnp.int32, (L, L), 1)
    umat = jnp.where((row >= col) if backward else (row <= col), 1.0, 0.0).astype(BF16)
    lf = jax.nn.log_sigmoid(x)
    cum3 = _dot(_split3(lf), umat)
    cum = cum3[0:8] + cum3[8:16] + cum3[16:24]
    a = jnp.broadcast_to(jnp.sum(lf, axis=1, keepdims=True), (8, L))
    b = pltpu.roll(cum, 4, axis=0)
    a = pltpu.roll(a, 4, axis=0)
    w_log = a - b + x
    g = jnp.broadcast_to(jnp.max(w_log, axis=1, keepdims=True), (8, L))
    stats = jnp.concatenate([jnp.exp(w_log - g), (x - b) * LOG2E, a * LOG2E, g * LOG2E], axis=0)
    b3 = jnp.concatenate([_split3(b * LOG2E), jnp.zeros((8, L), BF16)], axis=0)
    return stats, b3


def _proj_kernel(x_ref, pos_ref, gmix_ref, win_ref, gcq_ref, gckv_ref, wuqT_ref, wuk_ref,
                 wuvT_ref, gqcol_ref, gk_ref, invf_ref, bg_ref,
                 qk_ref, v_ref, o_ref, sf_ref, b3f_ref, sb_ref, b3b_ref, qT_ref, k_ref, vT_ref):
    x = x_ref[...]
    hn = _rms(x, gmix_ref[...]).astype(BF16)
    z = _dot(hn, win_ref[...])
    qk_ref[...] = z[:, 0:512]
    v_ref[...] = z[:, 512:1024].astype(BF16)
    o_ref[...] = z[:, 1024:1536]
    cq = z[:, 1536:1792]
    ckv = z[:, 1792:1920]
    tail = z[:, 1920:2048]

    tailT = tail.T
    for d, (s_ref, b3_ref) in enumerate(((sf_ref, b3f_ref), (sb_ref, b3b_ref))):
        gx = tailT[96 + 8 * d:104 + 8 * d] + bg_ref[d]
        for j in range(gx.shape[1] // M_CHUNK):
            sl = slice(j * M_CHUNK, (j + 1) * M_CHUNK)
            stats, b3 = _chunk_decay(gx[:, sl], backward=d == 1)
            s_ref[0, :, sl] = stats
            b3_ref[0, :, sl] = b3

    pos = pos_ref[0].astype(F32)
    angT = invf_ref[...] * pos
    cosT = jnp.cos(angT)
    sinT = jnp.sin(angT)

    cqnT = _rms(cq, gcq_ref[...]).T.astype(BF16)
    qT = _dot(wuqT_ref[...], cqnT)
    gq = gqcol_ref[...] * (A_DH ** -0.5 * LOG2E)
    for h in range(A_HEADS):
        qh = qT[h * A_DH:(h + 1) * A_DH]
        ms = jnp.sum(qh * qh, axis=0, keepdims=True) * (1.0 / A_DH)
        qn = qh * lax.rsqrt(ms + EPS) * gq
        x1 = qn[A_NOPE:A_NOPE + 32]
        x2 = qn[A_NOPE + 32:A_DH]
        qT_ref[0, h, 0, 0:A_NOPE, :] = qn[0:A_NOPE].astype(BF16)
        qT_ref[0, h, 0, A_NOPE:A_NOPE + 32, :] = (x1 * cosT - x2 * sinT).astype(BF16)
        qT_ref[0, h, 0, A_NOPE + 32:A_DH, :] = (x2 * cosT + x1 * sinT).astype(BF16)

    ckvn = _rms(ckv, gckv_ref[...])
    knope = _dot(ckvn.astype(BF16), wuk_ref[...])
    vT = _dot(wuvT_ref[...], ckvn.T.astype(BF16))
    for h in range(A_HEADS):
        vT_ref[0, h, 0] = vT[h * A_DV:(h + 1) * A_DV].astype(BF16)

    tab = jnp.concatenate([cosT, cosT, -sinT, sinT], axis=0).T
    cc = tab[:, 0:A_ROPE]
    ss = tab[:, A_ROPE:2 * A_ROPE]
    gk = gk_ref[...]
    kpe = tail[:, 0:A_ROPE]
    kpe_sw = pltpu.roll(tail, LANES - 32, axis=1)[:, 0:A_ROPE]
    ss_pe = jnp.sum(kpe * kpe, axis=-1, keepdims=True)
    krope = (kpe * gk[:, A_NOPE:A_DH] * cc + kpe_sw * gk[:, A_DH:A_DH + A_ROPE] * ss)
    for h in range(A_HEADS):
        kn = knope[:, h * A_NOPE:(h + 1) * A_NOPE]
        ms = (jnp.sum(kn * kn, axis=-1, keepdims=True) + ss_pe) * (1.0 / A_DH)
        r = lax.rsqrt(ms + EPS)
        k_ref[0, h, :, 0:A_NOPE] = (kn * r * gk[:, 0:A_NOPE]).astype(BF16)
        k_ref[0, h, :, A_NOPE:A_DH] = (krope * r).astype(BF16)


def _proj_call(xf, pos3, gmix, win, gcq, gckv, wuqT, wuk, wuvT, gqcol, gk, invf, bg, batch, seq):
    n, d = xf.shape
    tm = TOKEN_TILE
    nt = seq // tm
    steps = n // tm

    def full(a):
        return pl.BlockSpec(a.shape, lambda i: (0,) * a.ndim)

    tok = lambda w: pl.BlockSpec((tm, w), lambda i: (i, 0))
    in_specs = [tok(d), pl.BlockSpec((1, 1, tm), lambda i: (i, 0, 0)), full(gmix), full(win),
                full(gcq), full(gckv), full(wuqT), full(wuk), full(wuvT), full(gqcol), full(gk),
                full(invf), full(bg)]
    stat_spec = pl.BlockSpec((1, 32, tm), lambda i: (i // nt, 0, i % nt))
    out_shape = [
        jax.ShapeDtypeStruct((n, 512), F32),
        jax.ShapeDtypeStruct((n, 512), BF16),
        jax.ShapeDtypeStruct((n, 512), F32),
        jax.ShapeDtypeStruct((batch, 32, seq), F32),
        jax.ShapeDtypeStruct((batch, 32, seq), BF16),
        jax.ShapeDtypeStruct((batch, 32, seq), F32),
        jax.ShapeDtypeStruct((batch, 32, seq), BF16),
        jax.ShapeDtypeStruct((batch, A_HEADS, nt, A_DH, tm), BF16),
        jax.ShapeDtypeStruct((batch, A_HEADS, seq, A_DH), BF16),
        jax.ShapeDtypeStruct((batch, A_HEADS, nt, A_DV, tm), BF16),
    ]
    out_specs = [
        tok(512), tok(512), tok(512), stat_spec, stat_spec, stat_spec, stat_spec,
        pl.BlockSpec((1, A_HEADS, 1, A_DH, tm), lambda i: (i // nt, 0, i % nt, 0, 0)),
        pl.BlockSpec((1, A_HEADS, tm, A_DH), lambda i: (i // nt, 0, i % nt, 0)),
        pl.BlockSpec((1, A_HEADS, 1, A_DV, tm), lambda i: (i // nt, 0, i % nt, 0, 0)),
    ]
    return pl.pallas_call(
        _proj_kernel, grid=(steps,), in_specs=in_specs, out_specs=out_specs, out_shape=out_shape,
        compiler_params=pltpu.CompilerParams(dimension_semantics=("arbitrary",),
                                             vmem_limit_bytes=VMEM_LIMIT),
        name="proj",
    )(xf, pos3, gmix, win, gcq, gckv, wuqT, wuk, wuvT, gqcol, gk, invf, bg)


def _conv_kernel(cur_ref, prev_ref, next_ref, w_ref, b_ref, qk_ref, kT_ref, *, tiles_per_seq):
    t = pl.program_id(0) % tiles_per_seq
    tm = cur_ref.shape[0]
    prev = jnp.where(t == 0, 0.0, prev_ref[...])
    nxt = jnp.where(t == tiles_per_seq - 1, 0.0, next_ref[...])
    ext = jnp.concatenate([prev, cur_ref[...], nxt], axis=0)
    cw = w_ref[...]
    u = b_ref[...]
    for j in range(M_CONV):
        off = 8 - M_CONV // 2 + j
        u = u + cw[j:j + 1, :] * ext[off:off + tm, :]
    qk = u * jax.nn.sigmoid(u)
    k = qk[:, 256:512]
    qk_ref[:, 0:256] = (qk[:, 0:256] * (M_DQK ** -0.5)).astype(BF16)
    qk_ref[:, 256:512] = k.astype(BF16)
    kT_ref[0] = k.T.astype(BF16)


def _conv_call(qk, convw, convb, batch, seq):
    n = qk.shape[0]
    tm = TOKEN_TILE
    nt = seq // tm
    r8 = tm // 8
    return pl.pallas_call(
        functools.partial(_conv_kernel, tiles_per_seq=nt),
        grid=(n // tm,),
        in_specs=[
            pl.BlockSpec((tm, 512), lambda i: (i, 0)),
            pl.BlockSpec((8, 512), lambda i: (jnp.maximum(i * r8 - 1, 0), 0)),
            pl.BlockSpec((8, 512), lambda i: (jnp.minimum((i + 1) * r8, n // 8 - 1), 0)),
            pl.BlockSpec(convw.shape, lambda i: (0, 0)),
            pl.BlockSpec(convb.shape, lambda i: (0, 0)),
        ],
        out_specs=[pl.BlockSpec((tm, 512), lambda i: (i, 0)),
                   pl.BlockSpec((1, 256, tm), lambda i: (i // nt, 0, i % nt))],
        out_shape=[jax.ShapeDtypeStruct((n, 512), BF16),
                   jax.ShapeDtypeStruct((batch, 256, seq), BF16)],
        compiler_params=pltpu.CompilerParams(dimension_semantics=("arbitrary",),
                                             vmem_limit_bytes=VMEM_LIMIT),
        name="conv",
    )(qk, qk, qk, convw, convb)


def _mlstm_kernel(qkf_ref, kTf_ref, vf_ref, sf_ref, b3f_ref, qkb_ref, kTb_ref, vb_ref, sb_ref,
                  b3b_ref, esel_ref, hf_ref, hb_ref, cf_sc, cb_sc, mf_sc, mb_sc):
    L = M_CHUNK
    n_sub = qkf_ref.shape[1] // L

    @pl.when(pl.program_id(1) == 0)
    def _():
        cf_sc[...] = jnp.zeros_like(cf_sc)
        cb_sc[...] = jnp.zeros_like(cb_sc)
        mf_sc[...] = jnp.zeros_like(mf_sc)
        mb_sc[...] = jnp.zeros_like(mb_sc)

    esel = esel_ref[...]
    row = lax.broadcasted_iota(jnp.int32, (L, L), 0)
    col = lax.broadcasted_iota(jnp.int32, (L, L), 1)
    ones_blk = jnp.ones((L, M_DV), BF16)
    fw = dict(qk=qkf_ref, kT=kTf_ref, v=vf_ref, s=sf_ref, b3=b3f_ref, out=hf_ref, c_sc=cf_sc,
              m_sc=mf_sc, visible=col <= row)
    bw = dict(qk=qkb_ref, kT=kTb_ref, v=vb_ref, s=sb_ref, b3=b3b_ref, out=hb_ref, c_sc=cb_sc,
              m_sc=mb_sc, visible=col >= row)

    work = []
    for d in (fw, bw):
        m = d["m_sc"][...]
        order = range(n_sub) if d is fw else range(n_sub - 1, -1, -1)
        for i in order:
            sl = slice(i * L, (i + 1) * L)
            stats = d["s"][0, :, sl]
            a, g = stats[16:24], stats[24:32]
            m_new = jnp.maximum(a + m, g)
            work.append((d, sl, dict(
                w=stats[0:8], r=stats[8:16], m_prev=m,
                dec=jnp.exp2(a + m - m_new), add=jnp.exp2(g - m_new),
                brep=_dot_tn(d["b3"][0, :, sl], esel))))
            m = m_new
        d["m_sc"][...] = m
    work = [work[j + k * n_sub] for j in range(n_sub) for k in range(2)]
    units = [(d, sl, gt, h) for d, sl, gt in work for h in range(M_HEADS)]

    grams, v_augs, s_augs = [], [], []
    for d, sl, gt, h in units:
        q_h = d["qk"][0, sl, h * M_DQK:(h + 1) * M_DQK]
        k_h = d["qk"][0, sl, 256 + h * M_DQK:256 + (h + 1) * M_DQK]
        grams.append(_dot_nt(q_h, k_h))
        v_augs.append(jnp.concatenate([d["v"][0, sl, h * M_DV:(h + 1) * M_DV], ones_blk], axis=1))
    for (d, sl, gt, h), v_aug in zip(units, v_augs):
        kTw = d["kT"][0, h * M_DQK:(h + 1) * M_DQK, sl].astype(F32) * gt["w"][h:h + 1, :]
        s_augs.append(_dot(kTw.astype(BF16), v_aug))

    states = {id(fw): [cf_sc[h] for h in range(M_HEADS)],
              id(bw): [cb_sc[h] for h in range(M_HEADS)]}
    for (d, sl, gt, h), gram, v_aug, s_aug in zip(units, grams, v_augs, s_augs):
        q_h = d["qk"][0, sl, h * M_DQK:(h + 1) * M_DQK]
        brep = gt["brep"][:, h * LANES:(h + 1) * LANES]
        bm = brep + gt["m_prev"][h:h + 1, :]
        dmat = jnp.where(d["visible"], brep + gt["r"][h:h + 1, :], -jnp.inf)
        m_t = jnp.maximum(bm, jnp.max(dmat, axis=1, keepdims=True))
        p = jnp.exp2(dmat - m_t) * gram
        inter = jnp.exp2(bm - m_t)
        c_prev = states[id(d)][h]
        lhs = jnp.concatenate(
            [p.astype(BF16), (inter[:, 0:M_DQK] * q_h.astype(F32)).astype(BF16)], axis=1)
        rhs = jnp.concatenate([v_aug, c_prev.astype(BF16)], axis=0)
        res = _dot(lhs, rhs)
        den = jnp.maximum(jnp.abs(res[:, M_DV:2 * M_DV]), jnp.exp2(-m_t))
        d["out"][0, sl, h * M_DV:(h + 1) * M_DV] = res[:, 0:M_DV] / den
        dec_h = jnp.concatenate([gt["dec"][h:h + 1, :]] * 2, axis=1)
        add_h = jnp.concatenate([gt["add"][h:h + 1, :]] * 2, axis=1)
        states[id(d)][h] = dec_h * c_prev + add_h * s_aug
    for h in range(M_HEADS):
        cf_sc[h] = states[id(fw)][h]
        cb_sc[h] = states[id(bw)][h]


def _mlstm_call(qk3, kT3, v3, sf, b3f, sb, b3b, esel):
    batch, seq, _ = qk3.shape
    L = M_CHUNK * MLSTM_CHUNKS_PER_STEP
    nc = seq // L
    fw_rows = lambda w: pl.BlockSpec((1, L, w), lambda b, c: (b, c, 0))
    bw_rows = lambda w: pl.BlockSpec((1, L, w), lambda b, c: (b, nc - 1 - c, 0))
    fw_lanes = lambda r: pl.BlockSpec((1, r, L), lambda b, c: (b, 0, c))
    bw_lanes = lambda r: pl.BlockSpec((1, r, L), lambda b, c: (b, 0, nc - 1 - c))
    state = pltpu.VMEM((M_HEADS, M_DQK, 2 * M_DV), F32)
    return pl.pallas_call(
        _mlstm_kernel,
        grid=(batch, nc),
        in_specs=[fw_rows(512), fw_lanes(256), fw_rows(512), fw_lanes(32), fw_lanes(32),
                  bw_rows(512), bw_lanes(256), bw_rows(512), bw_lanes(32), bw_lanes(32),
                  pl.BlockSpec(esel.shape, lambda b, c: (0, 0))],
        out_specs=[fw_rows(512), bw_rows(512)],
        out_shape=[jax.ShapeDtypeStruct((batch, seq, 512), F32)] * 2,
        scratch_shapes=[state, state, pltpu.VMEM((8, M_CHUNK), F32), pltpu.VMEM((8, M_CHUNK), F32)],
        compiler_params=pltpu.CompilerParams(dimension_semantics=("arbitrary", "arbitrary"),
                                             vmem_limit_bytes=VMEM_LIMIT),
        name="mlstm",
    )(qk3, kT3, v3, sf, b3f, qk3, kT3, v3, sb, b3b, esel)


def _attn_kernel(qT_ref, k_ref, vT_ref, o_ref, s_sc, cm_sc, acc_sc, *, n_kv):
    tk = KV_TILE
    sub = tk // TOKEN_TILE
    qT = jnp.concatenate([qT_ref[0, 0, i] for i in range(qT_ref.shape[2])], axis=1)
    tq = qT.shape[1]

    def scores(j, slot):
        sT = _dot(k_ref[0, 0, j * tk:(j + 1) * tk, :], qT)
        s_sc[slot] = sT
        cm_sc[slot] = jnp.max(sT, axis=0, keepdims=True)

    m = jnp.full((1, tq), -jnp.inf, F32)
    l = jnp.zeros((1, tq), F32)
    scores(0, 0)
    for j in range(n_kv):
        slot = j % 2
        if j + 1 < n_kv:
            scores(j + 1, 1 - slot)
        m_new = jnp.maximum(m, cm_sc[slot])
        alpha = jnp.exp2(m - m_new)
        p = jnp.exp2(s_sc[slot] - m_new)
        l = alpha * l + jnp.sum(p, axis=0, keepdims=True)
        vt = jnp.concatenate([vT_ref[0, 0, j * sub + i] for i in range(sub)], axis=1)
        pv = _dot(vt, p.astype(BF16))
        if j == 0:
            acc_sc[...] = pv
        else:
            acc_sc[...] = alpha * acc_sc[...] + pv
        m = m_new
    o_ref[0] = (acc_sc[...] / l).T.astype(o_ref.dtype)


def _attn_bounded_kernel(qT_ref, k_ref, vT_ref, o_ref, acc_sc, *, n_kv):
    tk = KV_TILE
    sub = tk // TOKEN_TILE
    qT = jnp.concatenate([qT_ref[0, 0, i] for i in range(qT_ref.shape[2])], axis=1)
    l = jnp.zeros((1, qT.shape[1]), F32)
    for j in range(n_kv):
        p = jnp.exp2(_dot(k_ref[0, 0, j * tk:(j + 1) * tk, :], qT))
        l = l + jnp.sum(p, axis=0, keepdims=True)
        vt = jnp.concatenate([vT_ref[0, 0, j * sub + i] for i in range(sub)], axis=1)
        pv = _dot(vt, p.astype(BF16))
        if j == 0:
            acc_sc[...] = pv
        else:
            acc_sc[...] += pv
    o_ref[0] = (acc_sc[...] / l).T.astype(o_ref.dtype)


def _attn_call(qT, k, vT, score_bound_log2):
    batch, heads, nt, _, tm = qT.shape
    seq = k.shape[2]
    tq = Q_TILE
    n_kv = seq // KV_TILE
    slabs = tq // tm

    def call(body, scratch, name):
        return pl.pallas_call(
            functools.partial(body, n_kv=n_kv),
            grid=(batch, heads, seq // tq),
            in_specs=[
                pl.BlockSpec((1, 1, slabs, A_DH, tm), lambda b, h, i: (b, h, i, 0, 0)),
                pl.BlockSpec((1, 1, seq, A_DH), lambda b, h, i: (b, h, 0, 0)),
                pl.BlockSpec((1, 1, nt, A_DV, tm), lambda b, h, i: (b, h, 0, 0, 0)),
            ],
            out_specs=pl.BlockSpec((1, tq, A_DV), lambda b, h, i: (b, i, h)),
            out_shape=jax.ShapeDtypeStruct((batch, seq, heads * A_DV), BF16),
            scratch_shapes=scratch,
            compiler_params=pltpu.CompilerParams(
                dimension_semantics=("arbitrary", "arbitrary", "arbitrary"),
                vmem_limit_bytes=VMEM_LIMIT),
            name=name,
        )(qT, k, vT)

    acc = pltpu.VMEM((A_DV, tq), F32)
    return lax.cond(
        score_bound_log2 <= SAFE_SCORE_LOG2,
        lambda: call(_attn_bounded_kernel, [acc], "attn_bounded"),
        lambda: call(_attn_kernel, [pltpu.VMEM((2, KV_TILE, tq), F32),
                                    pltpu.VMEM((2, 1, tq), F32), acc], "attn"))


def _out_kernel(x_ref, hf_ref, hb_ref, o_ref, yb_ref, gml_ref, wout_ref, gffn_ref, w1_ref, w2_ref,
                y_ref):
    x = x_ref[...]
    hsum = hf_ref[...] + hb_ref[...]
    gate = jax.nn.sigmoid(o_ref[...])
    gml = gml_ref[...]
    parts = []
    for h in range(M_HEADS):
        sl = slice(h * M_DV, (h + 1) * M_DV)
        parts.append((gate[:, sl] * _rms(hsum[:, sl], gml[:, sl])).astype(BF16))
    y_a = jnp.concatenate(parts, axis=1)
    n_a = y_a.shape[1]
    x1 = x + _dot(y_a, wout_ref[0:n_a, :]) + _dot(yb_ref[...], wout_ref[n_a:, :])
    hn = _rms(x1, gffn_ref[...]).astype(BF16)
    y_ref[...] = x1
    d_ff = w1_ref.shape[1]
    for j in range(d_ff // FF_CHUNK):
        sl = slice(j * FF_CHUNK, (j + 1) * FF_CHUNK)
        t = jnp.maximum(_dot(hn, w1_ref[:, sl]), 0.0)
        y_ref[...] += _dot((t * t).astype(BF16), w2_ref[sl, :])


def _out_call(xf, hf, hb, o, yb, gml, wout, gffn, w1, w2):
    n, d = xf.shape
    tm = TOKEN_TILE

    def full(a):
        return pl.BlockSpec(a.shape, lambda i: (0,) * a.ndim)

    tok = lambda w: pl.BlockSpec((tm, w), lambda i: (i, 0))
    in_specs = [tok(d), tok(512), tok(512), tok(512), tok(512), full(gml), full(wout), full(gffn),
                full(w1), full(w2)]
    return pl.pallas_call(
        _out_kernel, grid=(n // tm,), in_specs=in_specs, out_specs=tok(d),
        out_shape=jax.ShapeDtypeStruct((n, d), F32),
        compiler_params=pltpu.CompilerParams(dimension_semantics=("arbitrary",),
                                             vmem_limit_bytes=VMEM_LIMIT),
        name="out",
    )(xf, hf, hb, o, yb, gml, wout, gffn, w1, w2)


def _layer(xf, pos3, invf, batch, seq, g_mix_norm, w_in, conv_w, conv_b, b_gates, g_mlstm_out,
           g_cq, g_ckv, w_uq, w_ukv, g_q, g_k, w_out, g_ffn_norm, w_ff1, w_ff2):
    d = xf.shape[1]
    c0 = 2 * 256 + 2 * 512
    gates_w = w_in[:, c0:c0 + N_GATES]
    cq_w = w_in[:, c0 + N_GATES:c0 + N_GATES + 256]
    ckv_w = w_in[:, c0 + N_GATES + 256:c0 + N_GATES + 384]
    kpe_w = w_in[:, c0 + N_GATES + 384:c0 + N_GATES + 448]
    win = jnp.concatenate(
        [w_in[:, 0:c0], cq_w, ckv_w, kpe_w, kpe_w[:, 0:32], gates_w,
         jnp.zeros((d, LANES - 96 - N_GATES), w_in.dtype)], axis=1).astype(BF16)
    wuqT = w_uq.T.astype(BF16)
    ukv = w_ukv.reshape(w_ukv.shape[0], A_HEADS, A_NOPE + A_DV)
    wuk = ukv[:, :, :A_NOPE].reshape(w_ukv.shape[0], A_HEADS * A_NOPE).astype(BF16)
    wuvT = ukv[:, :, A_NOPE:].reshape(w_ukv.shape[0], A_HEADS * A_DV).T.astype(BF16)
    gqcol = g_q.reshape(A_DH, 1)
    gk = jnp.concatenate([g_k, g_k[A_NOPE + 32:], g_k[A_NOPE:A_NOPE + 32]]).reshape(1, -1)

    qk, v, o, sf, b3f, sb, b3b, qT, k, vT = _proj_call(
        xf, pos3, g_mix_norm.reshape(1, -1), win, g_cq.reshape(1, -1), g_ckv.reshape(1, -1),
        wuqT, wuk, wuvT, gqcol, gk, invf, b_gates.reshape(2, 8, 1), batch, seq)

    qkc, kT = _conv_call(qk, conv_w, conv_b.reshape(1, -1), batch, seq)
    esel = jnp.tile(jnp.repeat(jnp.eye(8, M_HEADS, dtype=BF16), LANES, axis=1), (4, 1))
    hf, hb = _mlstm_call(qkc.reshape(batch, seq, 512), kT, v.reshape(batch, seq, 512),
                         sf, b3f, sb, b3b, esel)
    score_bound_log2 = (A_DH ** 0.5 * LOG2E * 1.01) * jnp.max(jnp.abs(g_q)) * jnp.max(jnp.abs(g_k))
    yb = _attn_call(qT, k, vT, score_bound_log2)
    return _out_call(xf, hf.reshape(batch * seq, -1), hb.reshape(batch * seq, -1), o,
                     yb.reshape(batch * seq, -1), g_mlstm_out.reshape(1, -1), w_out.astype(BF16),
                     g_ffn_norm.reshape(1, -1), w_ff1.astype(BF16), w_ff2.astype(BF16))


def kernel(x, positions, g_mix_norm, w_in, conv_w, conv_b, b_gates, g_mlstm_out, g_cq, g_ckv,
           w_uq, w_ukv, g_q, g_k, w_out, g_ffn_norm, w_ff1, w_ff2):
    batch, seq, d = x.shape
    assert seq % TOKEN_TILE == 0 and seq % M_CHUNK == 0
    inv_freq = ROPE_THETA ** (-jnp.arange(0, A_ROPE, 2, dtype=F32) / A_ROPE)
    invf = inv_freq.reshape(A_ROPE // 2, 1)
    pos3 = positions.reshape(batch * seq // TOKEN_TILE, 1, TOKEN_TILE)
    xf = x.reshape(batch * seq, d)
    for l in range(w_in.shape[0]):
        xf = _layer(xf, pos3, invf, batch, seq, g_mix_norm[l], w_in[l], conv_w[l], conv_b[l],
                    b_gates[l], g_mlstm_out[l], g_cq[l], g_ckv[l], w_uq[l], w_ukv[l], g_q[l],
                    g_k[l], w_out[l], g_ffn_norm[l], w_ff1[l], w_ff2[l])
    return xf.reshape(batch, seq, d)
```

```python
import functools
import math

import jax
import jax.numpy as jnp
from jax import lax
from jax.experimental import pallas as pl
from jax.experimental.pallas import tpu as pltpu

F32 = jnp.float32
BF16 = jnp.bfloat16

EPS = 1e-6
ROPE_THETA = 10000.0
M_HEADS = 4
M_DQK = 64
M_DV = 128
M_CONV = 5
M_CHUNK = 128
A_HEADS = 4
A_NOPE = 128
A_ROPE = 64
A_DV = 128
A_DH = A_NOPE + A_ROPE
N_GATES = 4 * M_HEADS

LANES = 128
TOKEN_TILE = 512
KV_TILE = 1024
Q_TILE = 1024
MLSTM_CHUNKS_PER_STEP = 8
SAFE_SCORE_LOG2 = 64.0
FF_CHUNK = 1024
VMEM_LIMIT = 56 * 1024 * 1024

LOG2E = math.log2(math.e)


def _dot(a, b):
    return jnp.dot(a, b, preferred_element_type=F32)


def _dot_nt(a, b):
    return lax.dot_general(a, b, (((1,), (1,)), ((), ())), preferred_element_type=F32)


def _dot_tn(a, b):
    return lax.dot_general(a, b, (((0,), (0,)), ((), ())), preferred_element_type=F32)


def _rms(x, g):
    ms = jnp.mean(x * x, axis=-1, keepdims=True)
    return x * lax.rsqrt(ms + EPS) * g


def _split3(y):
    hi = y.astype(BF16).astype(F32)
    mid = (y - hi).astype(BF16).astype(F32)
    lo = y - hi - mid
    return jnp.concatenate([hi, mid, lo], axis=0).astype(BF16)


def _chunk_decay(x, *, backward):
    L = M_CHUNK
    row = lax.broadcasted_iota(jnp.int32, (L, L), 0)
    col = lax.broadcasted_iota(jnp.int32, (L, L), 1)
    umat = jnp.where((row >= col) if backward else (row <= col), 1.0, 0.0).astype(BF16)
    lf = jax.nn.log_sigmoid(x)
    cum3 = _dot(_split3(lf), umat)
    cum = cum3[0:8] + cum3[8:16] + cum3[16:24]
    a = jnp.broadcast_to(jnp.sum(lf, axis=1, keepdims=True), (8, L))
    b = pltpu.roll(cum, 4, axis=0)
    a = pltpu.roll(a, 4, axis=0)
    w_log = a - b + x
    g = jnp.broadcast_to(jnp.max(w_log, axis=1, keepdims=True), (8, L))
    stats = jnp.concatenate([jnp.exp(w_log - g), (x - b) * LOG2E, a * LOG2E, g * LOG2E], axis=0)
    b3 = jnp.concatenate([_split3(b * LOG2E), jnp.zeros((8, L), BF16)], axis=0)
    return stats, b3


def _proj_kernel(x_ref, pos_ref, gmix_ref, wmain_ref, wtail_ref, gcq_ref, gckv_ref, wuqT_ref,
                 wuk_ref, wuvT_ref, gqcol_ref, gk_ref, invf_ref, bg_ref,
                 qk_ref, v_ref, o_ref, sf_ref, b3f_ref, sb_ref, b3b_ref, qT_ref, k_ref, vT_ref):
    x = x_ref[...]
    hn = _rms(x, gmix_ref[...]).astype(BF16)
    z = _dot(hn, wmain_ref[...])
    qk_ref[...] = z[:, 0:512]
    v_ref[...] = z[:, 512:1024].astype(BF16)
    o_ref[...] = z[:, 1024:1536]
    zt = _dot(hn, wtail_ref[...])
    cq = zt[:, 0:256]
    ckv = zt[:, 256:384]
    tail = zt[:, 384:512]

    tailT = tail.T
    for d, (s_ref, b3_ref) in enumerate(((sf_ref, b3f_ref), (sb_ref, b3b_ref))):
        gx = tailT[96 + 8 * d:104 + 8 * d] + bg_ref[d]
        for j in range(gx.shape[1] // M_CHUNK):
            sl = slice(j * M_CHUNK, (j + 1) * M_CHUNK)
            stats, b3 = _chunk_decay(gx[:, sl], backward=d == 1)
            s_ref[0, :, sl] = stats
            b3_ref[0, :, sl] = b3

    pos = pos_ref[0].astype(F32)
    angT = invf_ref[...] * pos
    cosT = jnp.cos(angT)
    sinT = jnp.sin(angT)

    cqnT = _rms(cq, gcq_ref[...]).T.astype(BF16)
    qT = _dot(wuqT_ref[...], cqnT)
    gq = gqcol_ref[...] * (A_DH ** -0.5 * LOG2E)
    for h in range(A_HEADS):
        qh = qT[h * A_DH:(h + 1) * A_DH]
        ms = jnp.sum(qh * qh, axis=0, keepdims=True) * (1.0 / A_DH)
        qn = qh * lax.rsqrt(ms + EPS) * gq
        x1 = qn[A_NOPE:A_NOPE + 32]
        x2 = qn[A_NOPE + 32:A_DH]
        qT_ref[0, h, 0, 0:A_NOPE, :] = qn[0:A_NOPE].astype(BF16)
        qT_ref[0, h, 0, A_NOPE:A_NOPE + 32, :] = (x1 * cosT - x2 * sinT).astype(BF16)
        qT_ref[0, h, 0, A_NOPE + 32:A_DH, :] = (x2 * cosT + x1 * sinT).astype(BF16)

    ckvn = _rms(ckv, gckv_ref[...])
    knope = _dot(ckvn.astype(BF16), wuk_ref[...])
    vT = _dot(wuvT_ref[...], ckvn.T.astype(BF16))
    for h in range(A_HEADS):
        vT_ref[0, h, 0] = vT[h * A_DV:(h + 1) * A_DV].astype(BF16)

    tab = jnp.concatenate([cosT, cosT, -sinT, sinT], axis=0).T
    cc = tab[:, 0:A_ROPE]
    ss = tab[:, A_ROPE:2 * A_ROPE]
    gk = gk_ref[...]
    kpe = tail[:, 0:A_ROPE]
    kpe_sw = pltpu.roll(tail, LANES - 32, axis=1)[:, 0:A_ROPE]
    ss_pe = jnp.sum(kpe * kpe, axis=-1, keepdims=True)
    krope = (kpe * gk[:, A_NOPE:A_DH] * cc + kpe_sw * gk[:, A_DH:A_DH + A_ROPE] * ss)
    for h in range(A_HEADS):
        kn = knope[:, h * A_NOPE:(h + 1) * A_NOPE]
        ms = (jnp.sum(kn * kn, axis=-1, keepdims=True) + ss_pe) * (1.0 / A_DH)
        r = lax.rsqrt(ms + EPS)
        k_ref[0, h, :, 0:A_NOPE] = (kn * r * gk[:, 0:A_NOPE]).astype(BF16)
        k_ref[0, h, :, A_NOPE:A_DH] = (krope * r).astype(BF16)


def _proj_call(xf, pos3, gmix, wmain, wtail, gcq, gckv, wuqT, wuk, wuvT, gqcol, gk, invf, bg,
               batch, seq):
    n, d = xf.shape
    tm = TOKEN_TILE
    nt = seq // tm
    steps = n // tm

    def full(a):
        return pl.BlockSpec(a.shape, lambda i: (0,) * a.ndim)

    tok = lambda w: pl.BlockSpec((tm, w), lambda i: (i, 0))
    in_specs = [tok(d), pl.BlockSpec((1, 1, tm), lambda i: (i, 0, 0)), full(gmix), full(wmain),
                full(wtail), full(gcq), full(gckv), full(wuqT), full(wuk), full(wuvT), full(gqcol),
                full(gk), full(invf), full(bg)]
    stat_spec = pl.BlockSpec((1, 32, tm), lambda i: (i // nt, 0, i % nt))
    out_shape = [
        jax.ShapeDtypeStruct((n, 512), F32),
        jax.ShapeDtypeStruct((n, 512), BF16),
        jax.ShapeDtypeStruct((n, 512), F32),
        jax.ShapeDtypeStruct((batch, 32, seq), F32),
        jax.ShapeDtypeStruct((batch, 32, seq), BF16),
        jax.ShapeDtypeStruct((batch, 32, seq), F32),
        jax.ShapeDtypeStruct((batch, 32, seq), BF16),
        jax.ShapeDtypeStruct((batch, A_HEADS, nt, A_DH, tm), BF16),
        jax.ShapeDtypeStruct((batch, A_HEADS, seq, A_DH), BF16),
        jax.ShapeDtypeStruct((batch, A_HEADS, nt, A_DV, tm), BF16),
    ]
    out_specs = [
        tok(512), tok(512), tok(512), stat_spec, stat_spec, stat_spec, stat_spec,
        pl.BlockSpec((1, A_HEADS, 1, A_DH, tm), lambda i: (i // nt, 0, i % nt, 0, 0)),
        pl.BlockSpec((1, A_HEADS, tm, A_DH), lambda i: (i // nt, 0, i % nt, 0)),
        pl.BlockSpec((1, A_HEADS, 1, A_DV, tm), lambda i: (i // nt, 0, i % nt, 0, 0)),
    ]
    return pl.pallas_call(
        _proj_kernel, grid=(steps,), in_specs=in_specs, out_specs=out_specs, out_shape=out_shape,
        compiler_params=pltpu.CompilerParams(dimension_semantics=("arbitrary",),
                                             vmem_limit_bytes=VMEM_LIMIT),
        name="proj",
    )(xf, pos3, gmix, wmain, wtail, gcq, gckv, wuqT, wuk, wuvT, gqcol, gk, invf, bg)


def _conv_kernel(cur_ref, prev_ref, next_ref, w_ref, b_ref, qk_ref, kT_ref, *, tiles_per_seq):
    t = pl.program_id(0) % tiles_per_seq
    tm = cur_ref.shape[0]
    prev = jnp.where(t == 0, 0.0, prev_ref[...])
    nxt = jnp.where(t == tiles_per_seq - 1, 0.0, next_ref[...])
    ext = jnp.concatenate([prev, cur_ref[...], nxt], axis=0)
    cw = w_ref[...]
    u = b_ref[...]
    for j in range(M_CONV):
        off = 8 - M_CONV // 2 + j
        u = u + cw[j:j + 1, :] * ext[off:off + tm, :]
    qk = u * jax.nn.sigmoid(u)
    k = qk[:, 256:512]
    qk_ref[:, 0:256] = (qk[:, 0:256] * (M_DQK ** -0.5)).astype(BF16)
    qk_ref[:, 256:512] = k.astype(BF16)
    kT_ref[0] = k.T.astype(BF16)


def _conv_call(qk, convw, convb, batch, seq):
    n = qk.shape[0]
    tm = TOKEN_TILE
    nt = seq // tm
    r8 = tm // 8
    return pl.pallas_call(
        functools.partial(_conv_kernel, tiles_per_seq=nt),
        grid=(n // tm,),
        in_specs=[
            pl.BlockSpec((tm, 512), lambda i: (i, 0)),
            pl.BlockSpec((8, 512), lambda i: (jnp.maximum(i * r8 - 1, 0), 0)),
            pl.BlockSpec((8, 512), lambda i: (jnp.minimum((i + 1) * r8, n // 8 - 1), 0)),
            pl.BlockSpec(convw.shape, lambda i: (0, 0)),
            pl.BlockSpec(convb.shape, lambda i: (0, 0)),
        ],
        out_specs=[pl.BlockSpec((tm, 512), lambda i: (i, 0)),
                   pl.BlockSpec((1, 256, tm), lambda i: (i // nt, 0, i % nt))],
        out_shape=[jax.ShapeDtypeStruct((n, 512), BF16),
                   jax.ShapeDtypeStruct((batch, 256, seq), BF16)],
        compiler_params=pltpu.CompilerParams(dimension_semantics=("arbitrary",),
                                             vmem_limit_bytes=VMEM_LIMIT),
        name="conv",
    )(qk, qk, qk, convw, convb)


def _mlstm_kernel(qkf_ref, kTf_ref, vf_ref, sf_ref, b3f_ref, qkb_ref, kTb_ref, vb_ref, sb_ref,
                  b3b_ref, esel_ref, hf_ref, hb_ref, cf_sc, cb_sc, mf_sc, mb_sc):
    L = M_CHUNK
    n_sub = qkf_ref.shape[1] // L

    @pl.when(pl.program_id(1) == 0)
    def _():
        cf_sc[...] = jnp.zeros_like(cf_sc)
        cb_sc[...] = jnp.zeros_like(cb_sc)
        mf_sc[...] = jnp.zeros_like(mf_sc)
        mb_sc[...] = jnp.zeros_like(mb_sc)

    esel = esel_ref[...]
    row = lax.broadcasted_iota(jnp.int32, (L, L), 0)
    col = lax.broadcasted_iota(jnp.int32, (L, L), 1)
    ones_blk = jnp.ones((L, M_DV), BF16)
    fw = dict(qk=qkf_ref, kT=kTf_ref, v=vf_ref, s=sf_ref, b3=b3f_ref, out=hf_ref, c_sc=cf_sc,
              m_sc=mf_sc, visible=col <= row)
    bw = dict(qk=qkb_ref, kT=kTb_ref, v=vb_ref, s=sb_ref, b3=b3b_ref, out=hb_ref, c_sc=cb_sc,
              m_sc=mb_sc, visible=col >= row)

    work = []
    for d in (fw, bw):
        m = d["m_sc"][...]
        order = range(n_sub) if d is fw else range(n_sub - 1, -1, -1)
        for i in order:
            sl = slice(i * L, (i + 1) * L)
            stats = d["s"][0, :, sl]
            a, g = stats[16:24], stats[24:32]
            m_new = jnp.maximum(a + m, g)
            work.append((d, sl, dict(
                w=stats[0:8], r=stats[8:16], m_prev=m,
                dec=jnp.exp2(a + m - m_new), add=jnp.exp2(g - m_new),
                brep=_dot_tn(d["b3"][0, :, sl], esel))))
            m = m_new
        d["m_sc"][...] = m
    work = [work[j + k * n_sub] for j in range(n_sub) for k in range(2)]
    units = [(d, sl, gt, h) for d, sl, gt in work for h in range(M_HEADS)]

    grams, v_augs, s_augs = [], [], []
    for d, sl, gt, h in units:
        q_h = d["qk"][0, sl, h * M_DQK:(h + 1) * M_DQK]
        k_h = d["qk"][0, sl, 256 + h * M_DQK:256 + (h + 1) * M_DQK]
        grams.append(_dot_nt(q_h, k_h))
        v_augs.append(jnp.concatenate([d["v"][0, sl, h * M_DV:(h + 1) * M_DV], ones_blk], axis=1))
    for (d, sl, gt, h), v_aug in zip(units, v_augs):
        kTw = d["kT"][0, h * M_DQK:(h + 1) * M_DQK, sl].astype(F32) * gt["w"][h:h + 1, :]
        s_augs.append(_dot(kTw.astype(BF16), v_aug))

    states = {id(fw): [cf_sc[h] for h in range(M_HEADS)],
              id(bw): [cb_sc[h] for h in range(M_HEADS)]}
    for (d, sl, gt, h), gram, v_aug, s_aug in zip(units, grams, v_augs, s_augs):
        q_h = d["qk"][0, sl, h * M_DQK:(h + 1) * M_DQK]
        brep = gt["brep"][:, h * LANES:(h + 1) * LANES]
        bm = brep + gt["m_prev"][h:h + 1, :]
        dmat = jnp.where(d["visible"], brep + gt["r"][h:h + 1, :], -jnp.inf)
        m_t = jnp.maximum(bm, jnp.max(dmat, axis=1, keepdims=True))
        p = jnp.exp2(dmat - m_t) * gram
        inter = jnp.exp2(bm - m_t)
        c_prev = states[id(d)][h]
        lhs = jnp.concatenate(
            [p.astype(BF16), (inter[:, 0:M_DQK] * q_h.astype(F32)).astype(BF16)], axis=1)
        rhs = jnp.concatenate([v_aug, c_prev.astype(BF16)], axis=0)
        res = _dot(lhs, rhs)
        den = jnp.maximum(jnp.abs(res[:, M_DV:2 * M_DV]), jnp.exp2(-m_t))
        d["out"][0, sl, h * M_DV:(h + 1) * M_DV] = res[:, 0:M_DV] / den
        dec_h = jnp.concatenate([gt["dec"][h:h + 1, :]] * 2, axis=1)
        add_h = jnp.concatenate([gt["add"][h:h + 1, :]] * 2, axis=1)
        states[id(d)][h] = dec_h * c_prev + add_h * s_aug
    for h in range(M_HEADS):
        cf_sc[h] = states[id(fw)][h]
        cb_sc[h] = states[id(bw)][h]


def _mlstm_call(qk3, kT3, v3, sf, b3f, sb, b3b, esel):
    batch, seq, _ = qk3.shape
    L = M_CHUNK * MLSTM_CHUNKS_PER_STEP
    nc = seq // L
    fw_rows = lambda w: pl.BlockSpec((1, L, w), lambda b, c: (b, c, 0))
    bw_rows = lambda w: pl.BlockSpec((1, L, w), lambda b, c: (b, nc - 1 - c, 0))
    fw_lanes = lambda r: pl.BlockSpec((1, r, L), lambda b, c: (b, 0, c))
    bw_lanes = lambda r: pl.BlockSpec((1, r, L), lambda b, c: (b, 0, nc - 1 - c))
    state = pltpu.VMEM((M_HEADS, M_DQK, 2 * M_DV), F32)
    return pl.pallas_call(
        _mlstm_kernel,
        grid=(batch, nc),
        in_specs=[fw_rows(512), fw_lanes(256), fw_rows(512), fw_lanes(32), fw_lanes(32),
                  bw_rows(512), bw_lanes(256), bw_rows(512), bw_lanes(32), bw_lanes(32),
                  pl.BlockSpec(esel.shape, lambda b, c: (0, 0))],
        out_specs=[fw_rows(512), bw_rows(512)],
        out_shape=[jax.ShapeDtypeStruct((batch, seq, 512), F32)] * 2,
        scratch_shapes=[state, state, pltpu.VMEM((8, M_CHUNK), F32), pltpu.VMEM((8, M_CHUNK), F32)],
        compiler_params=pltpu.CompilerParams(dimension_semantics=("arbitrary", "arbitrary"),
                                             vmem_limit_bytes=VMEM_LIMIT),
        name="mlstm",
    )(qk3, kT3, v3, sf, b3f, qk3, kT3, v3, sb, b3b, esel)


def _attn_kernel(qT_ref, k_ref, vT_ref, o_ref, s_sc, cm_sc, acc_sc, *, n_kv):
    tk = KV_TILE
    sub = tk // TOKEN_TILE
    qT = jnp.concatenate([qT_ref[0, 0, i] for i in range(qT_ref.shape[2])], axis=1)
    tq = qT.shape[1]

    def scores(j, slot):
        sT = _dot(k_ref[0, 0, j * tk:(j + 1) * tk, :], qT)
        s_sc[slot] = sT
        cm_sc[slot] = jnp.max(sT, axis=0, keepdims=True)

    m = jnp.full((1, tq), -jnp.inf, F32)
    l = jnp.zeros((1, tq), F32)
    scores(0, 0)
    for j in range(n_kv):
        slot = j % 2
        if j + 1 < n_kv:
            scores(j + 1, 1 - slot)
        m_new = jnp.maximum(m, cm_sc[slot])
        alpha = jnp.exp2(m - m_new)
        p = jnp.exp2(s_sc[slot] - m_new)
        l = alpha * l + jnp.sum(p, axis=0, keepdims=True)
        vt = jnp.concatenate([vT_ref[0, 0, j * sub + i] for i in range(sub)], axis=1)
        pv = _dot(vt, p.astype(BF16))
        if j == 0:
            acc_sc[...] = pv
        else:
            acc_sc[...] = alpha * acc_sc[...] + pv
        m = m_new
    o_ref[0] = (acc_sc[...] / l).T.astype(o_ref.dtype)


def _attn_bounded_kernel(qT_ref, k_ref, vT_ref, o_ref, acc_sc, *, n_kv):
    tk = KV_TILE
    sub = tk // TOKEN_TILE
    qT = jnp.concatenate([qT_ref[0, 0, i] for i in range(qT_ref.shape[2])], axis=1)
    l = jnp.zeros((1, qT.shape[1]), F32)
    for j in range(n_kv):
        p = jnp.exp2(_dot(k_ref[0, 0, j * tk:(j + 1) * tk, :], qT))
        l = l + jnp.sum(p, axis=0, keepdims=True)
        vt = jnp.concatenate([vT_ref[0, 0, j * sub + i] for i in range(sub)], axis=1)
        pv = _dot(vt, p.astype(BF16))
        if j == 0:
            acc_sc[...] = pv
        else:
            acc_sc[...] += pv
    o_ref[0] = (acc_sc[...] / l).T.astype(o_ref.dtype)


def _attn_call(qT, k, vT, score_bound_log2):
    batch, heads, nt, _, tm = qT.shape
    seq = k.shape[2]
    tq = Q_TILE
    n_kv = seq // KV_TILE
    slabs = tq // tm

    def call(body, scratch, name):
        return pl.pallas_call(
            functools.partial(body, n_kv=n_kv),
            grid=(batch, heads, seq // tq),
            in_specs=[
                pl.BlockSpec((1, 1, slabs, A_DH, tm), lambda b, h, i: (b, h, i, 0, 0)),
                pl.BlockSpec((1, 1, seq, A_DH), lambda b, h, i: (b, h, 0, 0)),
                pl.BlockSpec((1, 1, nt, A_DV, tm), lambda b, h, i: (b, h, 0, 0, 0)),
            ],
            out_specs=pl.BlockSpec((1, tq, A_DV), lambda b, h, i: (b, i, h)),
            out_shape=jax.ShapeDtypeStruct((batch, seq, heads * A_DV), BF16),
            scratch_shapes=scratch,
            compiler_params=pltpu.CompilerParams(
                dimension_semantics=("arbitrary", "arbitrary", "arbitrary"),
                vmem_limit_bytes=VMEM_LIMIT),
            name=name,
        )(qT, k, vT)

    acc = pltpu.VMEM((A_DV, tq), F32)
    return lax.cond(
        score_bound_log2 <= SAFE_SCORE_LOG2,
        lambda: call(_attn_bounded_kernel, [acc], "attn_bounded"),
        lambda: call(_attn_kernel, [pltpu.VMEM((2, KV_TILE, tq), F32),
                                    pltpu.VMEM((2, 1, tq), F32), acc], "attn"))


def _out_kernel(x_ref, hf_ref, hb_ref, o_ref, yb_ref, gml_ref, wout_ref, gffn_ref, w1_ref, w2_ref,
                y_ref):
    x = x_ref[...]
    hsum = hf_ref[...] + hb_ref[...]
    gate = jax.nn.sigmoid(o_ref[...])
    gml = gml_ref[...]
    parts = []
    for h in range(M_HEADS):
        sl = slice(h * M_DV, (h + 1) * M_DV)
        parts.append((gate[:, sl] * _rms(hsum[:, sl], gml[:, sl])).astype(BF16))
    y_a = jnp.concatenate(parts, axis=1)
    n_a = y_a.shape[1]
    x1 = x + _dot(y_a, wout_ref[0:n_a, :]) + _dot(yb_ref[...], wout_ref[n_a:, :])
    hn = _rms(x1, gffn_ref[...]).astype(BF16)
    y_ref[...] = x1
    d_ff = w1_ref.shape[1]
    for j in range(d_ff // FF_CHUNK):
        sl = slice(j * FF_CHUNK, (j + 1) * FF_CHUNK)
        t = jnp.maximum(_dot(hn, w1_ref[:, sl]), 0.0)
        y_ref[...] += _dot((t * t).astype(BF16), w2_ref[sl, :])


def _out_call(xf, hf, hb, o, yb, gml, wout, gffn, w1, w2):
    n, d = xf.shape
    tm = TOKEN_TILE

    def full(a):
        return pl.BlockSpec(a.shape, lambda i: (0,) * a.ndim)

    tok = lambda w: pl.BlockSpec((tm, w), lambda i: (i, 0))
    in_specs = [tok(d), tok(512), tok(512), tok(512), tok(512), full(gml), full(wout), full(gffn),
                full(w1), full(w2)]
    return pl.pallas_call(
        _out_kernel, grid=(n // tm,), in_specs=in_specs, out_specs=tok(d),
        out_shape=jax.ShapeDtypeStruct((n, d), F32),
        compiler_params=pltpu.CompilerParams(dimension_semantics=("arbitrary",),
                                             vmem_limit_bytes=VMEM_LIMIT),
        name="out",
    )(xf, hf, hb, o, yb, gml, wout, gffn, w1, w2)


def _layer(xf, pos3, invf, batch, seq, g_mix_norm, w_in, conv_w, conv_b, b_gates, g_mlstm_out,
           g_cq, g_ckv, w_uq, w_ukv, g_q, g_k, w_out, g_ffn_norm, w_ff1, w_ff2):
    d = xf.shape[1]
    c0 = 2 * 256 + 2 * 512
    wmain = w_in[:, 0:c0].astype(BF16)
    wt = w_in[:, c0:].astype(BF16)
    gates_w = wt[:, 0:N_GATES]
    cq_w = wt[:, N_GATES:N_GATES + 256]
    ckv_w = wt[:, N_GATES + 256:N_GATES + 384]
    kpe_w = wt[:, N_GATES + 384:N_GATES + 448]
    wtail = jnp.concatenate([cq_w, ckv_w, kpe_w, kpe_w[:, 0:32], gates_w,
                             jnp.zeros((d, LANES - 96 - N_GATES), BF16)], axis=1)
    wuqT = w_uq.T.astype(BF16)
    ukv = w_ukv.reshape(w_ukv.shape[0], A_HEADS, A_NOPE + A_DV)
    wuk = ukv[:, :, :A_NOPE].reshape(w_ukv.shape[0], A_HEADS * A_NOPE).astype(BF16)
    wuvT = ukv[:, :, A_NOPE:].reshape(w_ukv.shape[0], A_HEADS * A_DV).T.astype(BF16)
    gqcol = g_q.reshape(A_DH, 1)
    gk = jnp.concatenate([g_k, g_k[A_NOPE + 32:], g_k[A_NOPE:A_NOPE + 32]]).reshape(1, -1)

    qk, v, o, sf, b3f, sb, b3b, qT, k, vT = _proj_call(
        xf, pos3, g_mix_norm.reshape(1, -1), wmain, wtail, g_cq.reshape(1, -1),
        g_ckv.reshape(1, -1), wuqT, wuk, wuvT, gqcol, gk, invf, b_gates.reshape(2, 8, 1), batch, seq)

    qkc, kT = _conv_call(qk, conv_w, conv_b.reshape(1, -1), batch, seq)
    esel = jnp.tile(jnp.repeat(jnp.eye(8, M_HEADS, dtype=BF16), LANES, axis=1), (4, 1))
    hf, hb = _mlstm_call(qkc.reshape(batch, seq, 512), kT, v.reshape(batch, seq, 512),
                         sf, b3f, sb, b3b, esel)
    score_bound_log2 = (A_DH ** 0.5 * LOG2E * 1.01) * jnp.max(jnp.abs(g_q)) * jnp.max(jnp.abs(g_k))
    yb = _attn_call(qT, k, vT, score_bound_log2)
    return _out_call(xf, hf.reshape(batch * seq, -1), hb.reshape(batch * seq, -1), o,
                     yb.reshape(batch * seq, -1), g_mlstm_out.reshape(1, -1), w_out.astype(BF16),
                     g_ffn_norm.reshape(1, -1), w_ff1.astype(BF16), w_ff2.astype(BF16))


def kernel(x, positions, g_mix_norm, w_in, conv_w, conv_b, b_gates, g_mlstm_out, g_cq, g_ckv,
           w_uq, w_ukv, g_q, g_k, w_out, g_ffn_norm, w_ff1, w_ff2):
    batch, seq, d = x.shape
    assert seq % TOKEN_TILE == 0 and seq % M_CHUNK == 0
    inv_freq = ROPE_THETA ** (-jnp.arange(0, A_ROPE, 2, dtype=F32) / A_ROPE)
    invf = inv_freq.reshape(A_ROPE // 2, 1)
    pos3 = positions.reshape(batch * seq // TOKEN_TILE, 1, TOKEN_TILE)
    xf = x.reshape(batch * seq, d)
    for l in range(w_in.shape[0]):
        xf = _layer(xf, pos3, invf, batch, seq, g_mix_norm[l], w_in[l], conv_w[l], conv_b[l],
                    b_gates[l], g_mlstm_out[l], g_cq[l], g_ckv[l], w_uq[l], w_ukv[l], g_q[l],
                    g_k[l], w_out[l], g_ffn_norm[l], w_ff1[l], w_ff2[l])
    return xf.reshape(batch, seq, d)
```

```python
import functools
import math

import jax
import jax.numpy as jnp
from jax import lax
from jax.experimental import pallas as pl
from jax.experimental.pallas import tpu as pltpu

F32 = jnp.float32
BF16 = jnp.bfloat16

EPS = 1e-6
ROPE_THETA = 10000.0
M_HEADS = 4
M_DQK = 64
M_DV = 128
M_CONV = 5
M_CHUNK = 128
A_HEADS = 4
A_NOPE = 128
A_ROPE = 64
A_DV = 128
A_DH = A_NOPE + A_ROPE
N_GATES = 4 * M_HEADS

LANES = 128
TOKEN_TILE = 512
KV_TILE = 1024
Q_TILE = 1024
MLSTM_CHUNKS_PER_STEP = 8
SAFE_SCORE_LOG2 = 64.0
FF_CHUNK = 1024
VMEM_LIMIT = 56 * 1024 * 1024

LOG2E = math.log2(math.e)


def _dot(a, b):
    return jnp.dot(a, b, preferred_element_type=F32)


def _dot_nt(a, b):
    return lax.dot_general(a, b, (((1,), (1,)), ((), ())), preferred_element_type=F32)


def _dot_tn(a, b):
    return lax.dot_general(a, b, (((0,), (0,)), ((), ())), preferred_element_type=F32)


def _rms(x, g):
    ms = jnp.mean(x * x, axis=-1, keepdims=True)
    return x * lax.rsqrt(ms + EPS) * g


def _split3(y):
    hi = y.astype(BF16).astype(F32)
    mid = (y - hi).astype(BF16).astype(F32)
    lo = y - hi - mid
    return jnp.concatenate([hi, mid, lo], axis=0).astype(BF16)


def _chunk_decay(x, *, backward):
    L = M_CHUNK
    row = lax.broadcasted_iota(jnp.int32, (L, L), 0)
    col = lax.broadcasted_iota(jnp.int32, (L, L), 1)
    umat = jnp.where((row >= col) if backward else (row <= col), 1.0, 0.0).astype(BF16)
    lf = jax.nn.log_sigmoid(x)
    cum3 = _dot(_split3(lf), umat)
    cum = cum3[0:8] + cum3[8:16] + cum3[16:24]
    a = jnp.broadcast_to(jnp.sum(lf, axis=1, keepdims=True), (8, L))
    b = pltpu.roll(cum, 4, axis=0)
    a = pltpu.roll(a, 4, axis=0)
    w_log = a - b + x
    g = jnp.broadcast_to(jnp.max(w_log, axis=1, keepdims=True), (8, L))
    stats = jnp.concatenate([jnp.exp(w_log - g), (x - b) * LOG2E, a * LOG2E, g * LOG2E], axis=0)
    b3 = jnp.concatenate([_split3(b * LOG2E), jnp.zeros((8, L), BF16)], axis=0)
    return stats, b3


def _proj_kernel(x_ref, pos_ref, gmix_ref, wmain_ref, wtail_ref, gcq_ref, gckv_ref, wuqT_ref,
                 wuk_ref, wuvT_ref, gqcol_ref, gk_ref, invf_ref, bg_ref,
                 qk_ref, v_ref, o_ref, sf_ref, b3f_ref, sb_ref, b3b_ref, qT_ref, k_ref, vT_ref):
    x = x_ref[...]
    hn = _rms(x, gmix_ref[...]).astype(BF16)
    z = _dot(hn, wmain_ref[...])
    qk_ref[...] = z[:, 0:512]
    v_ref[...] = z[:, 512:1024].astype(BF16)
    o_ref[...] = z[:, 1024:1536]
    zt = _dot(hn, wtail_ref[...])
    cq = zt[:, 0:256]
    ckv = zt[:, 256:384]
    tail = zt[:, 384:512]

    tailT = tail.T
    for d, (s_ref, b3_ref) in enumerate(((sf_ref, b3f_ref), (sb_ref, b3b_ref))):
        gx = tailT[96 + 8 * d:104 + 8 * d] + bg_ref[d]
        for j in range(gx.shape[1] // M_CHUNK):
            sl = slice(j * M_CHUNK, (j + 1) * M_CHUNK)
            stats, b3 = _chunk_decay(gx[:, sl], backward=d == 1)
            s_ref[0, :, sl] = stats
            b3_ref[0, :, sl] = b3

    pos = pos_ref[0].astype(F32)
    angT = invf_ref[...] * pos
    cosT = jnp.cos(angT)
    sinT = jnp.sin(angT)

    cqnT = _rms(cq, gcq_ref[...]).T.astype(BF16)
    qT = _dot(wuqT_ref[...], cqnT)
    gq = gqcol_ref[...] * (A_DH ** -0.5 * LOG2E)
    for h in range(A_HEADS):
        qh = qT[h * A_DH:(h + 1) * A_DH]
        ms = jnp.sum(qh * qh, axis=0, keepdims=True) * (1.0 / A_DH)
        qn = qh * lax.rsqrt(ms + EPS) * gq
        x1 = qn[A_NOPE:A_NOPE + 32]
        x2 = qn[A_NOPE + 32:A_DH]
        qT_ref[0, h, 0, 0:A_NOPE, :] = qn[0:A_NOPE].astype(BF16)
        qT_ref[0, h, 0, A_NOPE:A_NOPE + 32, :] = (x1 * cosT - x2 * sinT).astype(BF16)
        qT_ref[0, h, 0, A_NOPE + 32:A_DH, :] = (x2 * cosT + x1 * sinT).astype(BF16)

    ckvn = _rms(ckv, gckv_ref[...])
    knope = _dot(ckvn.astype(BF16), wuk_ref[...])
    vT = _dot(wuvT_ref[...], ckvn.T.astype(BF16))
    for h in range(A_HEADS):
        vT_ref[0, h, 0] = vT[h * A_DV:(h + 1) * A_DV].astype(BF16)

    tab = jnp.concatenate([cosT, cosT, -sinT, sinT], axis=0).T
    cc = tab[:, 0:A_ROPE]
    ss = tab[:, A_ROPE:2 * A_ROPE]
    gk = gk_ref[...]
    kpe = tail[:, 0:A_ROPE]
    kpe_sw = pltpu.roll(tail, LANES - 32, axis=1)[:, 0:A_ROPE]
    ss_pe = jnp.sum(kpe * kpe, axis=-1, keepdims=True)
    krope = (kpe * gk[:, A_NOPE:A_DH] * cc + kpe_sw * gk[:, A_DH:A_DH + A_ROPE] * ss)
    for h in range(A_HEADS):
        kn = knope[:, h * A_NOPE:(h + 1) * A_NOPE]
        ms = (jnp.sum(kn * kn, axis=-1, keepdims=True) + ss_pe) * (1.0 / A_DH)
        r = lax.rsqrt(ms + EPS)
        k_ref[0, h, :, 0:A_NOPE] = (kn * r * gk[:, 0:A_NOPE]).astype(BF16)
        k_ref[0, h, :, A_NOPE:A_DH] = (krope * r).astype(BF16)


def _proj_call(xf, pos3, gmix, wmain, wtail, gcq, gckv, wuqT, wuk, wuvT, gqcol, gk, invf, bg,
               batch, seq):
    n, d = xf.shape
    tm = TOKEN_TILE
    nt = seq // tm
    steps = n // tm

    def full(a):
        return pl.BlockSpec(a.shape, lambda i: (0,) * a.ndim)

    tok = lambda w: pl.BlockSpec((tm, w), lambda i: (i, 0))
    in_specs = [tok(d), pl.BlockSpec((1, 1, tm), lambda i: (i, 0, 0)), full(gmix), full(wmain),
                full(wtail), full(gcq), full(gckv), full(wuqT), full(wuk), full(wuvT), full(gqcol),
                full(gk), full(invf), full(bg)]
    stat_spec = pl.BlockSpec((1, 32, tm), lambda i: (i // nt, 0, i % nt))
    out_shape = [
        jax.ShapeDtypeStruct((n, 512), F32),
        jax.ShapeDtypeStruct((n, 512), BF16),
        jax.ShapeDtypeStruct((n, 512), F32),
        jax.ShapeDtypeStruct((batch, 32, seq), F32),
        jax.ShapeDtypeStruct((batch, 32, seq), BF16),
        jax.ShapeDtypeStruct((batch, 32, seq), F32),
        jax.ShapeDtypeStruct((batch, 32, seq), BF16),
        jax.ShapeDtypeStruct((batch, A_HEADS, nt, A_DH, tm), BF16),
        jax.ShapeDtypeStruct((batch, A_HEADS, seq, A_DH), BF16),
        jax.ShapeDtypeStruct((batch, A_HEADS, nt, A_DV, tm), BF16),
    ]
    out_specs = [
        tok(512), tok(512), tok(512), stat_spec, stat_spec, stat_spec, stat_spec,
        pl.BlockSpec((1, A_HEADS, 1, A_DH, tm), lambda i: (i // nt, 0, i % nt, 0, 0)),
        pl.BlockSpec((1, A_HEADS, tm, A_DH), lambda i: (i // nt, 0, i % nt, 0)),
        pl.BlockSpec((1, A_HEADS, 1, A_DV, tm), lambda i: (i // nt, 0, i % nt, 0, 0)),
    ]
    return pl.pallas_call(
        _proj_kernel, grid=(steps,), in_specs=in_specs, out_specs=out_specs, out_shape=out_shape,
        compiler_params=pltpu.CompilerParams(dimension_semantics=("arbitrary",),
                                             vmem_limit_bytes=VMEM_LIMIT),
        name="proj",
    )(xf, pos3, gmix, wmain, wtail, gcq, gckv, wuqT, wuk, wuvT, gqcol, gk, invf, bg)


def _conv_kernel(cur_ref, prev_ref, next_ref, w_ref, b_ref, qk_ref, kT_ref, *, tiles_per_seq):
    t = pl.program_id(0) % tiles_per_seq
    tm = cur_ref.shape[0]
    prev = jnp.where(t == 0, 0.0, prev_ref[...])
    nxt = jnp.where(t == tiles_per_seq - 1, 0.0, next_ref[...])
    ext = jnp.concatenate([prev, cur_ref[...], nxt], axis=0)
    cw = w_ref[...]
    u = b_ref[...]
    for j in range(M_CONV):
        off = 8 - M_CONV // 2 + j
        u = u + cw[j:j + 1, :] * ext[off:off + tm, :]
    qk = u * jax.nn.sigmoid(u)
    k = qk[:, 256:512]
    qk_ref[:, 0:256] = (qk[:, 0:256] * (M_DQK ** -0.5)).astype(BF16)
    qk_ref[:, 256:512] = k.astype(BF16)
    kT_ref[0] = k.T.astype(BF16)


def _conv_call(qk, convw, convb, batch, seq):
    n = qk.shape[0]
    tm = TOKEN_TILE
    nt = seq // tm
    r8 = tm // 8
    return pl.pallas_call(
        functools.partial(_conv_kernel, tiles_per_seq=nt),
        grid=(n // tm,),
        in_specs=[
            pl.BlockSpec((tm, 512), lambda i: (i, 0)),
            pl.BlockSpec((8, 512), lambda i: (jnp.maximum(i * r8 - 1, 0), 0)),
            pl.BlockSpec((8, 512), lambda i: (jnp.minimum((i + 1) * r8, n // 8 - 1), 0)),
            pl.BlockSpec(convw.shape, lambda i: (0, 0)),
            pl.BlockSpec(convb.shape, lambda i: (0, 0)),
        ],
        out_specs=[pl.BlockSpec((tm, 512), lambda i: (i, 0)),
                   pl.BlockSpec((1, 256, tm), lambda i: (i // nt, 0, i % nt))],
        out_shape=[jax.ShapeDtypeStruct((n, 512), BF16),
                   jax.ShapeDtypeStruct((batch, 256, seq), BF16)],
        compiler_params=pltpu.CompilerParams(dimension_semantics=("arbitrary",),
                                             vmem_limit_bytes=VMEM_LIMIT),
        name="conv",
    )(qk, qk, qk, convw, convb)


def _mlstm_kernel(qkf_ref, kTf_ref, vf_ref, sf_ref, b3f_ref, qkb_ref, kTb_ref, vb_ref, sb_ref,
                  b3b_ref, esel_ref, hf_ref, hb_ref, cf_sc, cb_sc, mf_sc, mb_sc):
    L = M_CHUNK
    n_sub = qkf_ref.shape[1] // L

    @pl.when(pl.program_id(1) == 0)
    def _():
        cf_sc[...] = jnp.zeros_like(cf_sc)
        cb_sc[...] = jnp.zeros_like(cb_sc)
        mf_sc[...] = jnp.zeros_like(mf_sc)
        mb_sc[...] = jnp.zeros_like(mb_sc)

    esel = esel_ref[...]
    row = lax.broadcasted_iota(jnp.int32, (L, L), 0)
    col = lax.broadcasted_iota(jnp.int32, (L, L), 1)
    ones_blk = jnp.ones((L, M_DV), BF16)
    fw = dict(qk=qkf_ref, kT=kTf_ref, v=vf_ref, s=sf_ref, b3=b3f_ref, out=hf_ref, c_sc=cf_sc,
              m_sc=mf_sc, visible=col <= row)
    bw = dict(qk=qkb_ref, kT=kTb_ref, v=vb_ref, s=sb_ref, b3=b3b_ref, out=hb_ref, c_sc=cb_sc,
              m_sc=mb_sc, visible=col >= row)

    work = []
    for d in (fw, bw):
        m = d["m_sc"][...]
        order = range(n_sub) if d is fw else range(n_sub - 1, -1, -1)
        for i in order:
            sl = slice(i * L, (i + 1) * L)
            stats = d["s"][0, :, sl]
            a, g = stats[16:24], stats[24:32]
            m_new = jnp.maximum(a + m, g)
            work.append((d, sl, dict(
                w=stats[0:8], r=stats[8:16], m_prev=m,
                dec=jnp.exp2(a + m - m_new), add=jnp.exp2(g - m_new),
                brep=_dot_tn(d["b3"][0, :, sl], esel))))
            m = m_new
        d["m_sc"][...] = m
    work = [work[j + k * n_sub] for j in range(n_sub) for k in range(2)]
    units = [(d, sl, gt, h) for d, sl, gt in work for h in range(M_HEADS)]

    grams, v_augs, s_augs = [], [], []
    for d, sl, gt, h in units:
        q_h = d["qk"][0, sl, h * M_DQK:(h + 1) * M_DQK]
        k_h = d["qk"][0, sl, 256 + h * M_DQK:256 + (h + 1) * M_DQK]
        grams.append(_dot_nt(q_h, k_h))
        v_augs.append(jnp.concatenate([d["v"][0, sl, h * M_DV:(h + 1) * M_DV], ones_blk], axis=1))
    for (d, sl, gt, h), v_aug in zip(units, v_augs):
        kTw = d["kT"][0, h * M_DQK:(h + 1) * M_DQK, sl].astype(F32) * gt["w"][h:h + 1, :]
        s_augs.append(_dot(kTw.astype(BF16), v_aug))

    states = {id(fw): [cf_sc[h] for h in range(M_HEADS)],
              id(bw): [cb_sc[h] for h in range(M_HEADS)]}
    for (d, sl, gt, h), gram, v_aug, s_aug in zip(units, grams, v_augs, s_augs):
        q_h = d["qk"][0, sl, h * M_DQK:(h + 1) * M_DQK]
        brep = gt["brep"][:, h * LANES:(h + 1) * LANES]
        bm = brep + gt["m_prev"][h:h + 1, :]
        dmat = jnp.where(d["visible"], brep + gt["r"][h:h + 1, :], -jnp.inf)
        m_t = jnp.maximum(bm, jnp.max(dmat, axis=1, keepdims=True))
        p = jnp.exp2(dmat - m_t) * gram
        inter = jnp.exp2(bm - m_t)
        c_prev = states[id(d)][h]
        lhs = jnp.concatenate(
            [p.astype(BF16), (inter[:, 0:M_DQK] * q_h.astype(F32)).astype(BF16)], axis=1)
        rhs = jnp.concatenate([v_aug, c_prev.astype(BF16)], axis=0)
        res = _dot(lhs, rhs)
        den = jnp.maximum(jnp.abs(res[:, M_DV:2 * M_DV]), jnp.exp2(-m_t))
        d["out"][0, sl, h * M_DV:(h + 1) * M_DV] = res[:, 0:M_DV] / den
        dec_h = jnp.concatenate([gt["dec"][h:h + 1, :]] * 2, axis=1)
        add_h = jnp.concatenate([gt["add"][h:h + 1, :]] * 2, axis=1)
        states[id(d)][h] = dec_h * c_prev + add_h * s_aug
    for h in range(M_HEADS):
        cf_sc[h] = states[id(fw)][h]
        cb_sc[h] = states[id(bw)][h]


def _mlstm_call(qk3, kT3, v3, sf, b3f, sb, b3b, esel):
    batch, seq, _ = qk3.shape
    L = M_CHUNK * MLSTM_CHUNKS_PER_STEP
    nc = seq // L
    fw_rows = lambda w: pl.BlockSpec((1, L, w), lambda b, c: (b, c, 0))
    bw_rows = lambda w: pl.BlockSpec((1, L, w), lambda b, c: (b, nc - 1 - c, 0))
    fw_lanes = lambda r: pl.BlockSpec((1, r, L), lambda b, c: (b, 0, c))
    bw_lanes = lambda r: pl.BlockSpec((1, r, L), lambda b, c: (b, 0, nc - 1 - c))
    state = pltpu.VMEM((M_HEADS, M_DQK, 2 * M_DV), F32)
    return pl.pallas_call(
        _mlstm_kernel,
        grid=(batch, nc),
        in_specs=[fw_rows(512), fw_lanes(256), fw_rows(512), fw_lanes(32), fw_lanes(32),
                  bw_rows(512), bw_lanes(256), bw_rows(512), bw_lanes(32), bw_lanes(32),
                  pl.BlockSpec(esel.shape, lambda b, c: (0, 0))],
        out_specs=[fw_rows(512), bw_rows(512)],
        out_shape=[jax.ShapeDtypeStruct((batch, seq, 512), F32)] * 2,
        scratch_shapes=[state, state, pltpu.VMEM((8, M_CHUNK), F32), pltpu.VMEM((8, M_CHUNK), F32)],
        compiler_params=pltpu.CompilerParams(dimension_semantics=("arbitrary", "arbitrary"),
                                             vmem_limit_bytes=VMEM_LIMIT),
        name="mlstm",
    )(qk3, kT3, v3, sf, b3f, qk3, kT3, v3, sb, b3b, esel)


def _attn_kernel(qT_ref, k_ref, vT_ref, o_ref, s_sc, cm_sc, acc_sc, *, n_kv):
    tk = KV_TILE
    sub = tk // TOKEN_TILE
    qT = jnp.concatenate([qT_ref[0, 0, i] for i in range(qT_ref.shape[2])], axis=1)
    tq = qT.shape[1]

    def scores(j, slot):
        sT = _dot(k_ref[0, 0, j * tk:(j + 1) * tk, :], qT)
        s_sc[slot] = sT
        cm_sc[slot] = jnp.max(sT, axis=0, keepdims=True)

    m = jnp.full((1, tq), -jnp.inf, F32)
    l = jnp.zeros((1, tq), F32)
    scores(0, 0)
    for j in range(n_kv):
        slot = j % 2
        if j + 1 < n_kv:
            scores(j + 1, 1 - slot)
        m_new = jnp.maximum(m, cm_sc[slot])
        alpha = jnp.exp2(m - m_new)
        p = jnp.exp2(s_sc[slot] - m_new)
        l = alpha * l + jnp.sum(p, axis=0, keepdims=True)
        vt = jnp.concatenate([vT_ref[0, 0, j * sub + i] for i in range(sub)], axis=1)
        pv = _dot(vt, p.astype(BF16))
        if j == 0:
            acc_sc[...] = pv
        else:
            acc_sc[...] = alpha * acc_sc[...] + pv
        m = m_new
    o_ref[0] = (acc_sc[...] / l).T.astype(o_ref.dtype)


def _attn_bounded_kernel(qT_ref, k_ref, vT_ref, o_ref, acc_sc, *, n_kv):
    tk = KV_TILE
    sub = tk // TOKEN_TILE
    qT = jnp.concatenate([qT_ref[0, 0, i] for i in range(qT_ref.shape[2])], axis=1)
    l = jnp.zeros((1, qT.shape[1]), F32)
    for j in range(n_kv):
        p = jnp.exp2(_dot(k_ref[0, 0, j * tk:(j + 1) * tk, :], qT))
        l = l + jnp.sum(p, axis=0, keepdims=True)
        vt = jnp.concatenate([vT_ref[0, 0, j * sub + i] for i in range(sub)], axis=1)
        pv = _dot(vt, p.astype(BF16))
        if j == 0:
            acc_sc[...] = pv
        else:
            acc_sc[...] += pv
    o_ref[0] = (acc_sc[...] / l).T.astype(o_ref.dtype)


def _attn_call(qT, k, vT, score_bound_log2):
    batch, heads, nt, _, tm = qT.shape
    seq = k.shape[2]
    tq = Q_TILE
    n_kv = seq // KV_TILE
    slabs = tq // tm

    def call(body, scratch, name):
        return pl.pallas_call(
            functools.partial(body, n_kv=n_kv),
            grid=(batch, heads, seq // tq),
            in_specs=[
                pl.BlockSpec((1, 1, slabs, A_DH, tm), lambda b, h, i: (b, h, i, 0, 0)),
                pl.BlockSpec((1, 1, seq, A_DH), lambda b, h, i: (b, h, 0, 0)),
                pl.BlockSpec((1, 1, nt, A_DV, tm), lambda b, h, i: (b, h, 0, 0, 0)),
            ],
            out_specs=pl.BlockSpec((1, tq, A_DV), lambda b, h, i: (b, i, h)),
            out_shape=jax.ShapeDtypeStruct((batch, seq, heads * A_DV), BF16),
            scratch_shapes=scratch,
            compiler_params=pltpu.CompilerParams(
                dimension_semantics=("arbitrary", "arbitrary", "arbitrary"),
                vmem_limit_bytes=VMEM_LIMIT),
            name=name,
        )(qT, k, vT)

    acc = pltpu.VMEM((A_DV, tq), F32)
    return lax.cond(
        score_bound_log2 <= SAFE_SCORE_LOG2,
        lambda: call(_attn_bounded_kernel, [acc], "attn_bounded"),
        lambda: call(_attn_kernel, [pltpu.VMEM((2, KV_TILE, tq), F32),
                                    pltpu.VMEM((2, 1, tq), F32), acc], "attn"))


def _load_as_bf16(w_hbm, w_sc, stage_sc, sem, sem_base):
    rows = stage_sc.shape[1]
    n_chunks = w_hbm.shape[0] // rows

    def chunk_copy(i):
        return pltpu.make_async_copy(w_hbm.at[pl.ds(i * rows, rows), :], stage_sc.at[i % 2],
                                     sem.at[sem_base + i % 2])

    chunk_copy(0).start()
    for i in range(n_chunks):
        if i + 1 < n_chunks:
            chunk_copy(i + 1).start()
        chunk_copy(i).wait()
        w_sc[i * rows:(i + 1) * rows, :] = stage_sc[i % 2].astype(BF16)


def _out_kernel(x_ref, hf_ref, hb_ref, o_ref, yb_ref, gml_ref, wout_ref, gffn_ref, w1_hbm, w2_hbm,
                y_ref, w1_ref, w2_ref, stage1_sc, stage2_sc, sem):
    @pl.when(pl.program_id(0) == 0)
    def _():
        _load_as_bf16(w1_hbm, w1_ref, stage1_sc, sem, 0)
        _load_as_bf16(w2_hbm, w2_ref, stage2_sc, sem, 2)

    x = x_ref[...]
    hsum = hf_ref[...] + hb_ref[...]
    gate = jax.nn.sigmoid(o_ref[...])
    gml = gml_ref[...]
    parts = []
    for h in range(M_HEADS):
        sl = slice(h * M_DV, (h + 1) * M_DV)
        parts.append((gate[:, sl] * _rms(hsum[:, sl], gml[:, sl])).astype(BF16))
    y_a = jnp.concatenate(parts, axis=1)
    n_a = y_a.shape[1]
    x1 = x + _dot(y_a, wout_ref[0:n_a, :]) + _dot(yb_ref[...], wout_ref[n_a:, :])
    hn = _rms(x1, gffn_ref[...]).astype(BF16)
    y_ref[...] = x1
    d_ff = w1_ref.shape[1]
    for j in range(d_ff // FF_CHUNK):
        sl = slice(j * FF_CHUNK, (j + 1) * FF_CHUNK)
        t = jnp.maximum(_dot(hn, w1_ref[:, sl]), 0.0)
        y_ref[...] += _dot((t * t).astype(BF16), w2_ref[sl, :])


def _out_call(xf, hf, hb, o, yb, gml, wout, gffn, w1, w2):
    n, d = xf.shape
    tm = TOKEN_TILE

    def full(a):
        return pl.BlockSpec(a.shape, lambda i: (0,) * a.ndim)

    tok = lambda w: pl.BlockSpec((tm, w), lambda i: (i, 0))
    hbm = pl.BlockSpec(memory_space=pl.ANY)
    in_specs = [tok(d), tok(512), tok(512), tok(512), tok(512), full(gml), full(wout), full(gffn),
                hbm, hbm]
    d_ff = w1.shape[1]
    stage_bytes = 2 * 1024 * 1024
    rows1 = stage_bytes // (4 * d_ff)
    rows2 = stage_bytes // (4 * d)
    assert d % rows1 == 0 and d_ff % rows2 == 0
    scratch = [pltpu.VMEM((d, d_ff), BF16), pltpu.VMEM((d_ff, d), BF16),
               pltpu.VMEM((2, rows1, d_ff), F32), pltpu.VMEM((2, rows2, d), F32),
               pltpu.SemaphoreType.DMA((4,))]
    return pl.pallas_call(
        _out_kernel, grid=(n // tm,), in_specs=in_specs, out_specs=tok(d),
        out_shape=jax.ShapeDtypeStruct((n, d), F32), scratch_shapes=scratch,
        compiler_params=pltpu.CompilerParams(dimension_semantics=("arbitrary",),
                                             vmem_limit_bytes=VMEM_LIMIT),
        name="out",
    )(xf, hf, hb, o, yb, gml, wout, gffn, w1, w2)


def _layer(xf, pos3, invf, batch, seq, g_mix_norm, w_in, conv_w, conv_b, b_gates, g_mlstm_out,
           g_cq, g_ckv, w_uq, w_ukv, g_q, g_k, w_out, g_ffn_norm, w_ff1, w_ff2):
    d = xf.shape[1]
    c0 = 2 * 256 + 2 * 512
    wmain = w_in[:, 0:c0].astype(BF16)
    wt = w_in[:, c0:].astype(BF16)
    gates_w = wt[:, 0:N_GATES]
    cq_w = wt[:, N_GATES:N_GATES + 256]
    ckv_w = wt[:, N_GATES + 256:N_GATES + 384]
    kpe_w = wt[:, N_GATES + 384:N_GATES + 448]
    wtail = jnp.concatenate([cq_w, ckv_w, kpe_w, kpe_w[:, 0:32], gates_w,
                             jnp.zeros((d, LANES - 96 - N_GATES), BF16)], axis=1)
    wuqT = w_uq.T.astype(BF16)
    ukv = w_ukv.reshape(w_ukv.shape[0], A_HEADS, A_NOPE + A_DV)
    wuk = ukv[:, :, :A_NOPE].reshape(w_ukv.shape[0], A_HEADS * A_NOPE).astype(BF16)
    wuvT = ukv[:, :, A_NOPE:].reshape(w_ukv.shape[0], A_HEADS * A_DV).T.astype(BF16)
    gqcol = g_q.reshape(A_DH, 1)
    gk = jnp.concatenate([g_k, g_k[A_NOPE + 32:], g_k[A_NOPE:A_NOPE + 32]]).reshape(1, -1)

    qk, v, o, sf, b3f, sb, b3b, qT, k, vT = _proj_call(
        xf, pos3, g_mix_norm.reshape(1, -1), wmain, wtail, g_cq.reshape(1, -1),
        g_ckv.reshape(1, -1), wuqT, wuk, wuvT, gqcol, gk, invf, b_gates.reshape(2, 8, 1), batch, seq)

    qkc, kT = _conv_call(qk, conv_w, conv_b.reshape(1, -1), batch, seq)
    esel = jnp.tile(jnp.repeat(jnp.eye(8, M_HEADS, dtype=BF16), LANES, axis=1), (4, 1))
    hf, hb = _mlstm_call(qkc.reshape(batch, seq, 512), kT, v.reshape(batch, seq, 512),
                         sf, b3f, sb, b3b, esel)
    score_bound_log2 = (A_DH ** 0.5 * LOG2E * 1.01) * jnp.max(jnp.abs(g_q)) * jnp.max(jnp.abs(g_k))
    yb = _attn_call(qT, k, vT, score_bound_log2)
    return _out_call(xf, hf.reshape(batch * seq, -1), hb.reshape(batch * seq, -1), o,
                     yb.reshape(batch * seq, -1), g_mlstm_out.reshape(1, -1), w_out.astype(BF16),
                     g_ffn_norm.reshape(1, -1), w_ff1, w_ff2)


def kernel(x, positions, g_mix_norm, w_in, conv_w, conv_b, b_gates, g_mlstm_out, g_cq, g_ckv,
           w_uq, w_ukv, g_q, g_k, w_out, g_ffn_norm, w_ff1, w_ff2):
    batch, seq, d = x.shape
    assert seq % TOKEN_TILE == 0 and seq % M_CHUNK == 0
    inv_freq = ROPE_THETA ** (-jnp.arange(0, A_ROPE, 2, dtype=F32) / A_ROPE)
    invf = inv_freq.reshape(A_ROPE // 2, 1)
    pos3 = positions.reshape(batch * seq // TOKEN_TILE, 1, TOKEN_TILE)
    xf = x.reshape(batch * seq, d)
    for l in range(w_in.shape[0]):
        xf = _layer(xf, pos3, invf, batch, seq, g_mix_norm[l], w_in[l], conv_w[l], conv_b[l],
                    b_gates[l], g_mlstm_out[l], g_cq[l], g_ckv[l], w_uq[l], w_ukv[l], g_q[l],
                    g_k[l], w_out[l], g_ffn_norm[l], w_ff1[l], w_ff2[l])
    return xf.reshape(batch, seq, d)
```

```python
import functools
import math

import jax
import jax.numpy as jnp
from jax import lax
from jax.experimental import pallas as pl
from jax.experimental.pallas import tpu as pltpu

F32 = jnp.float32
BF16 = jnp.bfloat16

EPS = 1e-6
ROPE_THETA = 10000.0
M_HEADS = 4
M_DQK = 64
M_DV = 128
M_CONV = 5
M_CHUNK = 128
A_HEADS = 4
A_NOPE = 128
A_ROPE = 64
A_DV = 128
A_DH = A_NOPE + A_ROPE
N_GATES = 4 * M_HEADS

LANES = 128
TOKEN_TILE = 1024
ROW_TILE = 512
KV_TILE = 1024
Q_TILE = 1024
PROJ_PARTS = 4
MLSTM_CHUNKS_PER_STEP = 8
SAFE_SCORE_LOG2 = 64.0
FF_CHUNK = 1024
VMEM_LIMIT = 56 * 1024 * 1024

LOG2E = math.log2(math.e)


def _dot(a, b):
    return jnp.dot(a, b, preferred_element_type=F32)


def _dot_nt(a, b):
    return lax.dot_general(a, b, (((1,), (1,)), ((), ())), preferred_element_type=F32)


def _dot_tn(a, b):
    return lax.dot_general(a, b, (((0,), (0,)), ((), ())), preferred_element_type=F32)


def _rms(x, g):
    ms = jnp.mean(x * x, axis=-1, keepdims=True)
    return x * lax.rsqrt(ms + EPS) * g


def _split3(y):
    hi = y.astype(BF16).astype(F32)
    mid = (y - hi).astype(BF16).astype(F32)
    lo = y - hi - mid
    return jnp.concatenate([hi, mid, lo], axis=0).astype(BF16)


def _chunk_decay(x, *, backward):
    L = M_CHUNK
    row = lax.broadcasted_iota(jnp.int32, (L, L), 0)
    col = lax.broadcasted_iota(jnp.int32, (L, L), 1)
    umat = jnp.where((row >= col) if backward else (row <= col), 1.0, 0.0).astype(BF16)
    lf = jax.nn.log_sigmoid(x)
    cum3 = _dot(_split3(lf), umat)
    cum = cum3[0:8] + cum3[8:16] + cum3[16:24]
    a = jnp.broadcast_to(jnp.sum(lf, axis=1, keepdims=True), (8, L))
    b = pltpu.roll(cum, 4, axis=0)
    a = pltpu.roll(a, 4, axis=0)
    w_log = a - b + x
    g = jnp.broadcast_to(jnp.max(w_log, axis=1, keepdims=True), (8, L))
    stats = jnp.concatenate([jnp.exp(w_log - g), (x - b) * LOG2E, a * LOG2E, g * LOG2E], axis=0)
    b3 = jnp.concatenate([_split3(b * LOG2E), jnp.zeros((8, L), BF16)], axis=0)
    return stats, b3


def _proj_kernel(x_ref, pos_ref, gmix_ref, wmain_ref, wtail_ref, gcq_ref, gckv_ref, wuqT_ref,
                 wuk_ref, wuvT_ref, gqcol_ref, gk_ref, invf_ref, bg_ref,
                 qk_ref, v_ref, o_ref, sf_ref, b3f_ref, sb_ref, b3b_ref, qT_ref, k_ref, vT_ref):
    tm = x_ref.shape[0]
    parts = [slice(i * tm // PROJ_PARTS, (i + 1) * tm // PROJ_PARTS) for i in range(PROJ_PARTS)]

    def project(rows):
        hn = _rms(x_ref[rows, :], gmix_ref[...]).astype(BF16)
        z = _dot(hn, wmain_ref[...])
        qk_ref[rows, :] = z[:, 0:512]
        v_ref[rows, :] = z[:, 512:1024].astype(BF16)
        o_ref[rows, :] = z[:, 1024:1536]
        return _dot(hn, wtail_ref[...])

    def latents(zt, rows):
        cq = zt[:, 0:256]
        ckv = zt[:, 256:384]
        tail = zt[:, 384:512]

        tailT = tail.T
        for d, (s_ref, b3_ref) in enumerate(((sf_ref, b3f_ref), (sb_ref, b3b_ref))):
            gx = tailT[96 + 8 * d:104 + 8 * d] + bg_ref[d]
            for j in range(gx.shape[1] // M_CHUNK):
                sl = slice(j * M_CHUNK, (j + 1) * M_CHUNK)
                dst = slice(rows.start + sl.start, rows.start + sl.stop)
                stats, b3 = _chunk_decay(gx[:, sl], backward=d == 1)
                s_ref[0, :, dst] = stats
                b3_ref[0, :, dst] = b3

        pos = pos_ref[0][:, rows].astype(F32)
        angT = invf_ref[...] * pos
        cosT = jnp.cos(angT)
        sinT = jnp.sin(angT)

        cqnT = _rms(cq, gcq_ref[...]).T.astype(BF16)
        qT = _dot(wuqT_ref[...], cqnT)
        gq = gqcol_ref[...] * (A_DH ** -0.5 * LOG2E)
        for h in range(A_HEADS):
            qh = qT[h * A_DH:(h + 1) * A_DH]
            ms = jnp.sum(qh * qh, axis=0, keepdims=True) * (1.0 / A_DH)
            qn = qh * lax.rsqrt(ms + EPS) * gq
            x1 = qn[A_NOPE:A_NOPE + 32]
            x2 = qn[A_NOPE + 32:A_DH]
            qT_ref[0, h, 0, 0:A_NOPE, rows] = qn[0:A_NOPE].astype(BF16)
            qT_ref[0, h, 0, A_NOPE:A_NOPE + 32, rows] = (x1 * cosT - x2 * sinT).astype(BF16)
            qT_ref[0, h, 0, A_NOPE + 32:A_DH, rows] = (x2 * cosT + x1 * sinT).astype(BF16)

        ckvn = _rms(ckv, gckv_ref[...])
        knope = _dot(ckvn.astype(BF16), wuk_ref[...])
        vT = _dot(wuvT_ref[...], ckvn.T.astype(BF16))
        for h in range(A_HEADS):
            vT_ref[0, h, 0, :, rows] = vT[h * A_DV:(h + 1) * A_DV].astype(BF16)

        tab = jnp.concatenate([cosT, cosT, -sinT, sinT], axis=0).T
        cc = tab[:, 0:A_ROPE]
        ss = tab[:, A_ROPE:2 * A_ROPE]
        gk = gk_ref[...]
        kpe = tail[:, 0:A_ROPE]
        kpe_sw = pltpu.roll(tail, LANES - 32, axis=1)[:, 0:A_ROPE]
        ss_pe = jnp.sum(kpe * kpe, axis=-1, keepdims=True)
        krope = (kpe * gk[:, A_NOPE:A_DH] * cc + kpe_sw * gk[:, A_DH:A_DH + A_ROPE] * ss)
        for h in range(A_HEADS):
            kn = knope[:, h * A_NOPE:(h + 1) * A_NOPE]
            ms = (jnp.sum(kn * kn, axis=-1, keepdims=True) + ss_pe) * (1.0 / A_DH)
            r = lax.rsqrt(ms + EPS)
            k_ref[0, h, rows, 0:A_NOPE] = (kn * r * gk[:, 0:A_NOPE]).astype(BF16)
            k_ref[0, h, rows, A_NOPE:A_DH] = (krope * r).astype(BF16)

    zt = project(parts[0])
    for i, rows in enumerate(parts):
        zt_next = project(parts[i + 1]) if i + 1 < len(parts) else None
        latents(zt, rows)
        zt = zt_next


def _proj_call(xf, pos3, gmix, wmain, wtail, gcq, gckv, wuqT, wuk, wuvT, gqcol, gk, invf, bg,
               batch, seq):
    n, d = xf.shape
    tm = TOKEN_TILE
    nt = seq // tm
    steps = n // tm

    def full(a):
        return pl.BlockSpec(a.shape, lambda i: (0,) * a.ndim)

    tok = lambda w: pl.BlockSpec((tm, w), lambda i: (i, 0))
    in_specs = [tok(d), pl.BlockSpec((1, 1, tm), lambda i: (i, 0, 0)), full(gmix), full(wmain),
                full(wtail), full(gcq), full(gckv), full(wuqT), full(wuk), full(wuvT), full(gqcol),
                full(gk), full(invf), full(bg)]
    stat_spec = pl.BlockSpec((1, 32, tm), lambda i: (i // nt, 0, i % nt))
    out_shape = [
        jax.ShapeDtypeStruct((n, 512), F32),
        jax.ShapeDtypeStruct((n, 512), BF16),
        jax.ShapeDtypeStruct((n, 512), F32),
        jax.ShapeDtypeStruct((batch, 32, seq), F32),
        jax.ShapeDtypeStruct((batch, 32, seq), BF16),
        jax.ShapeDtypeStruct((batch, 32, seq), F32),
        jax.ShapeDtypeStruct((batch, 32, seq), BF16),
        jax.ShapeDtypeStruct((batch, A_HEADS, nt, A_DH, tm), BF16),
        jax.ShapeDtypeStruct((batch, A_HEADS, seq, A_DH), BF16),
        jax.ShapeDtypeStruct((batch, A_HEADS, nt, A_DV, tm), BF16),
    ]
    out_specs = [
        tok(512), tok(512), tok(512), stat_spec, stat_spec, stat_spec, stat_spec,
        pl.BlockSpec((1, A_HEADS, 1, A_DH, tm), lambda i: (i // nt, 0, i % nt, 0, 0)),
        pl.BlockSpec((1, A_HEADS, tm, A_DH), lambda i: (i // nt, 0, i % nt, 0)),
        pl.BlockSpec((1, A_HEADS, 1, A_DV, tm), lambda i: (i // nt, 0, i % nt, 0, 0)),
    ]
    return pl.pallas_call(
        _proj_kernel, grid=(steps,), in_specs=in_specs, out_specs=out_specs, out_shape=out_shape,
        compiler_params=pltpu.CompilerParams(dimension_semantics=("arbitrary",),
                                             vmem_limit_bytes=VMEM_LIMIT),
        name="proj",
    )(xf, pos3, gmix, wmain, wtail, gcq, gckv, wuqT, wuk, wuvT, gqcol, gk, invf, bg)


def _conv_kernel(cur_ref, prev_ref, next_ref, w_ref, b_ref, qk_ref, kT_ref, *, tiles_per_seq):
    t = pl.program_id(0) % tiles_per_seq
    tm = cur_ref.shape[0]
    prev = jnp.where(t == 0, 0.0, prev_ref[...])
    nxt = jnp.where(t == tiles_per_seq - 1, 0.0, next_ref[...])
    ext = jnp.concatenate([prev, cur_ref[...], nxt], axis=0)
    cw = w_ref[...]
    u = b_ref[...]
    for j in range(M_CONV):
        off = 8 - M_CONV // 2 + j
        u = u + cw[j:j + 1, :] * ext[off:off + tm, :]
    qk = u * jax.nn.sigmoid(u)
    k = qk[:, 256:512]
    qk_ref[:, 0:256] = (qk[:, 0:256] * (M_DQK ** -0.5)).astype(BF16)
    qk_ref[:, 256:512] = k.astype(BF16)
    kT_ref[0] = k.T.astype(BF16)


def _conv_call(qk, convw, convb, batch, seq):
    n = qk.shape[0]
    tm = ROW_TILE
    nt = seq // tm
    r8 = tm // 8
    return pl.pallas_call(
        functools.partial(_conv_kernel, tiles_per_seq=nt),
        grid=(n // tm,),
        in_specs=[
            pl.BlockSpec((tm, 512), lambda i: (i, 0)),
            pl.BlockSpec((8, 512), lambda i: (jnp.maximum(i * r8 - 1, 0), 0)),
            pl.BlockSpec((8, 512), lambda i: (jnp.minimum((i + 1) * r8, n // 8 - 1), 0)),
            pl.BlockSpec(convw.shape, lambda i: (0, 0)),
            pl.BlockSpec(convb.shape, lambda i: (0, 0)),
        ],
        out_specs=[pl.BlockSpec((tm, 512), lambda i: (i, 0)),
                   pl.BlockSpec((1, 256, tm), lambda i: (i // nt, 0, i % nt))],
        out_shape=[jax.ShapeDtypeStruct((n, 512), BF16),
                   jax.ShapeDtypeStruct((batch, 256, seq), BF16)],
        compiler_params=pltpu.CompilerParams(dimension_semantics=("arbitrary",),
                                             vmem_limit_bytes=VMEM_LIMIT),
        name="conv",
    )(qk, qk, qk, convw, convb)


def _mlstm_kernel(qkf_ref, kTf_ref, vf_ref, sf_ref, b3f_ref, qkb_ref, kTb_ref, vb_ref, sb_ref,
                  b3b_ref, esel_ref, hf_ref, hb_ref, cf_sc, cb_sc, mf_sc, mb_sc):
    L = M_CHUNK
    n_sub = qkf_ref.shape[1] // L

    @pl.when(pl.program_id(1) == 0)
    def _():
        cf_sc[...] = jnp.zeros_like(cf_sc)
        cb_sc[...] = jnp.zeros_like(cb_sc)
        mf_sc[...] = jnp.zeros_like(mf_sc)
        mb_sc[...] = jnp.zeros_like(mb_sc)

    esel = esel_ref[...]
    row = lax.broadcasted_iota(jnp.int32, (L, L), 0)
    col = lax.broadcasted_iota(jnp.int32, (L, L), 1)
    ones_blk = jnp.ones((L, M_DV), BF16)
    fw = dict(qk=qkf_ref, kT=kTf_ref, v=vf_ref, s=sf_ref, b3=b3f_ref, out=hf_ref, c_sc=cf_sc,
              m_sc=mf_sc, visible=col <= row)
    bw = dict(qk=qkb_ref, kT=kTb_ref, v=vb_ref, s=sb_ref, b3=b3b_ref, out=hb_ref, c_sc=cb_sc,
              m_sc=mb_sc, visible=col >= row)

    work = []
    for d in (fw, bw):
        m = d["m_sc"][...]
        order = range(n_sub) if d is fw else range(n_sub - 1, -1, -1)
        for i in order:
            sl = slice(i * L, (i + 1) * L)
            stats = d["s"][0, :, sl]
            a, g = stats[16:24], stats[24:32]
            m_new = jnp.maximum(a + m, g)
            work.append((d, sl, dict(
                w=stats[0:8], r=stats[8:16], m_prev=m,
                dec=jnp.exp2(a + m - m_new), add=jnp.exp2(g - m_new),
                brep=_dot_tn(d["b3"][0, :, sl], esel))))
            m = m_new
        d["m_sc"][...] = m
    work = [work[j + k * n_sub] for j in range(n_sub) for k in range(2)]
    units = [(d, sl, gt, h) for d, sl, gt in work for h in range(M_HEADS)]

    grams, v_augs, s_augs = [], [], []
    for d, sl, gt, h in units:
        q_h = d["qk"][0, sl, h * M_DQK:(h + 1) * M_DQK]
        k_h = d["qk"][0, sl, 256 + h * M_DQK:256 + (h + 1) * M_DQK]
        grams.append(_dot_nt(q_h, k_h))
        v_augs.append(jnp.concatenate([d["v"][0, sl, h * M_DV:(h + 1) * M_DV], ones_blk], axis=1))
    for (d, sl, gt, h), v_aug in zip(units, v_augs):
        kTw = d["kT"][0, h * M_DQK:(h + 1) * M_DQK, sl].astype(F32) * gt["w"][h:h + 1, :]
        s_augs.append(_dot(kTw.astype(BF16), v_aug))

    states = {id(fw): [cf_sc[h] for h in range(M_HEADS)],
              id(bw): [cb_sc[h] for h in range(M_HEADS)]}
    for (d, sl, gt, h), gram, v_aug, s_aug in zip(units, grams, v_augs, s_augs):
        q_h = d["qk"][0, sl, h * M_DQK:(h + 1) * M_DQK]
        brep = gt["brep"][:, h * LANES:(h + 1) * LANES]
        bm = brep + gt["m_prev"][h:h + 1, :]
        dmat = jnp.where(d["visible"], brep + gt["r"][h:h + 1, :], -jnp.inf)
        m_t = jnp.maximum(bm, jnp.max(dmat, axis=1, keepdims=True))
        p = jnp.exp2(dmat - m_t) * gram
        inter = jnp.exp2(bm - m_t)
        c_prev = states[id(d)][h]
        lhs = jnp.concatenate(
            [p.astype(BF16), (inter[:, 0:M_DQK] * q_h.astype(F32)).astype(BF16)], axis=1)
        rhs = jnp.concatenate([v_aug, c_prev.astype(BF16)], axis=0)
        res = _dot(lhs, rhs)
        den = jnp.maximum(jnp.abs(res[:, M_DV:2 * M_DV]), jnp.exp2(-m_t))
        d["out"][0, sl, h * M_DV:(h + 1) * M_DV] = res[:, 0:M_DV] / den
        dec_h = jnp.concatenate([gt["dec"][h:h + 1, :]] * 2, axis=1)
        add_h = jnp.concatenate([gt["add"][h:h + 1, :]] * 2, axis=1)
        states[id(d)][h] = dec_h * c_prev + add_h * s_aug
    for h in range(M_HEADS):
        cf_sc[h] = states[id(fw)][h]
        cb_sc[h] = states[id(bw)][h]


def _mlstm_call(qk3, kT3, v3, sf, b3f, sb, b3b, esel):
    batch, seq, _ = qk3.shape
    L = M_CHUNK * MLSTM_CHUNKS_PER_STEP
    nc = seq // L
    fw_rows = lambda w: pl.BlockSpec((1, L, w), lambda b, c: (b, c, 0))
    bw_rows = lambda w: pl.BlockSpec((1, L, w), lambda b, c: (b, nc - 1 - c, 0))
    fw_lanes = lambda r: pl.BlockSpec((1, r, L), lambda b, c: (b, 0, c))
    bw_lanes = lambda r: pl.BlockSpec((1, r, L), lambda b, c: (b, 0, nc - 1 - c))
    state = pltpu.VMEM((M_HEADS, M_DQK, 2 * M_DV), F32)
    return pl.pallas_call(
        _mlstm_kernel,
        grid=(batch, nc),
        in_specs=[fw_rows(512), fw_lanes(256), fw_rows(512), fw_lanes(32), fw_lanes(32),
                  bw_rows(512), bw_lanes(256), bw_rows(512), bw_lanes(32), bw_lanes(32),
                  pl.BlockSpec(esel.shape, lambda b, c: (0, 0))],
        out_specs=[fw_rows(512), bw_rows(512)],
        out_shape=[jax.ShapeDtypeStruct((batch, seq, 512), F32)] * 2,
        scratch_shapes=[state, state, pltpu.VMEM((8, M_CHUNK), F32), pltpu.VMEM((8, M_CHUNK), F32)],
        compiler_params=pltpu.CompilerParams(dimension_semantics=("arbitrary", "arbitrary"),
                                             vmem_limit_bytes=VMEM_LIMIT),
        name="mlstm",
    )(qk3, kT3, v3, sf, b3f, qk3, kT3, v3, sb, b3b, esel)


def _attn_kernel(qT_ref, k_ref, vT_ref, o_ref, s_sc, cm_sc, acc_sc, *, n_kv):
    tk = KV_TILE
    sub = tk // TOKEN_TILE
    qT = jnp.concatenate([qT_ref[0, 0, i] for i in range(qT_ref.shape[2])], axis=1)
    tq = qT.shape[1]

    def scores(j, slot):
        sT = _dot(k_ref[0, 0, j * tk:(j + 1) * tk, :], qT)
        s_sc[slot] = sT
        cm_sc[slot] = jnp.max(sT, axis=0, keepdims=True)

    m = jnp.full((1, tq), -jnp.inf, F32)
    l = jnp.zeros((1, tq), F32)
    scores(0, 0)
    for j in range(n_kv):
        slot = j % 2
        if j + 1 < n_kv:
            scores(j + 1, 1 - slot)
        m_new = jnp.maximum(m, cm_sc[slot])
        alpha = jnp.exp2(m - m_new)
        p = jnp.exp2(s_sc[slot] - m_new)
        l = alpha * l + jnp.sum(p, axis=0, keepdims=True)
        vt = jnp.concatenate([vT_ref[0, 0, j * sub + i] for i in range(sub)], axis=1)
        pv = _dot(vt, p.astype(BF16))
        if j == 0:
            acc_sc[...] = pv
        else:
            acc_sc[...] = alpha * acc_sc[...] + pv
        m = m_new
    o_ref[0] = (acc_sc[...] / l).T.astype(o_ref.dtype)


def _attn_bounded_kernel(qT_ref, k_ref, vT_ref, o_ref, acc_sc, *, n_kv):
    tk = KV_TILE
    sub = tk // TOKEN_TILE
    qT = jnp.concatenate([qT_ref[0, 0, i] for i in range(qT_ref.shape[2])], axis=1)
    l = jnp.zeros((1, qT.shape[1]), F32)
    for j in range(n_kv):
        p = jnp.exp2(_dot(k_ref[0, 0, j * tk:(j + 1) * tk, :], qT))
        l = l + jnp.sum(p, axis=0, keepdims=True)
        vt = jnp.concatenate([vT_ref[0, 0, j * sub + i] for i in range(sub)], axis=1)
        pv = _dot(vt, p.astype(BF16))
        if j == 0:
            acc_sc[...] = pv
        else:
            acc_sc[...] += pv
    o_ref[0] = (acc_sc[...] / l).T.astype(o_ref.dtype)


def _attn_call(qT, k, vT, score_bound_log2):
    batch, heads, nt, _, tm = qT.shape
    seq = k.shape[2]
    tq = Q_TILE
    n_kv = seq // KV_TILE
    slabs = tq // tm

    def call(body, scratch, name):
        return pl.pallas_call(
            functools.partial(body, n_kv=n_kv),
            grid=(batch, heads, seq // tq),
            in_specs=[
                pl.BlockSpec((1, 1, slabs, A_DH, tm), lambda b, h, i: (b, h, i, 0, 0)),
                pl.BlockSpec((1, 1, seq, A_DH), lambda b, h, i: (b, h, 0, 0)),
                pl.BlockSpec((1, 1, nt, A_DV, tm), lambda b, h, i: (b, h, 0, 0, 0)),
            ],
            out_specs=pl.BlockSpec((1, tq, A_DV), lambda b, h, i: (b, i, h)),
            out_shape=jax.ShapeDtypeStruct((batch, seq, heads * A_DV), BF16),
            scratch_shapes=scratch,
            compiler_params=pltpu.CompilerParams(
                dimension_semantics=("arbitrary", "arbitrary", "arbitrary"),
                vmem_limit_bytes=VMEM_LIMIT),
            name=name,
        )(qT, k, vT)

    acc = pltpu.VMEM((A_DV, tq), F32)
    return lax.cond(
        score_bound_log2 <= SAFE_SCORE_LOG2,
        lambda: call(_attn_bounded_kernel, [acc], "attn_bounded"),
        lambda: call(_attn_kernel, [pltpu.VMEM((2, KV_TILE, tq), F32),
                                    pltpu.VMEM((2, 1, tq), F32), acc], "attn"))


def _out_kernel(x_ref, hf_ref, hb_ref, o_ref, yb_ref, gml_ref, wout_ref, gffn_ref, w1_ref, w2_ref,
                y_ref):
    x = x_ref[...]
    hsum = hf_ref[...] + hb_ref[...]
    gate = jax.nn.sigmoid(o_ref[...])
    gml = gml_ref[...]
    parts = []
    for h in range(M_HEADS):
        sl = slice(h * M_DV, (h + 1) * M_DV)
        parts.append((gate[:, sl] * _rms(hsum[:, sl], gml[:, sl])).astype(BF16))
    y_a = jnp.concatenate(parts, axis=1)
    n_a = y_a.shape[1]
    x1 = x + _dot(y_a, wout_ref[0:n_a, :]) + _dot(yb_ref[...], wout_ref[n_a:, :])
    hn = _rms(x1, gffn_ref[...]).astype(BF16)
    y_ref[...] = x1
    d_ff = w1_ref.shape[1]
    for j in range(d_ff // FF_CHUNK):
        sl = slice(j * FF_CHUNK, (j + 1) * FF_CHUNK)
        t = jnp.maximum(_dot(hn, w1_ref[:, sl]), 0.0)
        y_ref[...] += _dot((t * t).astype(BF16), w2_ref[sl, :])


def _out_call(xf, hf, hb, o, yb, gml, wout, gffn, w1, w2):
    n, d = xf.shape
    tm = ROW_TILE

    def full(a):
        return pl.BlockSpec(a.shape, lambda i: (0,) * a.ndim)

    tok = lambda w: pl.BlockSpec((tm, w), lambda i: (i, 0))
    in_specs = [tok(d), tok(512), tok(512), tok(512), tok(512), full(gml), full(wout), full(gffn),
                full(w1), full(w2)]
    return pl.pallas_call(
        _out_kernel, grid=(n // tm,), in_specs=in_specs, out_specs=tok(d),
        out_shape=jax.ShapeDtypeStruct((n, d), F32),
        compiler_params=pltpu.CompilerParams(dimension_semantics=("arbitrary",),
                                             vmem_limit_bytes=VMEM_LIMIT),
        name="out",
    )(xf, hf, hb, o, yb, gml, wout, gffn, w1, w2)


def _layer(xf, pos3, invf, batch, seq, g_mix_norm, w_in, conv_w, conv_b, b_gates, g_mlstm_out,
           g_cq, g_ckv, w_uq, w_ukv, g_q, g_k, w_out, g_ffn_norm, w_ff1, w_ff2):
    d = xf.shape[1]
    c0 = 2 * 256 + 2 * 512
    wmain = w_in[:, 0:c0].astype(BF16)
    wt = w_in[:, c0:].astype(BF16)
    gates_w = wt[:, 0:N_GATES]
    cq_w = wt[:, N_GATES:N_GATES + 256]
    ckv_w = wt[:, N_GATES + 256:N_GATES + 384]
    kpe_w = wt[:, N_GATES + 384:N_GATES + 448]
    wtail = jnp.concatenate([cq_w, ckv_w, kpe_w, kpe_w[:, 0:32], gates_w,
                             jnp.zeros((d, LANES - 96 - N_GATES), BF16)], axis=1)
    wuqT = w_uq.T.astype(BF16)
    ukv = w_ukv.reshape(w_ukv.shape[0], A_HEADS, A_NOPE + A_DV)
    wuk = ukv[:, :, :A_NOPE].reshape(w_ukv.shape[0], A_HEADS * A_NOPE).astype(BF16)
    wuvT = ukv[:, :, A_NOPE:].reshape(w_ukv.shape[0], A_HEADS * A_DV).T.astype(BF16)
    gqcol = g_q.reshape(A_DH, 1)
    gk = jnp.concatenate([g_k, g_k[A_NOPE + 32:], g_k[A_NOPE:A_NOPE + 32]]).reshape(1, -1)

    qk, v, o, sf, b3f, sb, b3b, qT, k, vT = _proj_call(
        xf, pos3, g_mix_norm.reshape(1, -1), wmain, wtail, g_cq.reshape(1, -1),
        g_ckv.reshape(1, -1), wuqT, wuk, wuvT, gqcol, gk, invf, b_gates.reshape(2, 8, 1), batch, seq)

    qkc, kT = _conv_call(qk, conv_w, conv_b.reshape(1, -1), batch, seq)
    esel = jnp.tile(jnp.repeat(jnp.eye(8, M_HEADS, dtype=BF16), LANES, axis=1), (4, 1))
    hf, hb = _mlstm_call(qkc.reshape(batch, seq, 512), kT, v.reshape(batch, seq, 512),
                         sf, b3f, sb, b3b, esel)
    score_bound_log2 = (A_DH ** 0.5 * LOG2E * 1.01) * jnp.max(jnp.abs(g_q)) * jnp.max(jnp.abs(g_k))
    yb = _attn_call(qT, k, vT, score_bound_log2)
    return _out_call(xf, hf.reshape(batch * seq, -1), hb.reshape(batch * seq, -1), o,
                     yb.reshape(batch * seq, -1), g_mlstm_out.reshape(1, -1), w_out.astype(BF16),
                     g_ffn_norm.reshape(1, -1), w_ff1.astype(BF16), w_ff2.astype(BF16))


def kernel(x, positions, g_mix_norm, w_in, conv_w, conv_b, b_gates, g_mlstm_out, g_cq, g_ckv,
           w_uq, w_ukv, g_q, g_k, w_out, g_ffn_norm, w_ff1, w_ff2):
    batch, seq, d = x.shape
    assert seq % TOKEN_TILE == 0 and seq % M_CHUNK == 0
    inv_freq = ROPE_THETA ** (-jnp.arange(0, A_ROPE, 2, dtype=F32) / A_ROPE)
    invf = inv_freq.reshape(A_ROPE // 2, 1)
    pos3 = positions.reshape(batch * seq // TOKEN_TILE, 1, TOKEN_TILE)
    xf = x.reshape(batch * seq, d)
    for l in range(w_in.shape[0]):
        xf = _layer(xf, pos3, invf, batch, seq, g_mix_norm[l], w_in[l], conv_w[l], conv_b[l],
                    b_gates[l], g_mlstm_out[l], g_cq[l], g_ckv[l], w_uq[l], w_ukv[l], g_q[l],
                    g_k[l], w_out[l], g_ffn_norm[l], w_ff1[l], w_ff2[l])
    return xf.reshape(batch, seq, d)
```

```python
import functools
import math

import jax
import jax.numpy as jnp
from jax import lax
from jax.experimental import pallas as pl
from jax.experimental.pallas import tpu as pltpu

F32 = jnp.float32
BF16 = jnp.bfloat16

EPS = 1e-6
ROPE_THETA = 10000.0
M_HEADS = 4
M_DQK = 64
M_DV = 128
M_CONV = 5
M_CHUNK = 128
A_HEADS = 4
A_NOPE = 128
A_ROPE = 64
A_DV = 128
A_DH = A_NOPE + A_ROPE
N_GATES = 4 * M_HEADS

LANES = 128
TOKEN_TILE = 1024
ROW_TILE = 512
CONV_ROWS = 128
KV_TILE = 1024
Q_TILE = 1024
PROJ_PARTS = 4
MLSTM_CHUNKS_PER_STEP = 8
SAFE_SCORE_LOG2 = 64.0
FF_CHUNK = 1024
VMEM_LIMIT = 56 * 1024 * 1024

LOG2E = math.log2(math.e)


def _dot(a, b):
    return jnp.dot(a, b, preferred_element_type=F32)


def _dot_nt(a, b):
    return lax.dot_general(a, b, (((1,), (1,)), ((), ())), preferred_element_type=F32)


def _dot_tn(a, b):
    return lax.dot_general(a, b, (((0,), (0,)), ((), ())), preferred_element_type=F32)


def _rms(x, g):
    ms = jnp.mean(x * x, axis=-1, keepdims=True)
    return x * lax.rsqrt(ms + EPS) * g


def _split3(y):
    hi = y.astype(BF16).astype(F32)
    mid = (y - hi).astype(BF16).astype(F32)
    lo = y - hi - mid
    return jnp.concatenate([hi, mid, lo], axis=0).astype(BF16)


def _chunk_decay(x, *, backward):
    L = M_CHUNK
    row = lax.broadcasted_iota(jnp.int32, (L, L), 0)
    col = lax.broadcasted_iota(jnp.int32, (L, L), 1)
    umat = jnp.where((row >= col) if backward else (row <= col), 1.0, 0.0).astype(BF16)
    lf = jax.nn.log_sigmoid(x)
    cum3 = _dot(_split3(lf), umat)
    cum = cum3[0:8] + cum3[8:16] + cum3[16:24]
    a = jnp.broadcast_to(jnp.sum(lf, axis=1, keepdims=True), (8, L))
    b = pltpu.roll(cum, 4, axis=0)
    a = pltpu.roll(a, 4, axis=0)
    w_log = a - b + x
    g = jnp.broadcast_to(jnp.max(w_log, axis=1, keepdims=True), (8, L))
    stats = jnp.concatenate([jnp.exp(w_log - g), (x - b) * LOG2E, a * LOG2E, g * LOG2E], axis=0)
    b3 = jnp.concatenate([_split3(b * LOG2E), jnp.zeros((8, L), BF16)], axis=0)
    return stats, b3


def _proj_kernel(x_ref, pos_ref, gmix_ref, wmain_ref, wtail_ref, gcq_ref, gckv_ref, wuqT_ref,
                 wuk_ref, wuvT_ref, gqcol_ref, gk_ref, invf_ref, bg_ref,
                 qk_ref, v_ref, o_ref, sf_ref, b3f_ref, sb_ref, b3b_ref, qT_ref, k_ref, vT_ref):
    tm = x_ref.shape[0]
    parts = [slice(i * tm // PROJ_PARTS, (i + 1) * tm // PROJ_PARTS) for i in range(PROJ_PARTS)]

    def project(rows):
        hn = _rms(x_ref[rows, :], gmix_ref[...]).astype(BF16)
        z = _dot(hn, wmain_ref[...])
        qk_ref[rows, :] = z[:, 0:512]
        v_ref[rows, :] = z[:, 512:1024].astype(BF16)
        o_ref[rows, :] = z[:, 1024:1536]
        return _dot(hn, wtail_ref[...])

    def latents(zt, rows):
        cq = zt[:, 0:256]
        ckv = zt[:, 256:384]
        tail = zt[:, 384:512]

        tailT = tail.T
        for d, (s_ref, b3_ref) in enumerate(((sf_ref, b3f_ref), (sb_ref, b3b_ref))):
            gx = tailT[96 + 8 * d:104 + 8 * d] + bg_ref[d]
            for j in range(gx.shape[1] // M_CHUNK):
                sl = slice(j * M_CHUNK, (j + 1) * M_CHUNK)
                dst = slice(rows.start + sl.start, rows.start + sl.stop)
                stats, b3 = _chunk_decay(gx[:, sl], backward=d == 1)
                s_ref[0, :, dst] = stats
                b3_ref[0, :, dst] = b3

        pos = pos_ref[0][:, rows].astype(F32)
        angT = invf_ref[...] * pos
        cosT = jnp.cos(angT)
        sinT = jnp.sin(angT)

        cqnT = _rms(cq, gcq_ref[...]).T.astype(BF16)
        qT = _dot(wuqT_ref[...], cqnT)
        gq = gqcol_ref[...] * (A_DH ** -0.5 * LOG2E)
        for h in range(A_HEADS):
            qh = qT[h * A_DH:(h + 1) * A_DH]
            ms = jnp.sum(qh * qh, axis=0, keepdims=True) * (1.0 / A_DH)
            qn = qh * lax.rsqrt(ms + EPS) * gq
            x1 = qn[A_NOPE:A_NOPE + 32]
            x2 = qn[A_NOPE + 32:A_DH]
            qT_ref[0, h, 0, 0:A_NOPE, rows] = qn[0:A_NOPE].astype(BF16)
            qT_ref[0, h, 0, A_NOPE:A_NOPE + 32, rows] = (x1 * cosT - x2 * sinT).astype(BF16)
            qT_ref[0, h, 0, A_NOPE + 32:A_DH, rows] = (x2 * cosT + x1 * sinT).astype(BF16)

        ckvn = _rms(ckv, gckv_ref[...])
        knope = _dot(ckvn.astype(BF16), wuk_ref[...])
        vT = _dot(wuvT_ref[...], ckvn.T.astype(BF16))
        for h in range(A_HEADS):
            vT_ref[0, h, 0, :, rows] = vT[h * A_DV:(h + 1) * A_DV].astype(BF16)

        tab = jnp.concatenate([cosT, cosT, -sinT, sinT], axis=0).T
        cc = tab[:, 0:A_ROPE]
        ss = tab[:, A_ROPE:2 * A_ROPE]
        gk = gk_ref[...]
        kpe = tail[:, 0:A_ROPE]
        kpe_sw = pltpu.roll(tail, LANES - 32, axis=1)[:, 0:A_ROPE]
        ss_pe = jnp.sum(kpe * kpe, axis=-1, keepdims=True)
        krope = (kpe * gk[:, A_NOPE:A_DH] * cc + kpe_sw * gk[:, A_DH:A_DH + A_ROPE] * ss)
        for h in range(A_HEADS):
            kn = knope[:, h * A_NOPE:(h + 1) * A_NOPE]
            ms = (jnp.sum(kn * kn, axis=-1, keepdims=True) + ss_pe) * (1.0 / A_DH)
            r = lax.rsqrt(ms + EPS)
            k_ref[0, h, rows, 0:A_NOPE] = (kn * r * gk[:, 0:A_NOPE]).astype(BF16)
            k_ref[0, h, rows, A_NOPE:A_DH] = (krope * r).astype(BF16)

    zt = project(parts[0])
    for i, rows in enumerate(parts):
        zt_next = project(parts[i + 1]) if i + 1 < len(parts) else None
        latents(zt, rows)
        zt = zt_next


def _proj_call(xf, pos3, gmix, wmain, wtail, gcq, gckv, wuqT, wuk, wuvT, gqcol, gk, invf, bg,
               batch, seq):
    n, d = xf.shape
    tm = TOKEN_TILE
    nt = seq // tm
    steps = n // tm

    def full(a):
        return pl.BlockSpec(a.shape, lambda i: (0,) * a.ndim)

    tok = lambda w: pl.BlockSpec((tm, w), lambda i: (i, 0))
    in_specs = [tok(d), pl.BlockSpec((1, 1, tm), lambda i: (i, 0, 0)), full(gmix), full(wmain),
                full(wtail), full(gcq), full(gckv), full(wuqT), full(wuk), full(wuvT), full(gqcol),
                full(gk), full(invf), full(bg)]
    stat_spec = pl.BlockSpec((1, 32, tm), lambda i: (i // nt, 0, i % nt))
    out_shape = [
        jax.ShapeDtypeStruct((n, 512), F32),
        jax.ShapeDtypeStruct((n, 512), BF16),
        jax.ShapeDtypeStruct((n, 512), F32),
        jax.ShapeDtypeStruct((batch, 32, seq), F32),
        jax.ShapeDtypeStruct((batch, 32, seq), BF16),
        jax.ShapeDtypeStruct((batch, 32, seq), F32),
        jax.ShapeDtypeStruct((batch, 32, seq), BF16),
        jax.ShapeDtypeStruct((batch, A_HEADS, nt, A_DH, tm), BF16),
        jax.ShapeDtypeStruct((batch, A_HEADS, seq, A_DH), BF16),
        jax.ShapeDtypeStruct((batch, A_HEADS, nt, A_DV, tm), BF16),
    ]
    out_specs = [
        tok(512), tok(512), tok(512), stat_spec, stat_spec, stat_spec, stat_spec,
        pl.BlockSpec((1, A_HEADS, 1, A_DH, tm), lambda i: (i // nt, 0, i % nt, 0, 0)),
        pl.BlockSpec((1, A_HEADS, tm, A_DH), lambda i: (i // nt, 0, i % nt, 0)),
        pl.BlockSpec((1, A_HEADS, 1, A_DV, tm), lambda i: (i // nt, 0, i % nt, 0, 0)),
    ]
    return pl.pallas_call(
        _proj_kernel, grid=(steps,), in_specs=in_specs, out_specs=out_specs, out_shape=out_shape,
        compiler_params=pltpu.CompilerParams(dimension_semantics=("arbitrary",),
                                             vmem_limit_bytes=VMEM_LIMIT),
        name="proj",
    )(xf, pos3, gmix, wmain, wtail, gcq, gckv, wuqT, wuk, wuvT, gqcol, gk, invf, bg)


def _conv_kernel(cur_ref, prev_ref, next_ref, w_ref, b_ref, qk_ref, kT_ref, *, tiles_per_seq):
    t = pl.program_id(0) % tiles_per_seq
    tm = cur_ref.shape[0]
    prev = jnp.where(t == 0, 0.0, prev_ref[...])
    nxt = jnp.where(t == tiles_per_seq - 1, 0.0, next_ref[...])
    cw = w_ref[...]
    rb = CONV_ROWS
    for blk in range(tm // rb):
        lo = blk * rb
        top = prev if blk == 0 else cur_ref[lo - 8:lo, :]
        bot = nxt if lo + rb == tm else cur_ref[lo + rb:lo + rb + 8, :]
        ext = jnp.concatenate([top, cur_ref[lo:lo + rb, :], bot], axis=0)
        u = b_ref[...]
        for j in range(M_CONV):
            shifted = (ext if j == M_CONV // 2
                       else pltpu.roll(ext, (M_CONV // 2 - j) % (rb + 16), axis=0))
            u = u + cw[j:j + 1, :] * shifted[8:8 + rb, :]
        qk = u * jax.nn.sigmoid(u)
        k = qk[:, 256:512]
        qk_ref[lo:lo + rb, 0:256] = (qk[:, 0:256] * (M_DQK ** -0.5)).astype(BF16)
        qk_ref[lo:lo + rb, 256:512] = k.astype(BF16)
        kT_ref[0, :, lo:lo + rb] = k.T.astype(BF16)


def _conv_call(qk, convw, convb, batch, seq):
    n = qk.shape[0]
    tm = ROW_TILE
    nt = seq // tm
    r8 = tm // 8
    return pl.pallas_call(
        functools.partial(_conv_kernel, tiles_per_seq=nt),
        grid=(n // tm,),
        in_specs=[
            pl.BlockSpec((tm, 512), lambda i: (i, 0)),
            pl.BlockSpec((8, 512), lambda i: (jnp.maximum(i * r8 - 1, 0), 0)),
            pl.BlockSpec((8, 512), lambda i: (jnp.minimum((i + 1) * r8, n // 8 - 1), 0)),
            pl.BlockSpec(convw.shape, lambda i: (0, 0)),
            pl.BlockSpec(convb.shape, lambda i: (0, 0)),
        ],
        out_specs=[pl.BlockSpec((tm, 512), lambda i: (i, 0)),
                   pl.BlockSpec((1, 256, tm), lambda i: (i // nt, 0, i % nt))],
        out_shape=[jax.ShapeDtypeStruct((n, 512), BF16),
                   jax.ShapeDtypeStruct((batch, 256, seq), BF16)],
        compiler_params=pltpu.CompilerParams(dimension_semantics=("arbitrary",),
                                             vmem_limit_bytes=VMEM_LIMIT),
        name="conv",
    )(qk, qk, qk, convw, convb)


def _mlstm_kernel(qkf_ref, kTf_ref, vf_ref, sf_ref, b3f_ref, qkb_ref, kTb_ref, vb_ref, sb_ref,
                  b3b_ref, esel_ref, hf_ref, hb_ref, cf_sc, cb_sc, mf_sc, mb_sc):
    L = M_CHUNK
    n_sub = qkf_ref.shape[1] // L

    @pl.when(pl.program_id(1) == 0)
    def _():
        cf_sc[...] = jnp.zeros_like(cf_sc)
        cb_sc[...] = jnp.zeros_like(cb_sc)
        mf_sc[...] = jnp.zeros_like(mf_sc)
        mb_sc[...] = jnp.zeros_like(mb_sc)

    esel = esel_ref[...]
    row = lax.broadcasted_iota(jnp.int32, (L, L), 0)
    col = lax.broadcasted_iota(jnp.int32, (L, L), 1)
    ones_blk = jnp.ones((L, M_DV), BF16)
    fw = dict(qk=qkf_ref, kT=kTf_ref, v=vf_ref, s=sf_ref, b3=b3f_ref, out=hf_ref, c_sc=cf_sc,
              m_sc=mf_sc, visible=col <= row)
    bw = dict(qk=qkb_ref, kT=kTb_ref, v=vb_ref, s=sb_ref, b3=b3b_ref, out=hb_ref, c_sc=cb_sc,
              m_sc=mb_sc, visible=col >= row)

    work = []
    for d in (fw, bw):
        m = d["m_sc"][...]
        order = range(n_sub) if d is fw else range(n_sub - 1, -1, -1)
        for i in order:
            sl = slice(i * L, (i + 1) * L)
            stats = d["s"][0, :, sl]
            a, g = stats[16:24], stats[24:32]
            m_new = jnp.maximum(a + m, g)
            work.append((d, sl, dict(
                w=stats[0:8], r=stats[8:16], m_prev=m,
                dec=jnp.exp2(a + m - m_new), add=jnp.exp2(g - m_new),
                brep=_dot_tn(d["b3"][0, :, sl], esel))))
            m = m_new
        d["m_sc"][...] = m
    work = [work[j + k * n_sub] for j in range(n_sub) for k in range(2)]
    units = [(d, sl, gt, h) for d, sl, gt in work for h in range(M_HEADS)]

    grams, v_augs, s_augs = [], [], []
    for d, sl, gt, h in units:
        q_h = d["qk"][0, sl, h * M_DQK:(h + 1) * M_DQK]
        k_h = d["qk"][0, sl, 256 + h * M_DQK:256 + (h + 1) * M_DQK]
        grams.append(_dot_nt(q_h, k_h))
        v_augs.append(jnp.concatenate([d["v"][0, sl, h * M_DV:(h + 1) * M_DV], ones_blk], axis=1))
    for (d, sl, gt, h), v_aug in zip(units, v_augs):
        kTw = d["kT"][0, h * M_DQK:(h + 1) * M_DQK, sl].astype(F32) * gt["w"][h:h + 1, :]
        s_augs.append(_dot(kTw.astype(BF16), v_aug))

    states = {id(fw): [cf_sc[h] for h in range(M_HEADS)],
              id(bw): [cb_sc[h] for h in range(M_HEADS)]}
    for (d, sl, gt, h), gram, v_aug, s_aug in zip(units, grams, v_augs, s_augs):
        q_h = d["qk"][0, sl, h * M_DQK:(h + 1) * M_DQK]
        brep = gt["brep"][:, h * LANES:(h + 1) * LANES]
        bm = brep + gt["m_prev"][h:h + 1, :]
        dmat = jnp.where(d["visible"], brep + gt["r"][h:h + 1, :], -jnp.inf)
        m_t = jnp.maximum(bm, jnp.max(dmat, axis=1, keepdims=True))
        p = jnp.exp2(dmat - m_t) * gram
        inter = jnp.exp2(bm - m_t)
        c_prev = states[id(d)][h]
        lhs = jnp.concatenate(
            [p.astype(BF16), (inter[:, 0:M_DQK] * q_h.astype(F32)).astype(BF16)], axis=1)
        rhs = jnp.concatenate([v_aug, c_prev.astype(BF16)], axis=0)
        res = _dot(lhs, rhs)
        den = jnp.maximum(jnp.abs(res[:, M_DV:2 * M_DV]), jnp.exp2(-m_t))
        d["out"][0, sl, h * M_DV:(h + 1) * M_DV] = res[:, 0:M_DV] / den
        dec_h = jnp.concatenate([gt["dec"][h:h + 1, :]] * 2, axis=1)
        add_h = jnp.concatenate([gt["add"][h:h + 1, :]] * 2, axis=1)
        states[id(d)][h] = dec_h * c_prev + add_h * s_aug
    for h in range(M_HEADS):
        cf_sc[h] = states[id(fw)][h]
        cb_sc[h] = states[id(bw)][h]


def _mlstm_call(qk3, kT3, v3, sf, b3f, sb, b3b, esel):
    batch, seq, _ = qk3.shape
    L = M_CHUNK * MLSTM_CHUNKS_PER_STEP
    nc = seq // L
    fw_rows = lambda w: pl.BlockSpec((1, L, w), lambda b, c: (b, c, 0))
    bw_rows = lambda w: pl.BlockSpec((1, L, w), lambda b, c: (b, nc - 1 - c, 0))
    fw_lanes = lambda r: pl.BlockSpec((1, r, L), lambda b, c: (b, 0, c))
    bw_lanes = lambda r: pl.BlockSpec((1, r, L), lambda b, c: (b, 0, nc - 1 - c))
    state = pltpu.VMEM((M_HEADS, M_DQK, 2 * M_DV), F32)
    return pl.pallas_call(
        _mlstm_kernel,
        grid=(batch, nc),
        in_specs=[fw_rows(512), fw_lanes(256), fw_rows(512), fw_lanes(32), fw_lanes(32),
                  bw_rows(512), bw_lanes(256), bw_rows(512), bw_lanes(32), bw_lanes(32),
                  pl.BlockSpec(esel.shape, lambda b, c: (0, 0))],
        out_specs=[fw_rows(512), bw_rows(512)],
        out_shape=[jax.ShapeDtypeStruct((batch, seq, 512), F32)] * 2,
        scratch_shapes=[state, state, pltpu.VMEM((8, M_CHUNK), F32), pltpu.VMEM((8, M_CHUNK), F32)],
        compiler_params=pltpu.CompilerParams(dimension_semantics=("arbitrary", "arbitrary"),
                                             vmem_limit_bytes=VMEM_LIMIT),
        name="mlstm",
    )(qk3, kT3, v3, sf, b3f, qk3, kT3, v3, sb, b3b, esel)


def _attn_kernel(qT_ref, k_ref, vT_ref, o_ref, s_sc, cm_sc, acc_sc, *, n_kv):
    tk = KV_TILE
    sub = tk // TOKEN_TILE
    qT = jnp.concatenate([qT_ref[0, 0, i] for i in range(qT_ref.shape[2])], axis=1)
    tq = qT.shape[1]

    def scores(j, slot):
        sT = _dot(k_ref[0, 0, j * tk:(j + 1) * tk, :], qT)
        s_sc[slot] = sT
        cm_sc[slot] = jnp.max(sT, axis=0, keepdims=True)

    m = jnp.full((1, tq), -jnp.inf, F32)
    l = jnp.zeros((1, tq), F32)
    scores(0, 0)
    for j in range(n_kv):
        slot = j % 2
        if j + 1 < n_kv:
            scores(j + 1, 1 - slot)
        m_new = jnp.maximum(m, cm_sc[slot])
        alpha = jnp.exp2(m - m_new)
        p = jnp.exp2(s_sc[slot] - m_new)
        l = alpha * l + jnp.sum(p, axis=0, keepdims=True)
        vt = jnp.concatenate([vT_ref[0, 0, j * sub + i] for i in range(sub)], axis=1)
        pv = _dot(vt, p.astype(BF16))
        if j == 0:
            acc_sc[...] = pv
        else:
            acc_sc[...] = alpha * acc_sc[...] + pv
        m = m_new
    o_ref[0] = (acc_sc[...] / l).T.astype(o_ref.dtype)


def _attn_bounded_kernel(qT_ref, k_ref, vT_ref, o_ref, acc_sc, *, n_kv):
    tk = KV_TILE
    sub = tk // TOKEN_TILE
    qT = jnp.concatenate([qT_ref[0, 0, i] for i in range(qT_ref.shape[2])], axis=1)
    l = jnp.zeros((1, qT.shape[1]), F32)
    for j in range(n_kv):
        p = jnp.exp2(_dot(k_ref[0, 0, j * tk:(j + 1) * tk, :], qT))
        l = l + jnp.sum(p, axis=0, keepdims=True)
        vt = jnp.concatenate([vT_ref[0, 0, j * sub + i] for i in range(sub)], axis=1)
        pv = _dot(vt, p.astype(BF16))
        if j == 0:
            acc_sc[...] = pv
        else:
            acc_sc[...] += pv
    o_ref[0] = (acc_sc[...] / l).T.astype(o_ref.dtype)


def _attn_call(qT, k, vT, score_bound_log2):
    batch, heads, nt, _, tm = qT.shape
    seq = k.shape[2]
    tq = Q_TILE
    n_kv = seq // KV_TILE
    slabs = tq // tm

    def call(body, scratch, name):
        return pl.pallas_call(
            functools.partial(body, n_kv=n_kv),
            grid=(batch, heads, seq // tq),
            in_specs=[
                pl.BlockSpec((1, 1, slabs, A_DH, tm), lambda b, h, i: (b, h, i, 0, 0)),
                pl.BlockSpec((1, 1, seq, A_DH), lambda b, h, i: (b, h, 0, 0)),
                pl.BlockSpec((1, 1, nt, A_DV, tm), lambda b, h, i: (b, h, 0, 0, 0)),
            ],
            out_specs=pl.BlockSpec((1, tq, A_DV), lambda b, h, i: (b, i, h)),
            out_shape=jax.ShapeDtypeStruct((batch, seq, heads * A_DV), BF16),
            scratch_shapes=scratch,
            compiler_params=pltpu.CompilerParams(
                dimension_semantics=("arbitrary", "arbitrary", "arbitrary"),
                vmem_limit_bytes=VMEM_LIMIT),
            name=name,
        )(qT, k, vT)

    acc = pltpu.VMEM((A_DV, tq), F32)
    return lax.cond(
        score_bound_log2 <= SAFE_SCORE_LOG2,
        lambda: call(_attn_bounded_kernel, [acc], "attn_bounded"),
        lambda: call(_attn_kernel, [pltpu.VMEM((2, KV_TILE, tq), F32),
                                    pltpu.VMEM((2, 1, tq), F32), acc], "attn"))


def _out_kernel(x_ref, hf_ref, hb_ref, o_ref, yb_ref, gml_ref, wout_ref, gffn_ref, w1_ref, w2_ref,
                y_ref):
    x = x_ref[...]
    hsum = hf_ref[...] + hb_ref[...]
    gate = jax.nn.sigmoid(o_ref[...])
    gml = gml_ref[...]
    parts = []
    for h in range(M_HEADS):
        sl = slice(h * M_DV, (h + 1) * M_DV)
        parts.append((gate[:, sl] * _rms(hsum[:, sl], gml[:, sl])).astype(BF16))
    y_a = jnp.concatenate(parts, axis=1)
    n_a = y_a.shape[1]
    x1 = x + _dot(y_a, wout_ref[0:n_a, :]) + _dot(yb_ref[...], wout_ref[n_a:, :])
    hn = _rms(x1, gffn_ref[...]).astype(BF16)
    y_ref[...] = x1
    d_ff = w1_ref.shape[1]
    for j in range(d_ff // FF_CHUNK):
        sl = slice(j * FF_CHUNK, (j + 1) * FF_CHUNK)
        t = jnp.maximum(_dot(hn, w1_ref[:, sl]), 0.0)
        y_ref[...] += _dot((t * t).astype(BF16), w2_ref[sl, :])


def _out_call(xf, hf, hb, o, yb, gml, wout, gffn, w1, w2):
    n, d = xf.shape
    tm = ROW_TILE

    def full(a):
        return pl.BlockSpec(a.shape, lambda i: (0,) * a.ndim)

    tok = lambda w: pl.BlockSpec((tm, w), lambda i: (i, 0))
    in_specs = [tok(d), tok(512), tok(512), tok(512), tok(512), full(gml), full(wout), full(gffn),
                full(w1), full(w2)]
    return pl.pallas_call(
        _out_kernel, grid=(n // tm,), in_specs=in_specs, out_specs=tok(d),
        out_shape=jax.ShapeDtypeStruct((n, d), F32),
        compiler_params=pltpu.CompilerParams(dimension_semantics=("arbitrary",),
                                             vmem_limit_bytes=VMEM_LIMIT),
        name="out",
    )(xf, hf, hb, o, yb, gml, wout, gffn, w1, w2)


def _layer(xf, pos3, invf, batch, seq, g_mix_norm, w_in, conv_w, conv_b, b_gates, g_mlstm_out,
           g_cq, g_ckv, w_uq, w_ukv, g_q, g_k, w_out, g_ffn_norm, w_ff1, w_ff2):
    d = xf.shape[1]
    c0 = 2 * 256 + 2 * 512
    wmain = w_in[:, 0:c0].astype(BF16)
    wt = w_in[:, c0:].astype(BF16)
    gates_w = wt[:, 0:N_GATES]
    cq_w = wt[:, N_GATES:N_GATES + 256]
    ckv_w = wt[:, N_GATES + 256:N_GATES + 384]
    kpe_w = wt[:, N_GATES + 384:N_GATES + 448]
    wtail = jnp.concatenate([cq_w, ckv_w, kpe_w, kpe_w[:, 0:32], gates_w,
                             jnp.zeros((d, LANES - 96 - N_GATES), BF16)], axis=1)
    wuqT = w_uq.T.astype(BF16)
    ukv = w_ukv.reshape(w_ukv.shape[0], A_HEADS, A_NOPE + A_DV)
    wuk = ukv[:, :, :A_NOPE].reshape(w_ukv.shape[0], A_HEADS * A_NOPE).astype(BF16)
    wuvT = ukv[:, :, A_NOPE:].reshape(w_ukv.shape[0], A_HEADS * A_DV).T.astype(BF16)
    gqcol = g_q.reshape(A_DH, 1)
    gk = jnp.concatenate([g_k, g_k[A_NOPE + 32:], g_k[A_NOPE:A_NOPE + 32]]).reshape(1, -1)

    qk, v, o, sf, b3f, sb, b3b, qT, k, vT = _proj_call(
        xf, pos3, g_mix_norm.reshape(1, -1), wmain, wtail, g_cq.reshape(1, -1),
        g_ckv.reshape(1, -1), wuqT, wuk, wuvT, gqcol, gk, invf, b_gates.reshape(2, 8, 1), batch, seq)

    qkc, kT = _conv_call(qk, conv_w, conv_b.reshape(1, -1), batch, seq)
    esel = jnp.tile(jnp.repeat(jnp.eye(8, M_HEADS, dtype=BF16), LANES, axis=1), (4, 1))
    hf, hb = _mlstm_call(qkc.reshape(batch, seq, 512), kT, v.reshape(batch, seq, 512),
                         sf, b3f, sb, b3b, esel)
    score_bound_log2 = (A_DH ** 0.5 * LOG2E * 1.01) * jnp.max(jnp.abs(g_q)) * jnp.max(jnp.abs(g_k))
    yb = _attn_call(qT, k, vT, score_bound_log2)
    return _out_call(xf, hf.reshape(batch * seq, -1), hb.reshape(batch * seq, -1), o,
                     yb.reshape(batch * seq, -1), g_mlstm_out.reshape(1, -1), w_out.astype(BF16),
                     g_ffn_norm.reshape(1, -1), w_ff1.astype(BF16), w_ff2.astype(BF16))


def kernel(x, positions, g_mix_norm, w_in, conv_w, conv_b, b_gates, g_mlstm_out, g_cq, g_ckv,
           w_uq, w_ukv, g_q, g_k, w_out, g_ffn_norm, w_ff1, w_ff2):
    batch, seq, d = x.shape
    assert seq % TOKEN_TILE == 0 and seq % M_CHUNK == 0
    inv_freq = ROPE_THETA ** (-jnp.arange(0, A_ROPE, 2, dtype=F32) / A_ROPE)
    invf = inv_freq.reshape(A_ROPE // 2, 1)
    pos3 = positions.reshape(batch * seq // TOKEN_TILE, 1, TOKEN_TILE)
    xf = x.reshape(batch * seq, d)
    for l in range(w_in.shape[0]):
        xf = _layer(xf, pos3, invf, batch, seq, g_mix_norm[l], w_in[l], conv_w[l], conv_b[l],
                    b_gates[l], g_mlstm_out[l], g_cq[l], g_ckv[l], w_uq[l], w_ukv[l], g_q[l],
                    g_k[l], w_out[l], g_ffn_norm[l], w_ff1[l], w_ff2[l])
    return xf.reshape(batch, seq, d)
```

```python
import functools
import math

import jax
import jax.numpy as jnp
from jax import lax
from jax.experimental import pallas as pl
from jax.experimental.pallas import tpu as pltpu

F32 = jnp.float32
BF16 = jnp.bfloat16

EPS = 1e-6
ROPE_THETA = 10000.0
M_HEADS = 4
M_DQK = 64
M_DV = 128
M_CONV = 5
M_CHUNK = 128
A_HEADS = 4
A_NOPE = 128
A_ROPE = 64
A_DV = 128
A_DH = A_NOPE + A_ROPE
N_GATES = 4 * M_HEADS

LANES = 128
TOKEN_TILE = 1024
ROW_TILE = 512
CONV_ROWS = 128
KV_TILE = 1024
Q_TILE = 1024
PROJ_PARTS = 4
MLSTM_CHUNKS_PER_STEP = 8
SAFE_SCORE_LOG2 = 64.0
FF_CHUNK = 1024
VMEM_LIMIT = 56 * 1024 * 1024

LOG2E = math.log2(math.e)


def _dot(a, b):
    return jnp.dot(a, b, preferred_element_type=F32)


def _dot_nt(a, b):
    return lax.dot_general(a, b, (((1,), (1,)), ((), ())), preferred_element_type=F32)


def _dot_tn(a, b):
    return lax.dot_general(a, b, (((0,), (0,)), ((), ())), preferred_element_type=F32)


def _rms(x, g):
    ms = jnp.mean(x * x, axis=-1, keepdims=True)
    return x * lax.rsqrt(ms + EPS) * g


def _split3(y):
    hi = y.astype(BF16).astype(F32)
    mid = (y - hi).astype(BF16).astype(F32)
    lo = y - hi - mid
    return jnp.concatenate([hi, mid, lo], axis=0).astype(BF16)


def _chunk_decay(x, *, backward):
    L = M_CHUNK
    row = lax.broadcasted_iota(jnp.int32, (L, L), 0)
    col = lax.broadcasted_iota(jnp.int32, (L, L), 1)
    umat = jnp.where((row >= col) if backward else (row <= col), 1.0, 0.0).astype(BF16)
    lf = jax.nn.log_sigmoid(x)
    cum3 = _dot(_split3(lf), umat)
    cum = cum3[0:8] + cum3[8:16] + cum3[16:24]
    a = jnp.broadcast_to(jnp.sum(lf, axis=1, keepdims=True), (8, L))
    b = pltpu.roll(cum, 4, axis=0)
    a = pltpu.roll(a, 4, axis=0)
    w_log = a - b + x
    g = jnp.broadcast_to(jnp.max(w_log, axis=1, keepdims=True), (8, L))
    stats = jnp.concatenate([jnp.exp(w_log - g), (x - b) * LOG2E, a * LOG2E, g * LOG2E], axis=0)
    b3 = jnp.concatenate([_split3(b * LOG2E), jnp.zeros((8, L), BF16)], axis=0)
    return stats, b3


def _proj_kernel(x_ref, pos_ref, gmix_ref, wmain_ref, wtail_ref, gcq_ref, gckv_ref, wuqT_ref,
                 wuk_ref, wuvT_ref, gqcol_ref, gk_ref, invf_ref, bg_ref,
                 qk_ref, v_ref, o_ref, sf_ref, b3f_ref, sb_ref, b3b_ref, qT_ref, k_ref, vT_ref):
    tm = x_ref.shape[0]
    parts = [slice(i * tm // PROJ_PARTS, (i + 1) * tm // PROJ_PARTS) for i in range(PROJ_PARTS)]

    def project(rows):
        hn = _rms(x_ref[rows, :], gmix_ref[...]).astype(BF16)
        z = _dot(hn, wmain_ref[...])
        qk_ref[rows, :] = z[:, 0:512]
        v_ref[rows, :] = z[:, 512:1024].astype(BF16)
        o_ref[rows, :] = z[:, 1024:1536]
        return _dot(hn, wtail_ref[...])

    def latents(zt, rows):
        cq = zt[:, 0:256]
        ckv = zt[:, 256:384]
        tail = zt[:, 384:512]

        tailT = tail.T
        for d, (s_ref, b3_ref) in enumerate(((sf_ref, b3f_ref), (sb_ref, b3b_ref))):
            gx = tailT[96 + 8 * d:104 + 8 * d] + bg_ref[d]
            for j in range(gx.shape[1] // M_CHUNK):
                sl = slice(j * M_CHUNK, (j + 1) * M_CHUNK)
                dst = slice(rows.start + sl.start, rows.start + sl.stop)
                stats, b3 = _chunk_decay(gx[:, sl], backward=d == 1)
                s_ref[0, :, dst] = stats
                b3_ref[0, :, dst] = b3

        pos = pos_ref[0][:, rows].astype(F32)
        angT = invf_ref[...] * pos
        cosT = jnp.cos(angT)
        sinT = jnp.sin(angT)

        cqnT = _rms(cq, gcq_ref[...]).T.astype(BF16)
        qT = _dot(wuqT_ref[...], cqnT)
        gq = gqcol_ref[...] * (A_DH ** -0.5 * LOG2E)
        for h in range(A_HEADS):
            qh = qT[h * A_DH:(h + 1) * A_DH]
            ms = jnp.sum(qh * qh, axis=0, keepdims=True) * (1.0 / A_DH)
            qn = qh * lax.rsqrt(ms + EPS) * gq
            x1 = qn[A_NOPE:A_NOPE + 32]
            x2 = qn[A_NOPE + 32:A_DH]
            qT_ref[0, h, 0, 0:A_NOPE, rows] = qn[0:A_NOPE].astype(BF16)
            qT_ref[0, h, 0, A_NOPE:A_NOPE + 32, rows] = (x1 * cosT - x2 * sinT).astype(BF16)
            qT_ref[0, h, 0, A_NOPE + 32:A_DH, rows] = (x2 * cosT + x1 * sinT).astype(BF16)

        ckvn = _rms(ckv, gckv_ref[...])
        knope = _dot(ckvn.astype(BF16), wuk_ref[...])
        vT = _dot(wuvT_ref[...], ckvn.T.astype(BF16))
        for h in range(A_HEADS):
            vT_ref[0, h, 0, :, rows] = vT[h * A_DV:(h + 1) * A_DV].astype(BF16)

        tab = jnp.concatenate([cosT, cosT, -sinT, sinT], axis=0).T
        cc = tab[:, 0:A_ROPE]
        ss = tab[:, A_ROPE:2 * A_ROPE]
        gk = gk_ref[...]
        kpe = tail[:, 0:A_ROPE]
        kpe_sw = pltpu.roll(tail, LANES - 32, axis=1)[:, 0:A_ROPE]
        ss_pe = jnp.sum(kpe * kpe, axis=-1, keepdims=True)
        krope = (kpe * gk[:, A_NOPE:A_DH] * cc + kpe_sw * gk[:, A_DH:A_DH + A_ROPE] * ss)
        for h in range(A_HEADS):
            kn = knope[:, h * A_NOPE:(h + 1) * A_NOPE]
            ms = (jnp.sum(kn * kn, axis=-1, keepdims=True) + ss_pe) * (1.0 / A_DH)
            r = lax.rsqrt(ms + EPS)
            k_ref[0, h, rows, 0:A_NOPE] = (kn * r * gk[:, 0:A_NOPE]).astype(BF16)
            k_ref[0, h, rows, A_NOPE:A_DH] = (krope * r).astype(BF16)

    zt = project(parts[0])
    for i, rows in enumerate(parts):
        zt_next = project(parts[i + 1]) if i + 1 < len(parts) else None
        latents(zt, rows)
        zt = zt_next


def _proj_call(xf, pos3, gmix, wmain, wtail, gcq, gckv, wuqT, wuk, wuvT, gqcol, gk, invf, bg,
               batch, seq):
    n, d = xf.shape
    tm = TOKEN_TILE
    nt = seq // tm
    steps = n // tm

    def full(a):
        return pl.BlockSpec(a.shape, lambda i: (0,) * a.ndim)

    tok = lambda w: pl.BlockSpec((tm, w), lambda i: (i, 0))
    in_specs = [tok(d), pl.BlockSpec((1, 1, tm), lambda i: (i, 0, 0)), full(gmix), full(wmain),
                full(wtail), full(gcq), full(gckv), full(wuqT), full(wuk), full(wuvT), full(gqcol),
                full(gk), full(invf), full(bg)]
    stat_spec = pl.BlockSpec((1, 32, tm), lambda i: (i // nt, 0, i % nt))
    out_shape = [
        jax.ShapeDtypeStruct((n, 512), F32),
        jax.ShapeDtypeStruct((n, 512), BF16),
        jax.ShapeDtypeStruct((n, 512), F32),
        jax.ShapeDtypeStruct((batch, 32, seq), F32),
        jax.ShapeDtypeStruct((batch, 32, seq), BF16),
        jax.ShapeDtypeStruct((batch, 32, seq), F32),
        jax.ShapeDtypeStruct((batch, 32, seq), BF16),
        jax.ShapeDtypeStruct((batch, A_HEADS, nt, A_DH, tm), BF16),
        jax.ShapeDtypeStruct((batch, A_HEADS, seq, A_DH), BF16),
        jax.ShapeDtypeStruct((batch, A_HEADS, nt, A_DV, tm), BF16),
    ]
    out_specs = [
        tok(512), tok(512), tok(512), stat_spec, stat_spec, stat_spec, stat_spec,
        pl.BlockSpec((1, A_HEADS, 1, A_DH, tm), lambda i: (i // nt, 0, i % nt, 0, 0)),
        pl.BlockSpec((1, A_HEADS, tm, A_DH), lambda i: (i // nt, 0, i % nt, 0)),
        pl.BlockSpec((1, A_HEADS, 1, A_DV, tm), lambda i: (i // nt, 0, i % nt, 0, 0)),
    ]
    return pl.pallas_call(
        _proj_kernel, grid=(steps,), in_specs=in_specs, out_specs=out_specs, out_shape=out_shape,
        compiler_params=pltpu.CompilerParams(dimension_semantics=("arbitrary",),
                                             vmem_limit_bytes=VMEM_LIMIT),
        name="proj",
    )(xf, pos3, gmix, wmain, wtail, gcq, gckv, wuqT, wuk, wuvT, gqcol, gk, invf, bg)


def _conv_kernel(cur_ref, prev_ref, next_ref, w_ref, b_ref, qk_ref, kT_ref, *, tiles_per_seq):
    t = pl.program_id(0) % tiles_per_seq
    tm = cur_ref.shape[0]
    prev = jnp.where(t == 0, 0.0, prev_ref[...])
    nxt = jnp.where(t == tiles_per_seq - 1, 0.0, next_ref[...])
    cw = w_ref[...]
    rb = CONV_ROWS
    for blk in range(tm // rb):
        lo = blk * rb
        top = prev if blk == 0 else cur_ref[lo - 8:lo, :]
        bot = nxt if lo + rb == tm else cur_ref[lo + rb:lo + rb + 8, :]
        ext = jnp.concatenate([top, cur_ref[lo:lo + rb, :], bot], axis=0)
        u = b_ref[...]
        for j in range(M_CONV):
            shifted = (ext if j == M_CONV // 2
                       else pltpu.roll(ext, (M_CONV // 2 - j) % (rb + 16), axis=0))
            u = u + cw[j:j + 1, :] * shifted[8:8 + rb, :]
        qk = u * jax.nn.sigmoid(u)
        k = qk[:, 256:512]
        qk_ref[lo:lo + rb, 0:256] = (qk[:, 0:256] * (M_DQK ** -0.5)).astype(BF16)
        qk_ref[lo:lo + rb, 256:512] = k.astype(BF16)
        kT_ref[0, :, lo:lo + rb] = k.T.astype(BF16)


def _conv_call(qk, convw, convb, batch, seq):
    n = qk.shape[0]
    tm = ROW_TILE
    nt = seq // tm
    r8 = tm // 8
    return pl.pallas_call(
        functools.partial(_conv_kernel, tiles_per_seq=nt),
        grid=(n // tm,),
        in_specs=[
            pl.BlockSpec((tm, 512), lambda i: (i, 0)),
            pl.BlockSpec((8, 512), lambda i: (jnp.maximum(i * r8 - 1, 0), 0)),
            pl.BlockSpec((8, 512), lambda i: (jnp.minimum((i + 1) * r8, n // 8 - 1), 0)),
            pl.BlockSpec(convw.shape, lambda i: (0, 0)),
            pl.BlockSpec(convb.shape, lambda i: (0, 0)),
        ],
        out_specs=[pl.BlockSpec((tm, 512), lambda i: (i, 0)),
                   pl.BlockSpec((1, 256, tm), lambda i: (i // nt, 0, i % nt))],
        out_shape=[jax.ShapeDtypeStruct((n, 512), BF16),
                   jax.ShapeDtypeStruct((batch, 256, seq), BF16)],
        compiler_params=pltpu.CompilerParams(dimension_semantics=("arbitrary",),
                                             vmem_limit_bytes=VMEM_LIMIT),
        name="conv",
    )(qk, qk, qk, convw, convb)


def _mlstm_kernel(qkf_ref, kTf_ref, vf_ref, sf_ref, b3f_ref, qkb_ref, kTb_ref, vb_ref, sb_ref,
                  b3b_ref, esel_ref, hf_ref, hb_ref, cf_sc, cb_sc, mf_sc, mb_sc):
    L = M_CHUNK
    n_sub = qkf_ref.shape[1] // L

    @pl.when(pl.program_id(1) == 0)
    def _():
        cf_sc[...] = jnp.zeros_like(cf_sc)
        cb_sc[...] = jnp.zeros_like(cb_sc)
        mf_sc[...] = jnp.zeros_like(mf_sc)
        mb_sc[...] = jnp.zeros_like(mb_sc)

    esel = esel_ref[...]
    row = lax.broadcasted_iota(jnp.int32, (L, L), 0)
    col = lax.broadcasted_iota(jnp.int32, (L, L), 1)
    ones_blk = jnp.ones((L, M_DV), BF16)
    fw = dict(qk=qkf_ref, kT=kTf_ref, v=vf_ref, s=sf_ref, b3=b3f_ref, out=hf_ref, c_sc=cf_sc,
              m_sc=mf_sc, visible=col <= row)
    bw = dict(qk=qkb_ref, kT=kTb_ref, v=vb_ref, s=sb_ref, b3=b3b_ref, out=hb_ref, c_sc=cb_sc,
              m_sc=mb_sc, visible=col >= row)

    work = []
    for d in (fw, bw):
        m = d["m_sc"][...]
        order = range(n_sub) if d is fw else range(n_sub - 1, -1, -1)
        for i in order:
            sl = slice(i * L, (i + 1) * L)
            stats = d["s"][0, :, sl]
            a, g = stats[16:24], stats[24:32]
            m_new = jnp.maximum(a + m, g)
            work.append((d, sl, dict(
                w=stats[0:8], r=stats[8:16], m_prev=m,
                dec=jnp.exp2(a + m - m_new), add=jnp.exp2(g - m_new),
                brep=_dot_tn(d["b3"][0, :, sl], esel))))
            m = m_new
        d["m_sc"][...] = m
    work = [work[j + k * n_sub] for j in range(n_sub) for k in range(2)]
    units = [(d, sl, gt, h) for d, sl, gt in work for h in range(M_HEADS)]

    grams, v_augs, s_augs = [], [], []
    for d, sl, gt, h in units:
        q_h = d["qk"][0, sl, h * M_DQK:(h + 1) * M_DQK]
        k_h = d["qk"][0, sl, 256 + h * M_DQK:256 + (h + 1) * M_DQK]
        grams.append(_dot_nt(q_h, k_h))
        v_augs.append(jnp.concatenate([d["v"][0, sl, h * M_DV:(h + 1) * M_DV], ones_blk], axis=1))
    for (d, sl, gt, h), v_aug in zip(units, v_augs):
        kTw = d["kT"][0, h * M_DQK:(h + 1) * M_DQK, sl].astype(F32) * gt["w"][h:h + 1, :]
        s_augs.append(_dot(kTw.astype(BF16), v_aug))

    states = {id(fw): [cf_sc[h] for h in range(M_HEADS)],
              id(bw): [cb_sc[h] for h in range(M_HEADS)]}
    for (d, sl, gt, h), gram, v_aug, s_aug in zip(units, grams, v_augs, s_augs):
        q_h = d["qk"][0, sl, h * M_DQK:(h + 1) * M_DQK]
        brep = gt["brep"][:, h * LANES:(h + 1) * LANES]
        bm = brep + gt["m_prev"][h:h + 1, :]
        dmat = jnp.where(d["visible"], brep + gt["r"][h:h + 1, :], -jnp.inf)
        m_t = jnp.maximum(bm, jnp.max(dmat, axis=1, keepdims=True))
        p = jnp.exp2(dmat - m_t) * gram
        inter = jnp.exp2(bm - m_t)
        c_prev = states[id(d)][h]
        lhs = jnp.concatenate(
            [p.astype(BF16), (inter[:, 0:M_DQK] * q_h.astype(F32)).astype(BF16)], axis=1)
        rhs = jnp.concatenate([v_aug, c_prev.astype(BF16)], axis=0)
        res = _dot(lhs, rhs)
        den = jnp.maximum(jnp.abs(res[:, M_DV:2 * M_DV]), jnp.exp2(-m_t))
        d["out"][0, sl, h * M_DV:(h + 1) * M_DV] = res[:, 0:M_DV] / den
        dec_h = jnp.concatenate([gt["dec"][h:h + 1, :]] * 2, axis=1)
        add_h = jnp.concatenate([gt["add"][h:h + 1, :]] * 2, axis=1)
        states[id(d)][h] = dec_h * c_prev + add_h * s_aug
    for h in range(M_HEADS):
        cf_sc[h] = states[id(fw)][h]
        cb_sc[h] = states[id(bw)][h]


def _mlstm_call(qk3, kT3, v3, sf, b3f, sb, b3b, esel):
    batch, seq, _ = qk3.shape
    L = M_CHUNK * MLSTM_CHUNKS_PER_STEP
    nc = seq // L
    fw_rows = lambda w: pl.BlockSpec((1, L, w), lambda b, c: (b, c, 0))
    bw_rows = lambda w: pl.BlockSpec((1, L, w), lambda b, c: (b, nc - 1 - c, 0))
    fw_lanes = lambda r: pl.BlockSpec((1, r, L), lambda b, c: (b, 0, c))
    bw_lanes = lambda r: pl.BlockSpec((1, r, L), lambda b, c: (b, 0, nc - 1 - c))
    state = pltpu.VMEM((M_HEADS, M_DQK, 2 * M_DV), F32)
    return pl.pallas_call(
        _mlstm_kernel,
        grid=(batch, nc),
        in_specs=[fw_rows(512), fw_lanes(256), fw_rows(512), fw_lanes(32), fw_lanes(32),
                  bw_rows(512), bw_lanes(256), bw_rows(512), bw_lanes(32), bw_lanes(32),
                  pl.BlockSpec(esel.shape, lambda b, c: (0, 0))],
        out_specs=[fw_rows(512), bw_rows(512)],
        out_shape=[jax.ShapeDtypeStruct((batch, seq, 512), F32)] * 2,
        scratch_shapes=[state, state, pltpu.VMEM((8, M_CHUNK), F32), pltpu.VMEM((8, M_CHUNK), F32)],
        compiler_params=pltpu.CompilerParams(dimension_semantics=("arbitrary", "arbitrary"),
                                             vmem_limit_bytes=VMEM_LIMIT),
        name="mlstm",
    )(qk3, kT3, v3, sf, b3f, qk3, kT3, v3, sb, b3b, esel)


def _attn_kernel(qT_ref, k_ref, vT_ref, o_ref, s_sc, cm_sc, acc_sc, *, n_kv):
    tk = KV_TILE
    sub = tk // TOKEN_TILE
    qT = jnp.concatenate([qT_ref[0, 0, i] for i in range(qT_ref.shape[2])], axis=1)
    tq = qT.shape[1]

    def scores(j, slot):
        sT = _dot(k_ref[0, 0, j * tk:(j + 1) * tk, :], qT)
        s_sc[slot] = sT
        cm_sc[slot] = jnp.max(sT, axis=0, keepdims=True)

    m = jnp.full((1, tq), -jnp.inf, F32)
    l = jnp.zeros((1, tq), F32)
    scores(0, 0)
    for j in range(n_kv):
        slot = j % 2
        if j + 1 < n_kv:
            scores(j + 1, 1 - slot)
        m_new = jnp.maximum(m, cm_sc[slot])
        alpha = jnp.exp2(m - m_new)
        p = jnp.exp2(s_sc[slot] - m_new)
        l = alpha * l + jnp.sum(p, axis=0, keepdims=True)
        vt = jnp.concatenate([vT_ref[0, 0, j * sub + i] for i in range(sub)], axis=1)
        pv = _dot(vt, p.astype(BF16))
        if j == 0:
            acc_sc[...] = pv
        else:
            acc_sc[...] = alpha * acc_sc[...] + pv
        m = m_new
    o_ref[0] = (acc_sc[...] / l).T.astype(o_ref.dtype)


def _attn_bounded_kernel(qT_ref, k_ref, vT_ref, o_ref, acc_sc, *, n_kv):
    tk = KV_TILE
    sub = tk // TOKEN_TILE
    qT = jnp.concatenate([qT_ref[0, 0, i] for i in range(qT_ref.shape[2])], axis=1)
    l = jnp.zeros((1, qT.shape[1]), F32)
    for j in range(n_kv):
        p = jnp.exp2(_dot(k_ref[0, 0, j * tk:(j + 1) * tk, :], qT))
        l = l + jnp.sum(p, axis=0, keepdims=True)
        vt = jnp.concatenate([vT_ref[0, 0, j * sub + i] for i in range(sub)], axis=1)
        pv = _dot(vt, p.astype(BF16))
        if j == 0:
            acc_sc[...] = pv
        else:
            acc_sc[...] += pv
    o_ref[0] = (acc_sc[...] / l).T.astype(o_ref.dtype)


def _attn_call(qT, k, vT, score_bound_log2):
    batch, heads, nt, _, tm = qT.shape
    seq = k.shape[2]
    tq = Q_TILE
    n_kv = seq // KV_TILE
    slabs = tq // tm

    def body(safe_ref, qT_ref, k_ref, vT_ref, o_ref, s_sc, cm_sc, acc_sc):
        @pl.when(safe_ref[0] != 0)
        def _():
            _attn_bounded_kernel(qT_ref, k_ref, vT_ref, o_ref, acc_sc, n_kv=n_kv)

        @pl.when(safe_ref[0] == 0)
        def _():
            _attn_kernel(qT_ref, k_ref, vT_ref, o_ref, s_sc, cm_sc, acc_sc, n_kv=n_kv)

    safe = (score_bound_log2 <= SAFE_SCORE_LOG2).astype(jnp.int32).reshape(1)
    return pl.pallas_call(
        body,
        grid_spec=pltpu.PrefetchScalarGridSpec(
            num_scalar_prefetch=1,
            grid=(batch, heads, seq // tq),
            in_specs=[
                pl.BlockSpec((1, 1, slabs, A_DH, tm), lambda b, h, i, safe: (b, h, i, 0, 0)),
                pl.BlockSpec((1, 1, seq, A_DH), lambda b, h, i, safe: (b, h, 0, 0)),
                pl.BlockSpec((1, 1, nt, A_DV, tm), lambda b, h, i, safe: (b, h, 0, 0, 0)),
            ],
            out_specs=pl.BlockSpec((1, tq, A_DV), lambda b, h, i, safe: (b, i, h)),
            scratch_shapes=[pltpu.VMEM((2, KV_TILE, tq), F32), pltpu.VMEM((2, 1, tq), F32),
                            pltpu.VMEM((A_DV, tq), F32)]),
        out_shape=jax.ShapeDtypeStruct((batch, seq, heads * A_DV), BF16),
        compiler_params=pltpu.CompilerParams(
            dimension_semantics=("arbitrary", "arbitrary", "arbitrary"),
            vmem_limit_bytes=VMEM_LIMIT),
        name="attn",
    )(safe, qT, k, vT)


def _out_kernel(x_ref, hf_ref, hb_ref, o_ref, yb_ref, gml_ref, wout_ref, gffn_ref, w1_ref, w2_ref,
                y_ref):
    x = x_ref[...]
    hsum = hf_ref[...] + hb_ref[...]
    gate = jax.nn.sigmoid(o_ref[...])
    gml = gml_ref[...]
    parts = []
    for h in range(M_HEADS):
        sl = slice(h * M_DV, (h + 1) * M_DV)
        parts.append((gate[:, sl] * _rms(hsum[:, sl], gml[:, sl])).astype(BF16))
    y_a = jnp.concatenate(parts, axis=1)
    n_a = y_a.shape[1]
    x1 = x + _dot(y_a, wout_ref[0:n_a, :]) + _dot(yb_ref[...], wout_ref[n_a:, :])
    hn = _rms(x1, gffn_ref[...]).astype(BF16)
    y_ref[...] = x1
    d_ff = w1_ref.shape[1]
    for j in range(d_ff // FF_CHUNK):
        sl = slice(j * FF_CHUNK, (j + 1) * FF_CHUNK)
        t = jnp.maximum(_dot(hn, w1_ref[:, sl]), 0.0)
        y_ref[...] += _dot((t * t).astype(BF16), w2_ref[sl, :])


def _out_call(xf, hf, hb, o, yb, gml, wout, gffn, w1, w2):
    n, d = xf.shape
    tm = ROW_TILE

    def full(a):
        return pl.BlockSpec(a.shape, lambda i: (0,) * a.ndim)

    tok = lambda w: pl.BlockSpec((tm, w), lambda i: (i, 0))
    in_specs = [tok(d), tok(512), tok(512), tok(512), tok(512), full(gml), full(wout), full(gffn),
                full(w1), full(w2)]
    return pl.pallas_call(
        _out_kernel, grid=(n // tm,), in_specs=in_specs, out_specs=tok(d),
        out_shape=jax.ShapeDtypeStruct((n, d), F32),
        compiler_params=pltpu.CompilerParams(dimension_semantics=("arbitrary",),
                                             vmem_limit_bytes=VMEM_LIMIT),
        name="out",
    )(xf, hf, hb, o, yb, gml, wout, gffn, w1, w2)


def _layer(xf, pos3, invf, batch, seq, g_mix_norm, w_in, conv_w, conv_b, b_gates, g_mlstm_out,
           g_cq, g_ckv, w_uq, w_ukv, g_q, g_k, w_out, g_ffn_norm, w_ff1, w_ff2):
    d = xf.shape[1]
    c0 = 2 * 256 + 2 * 512
    wmain = w_in[:, 0:c0].astype(BF16)
    wt = w_in[:, c0:].astype(BF16)
    gates_w = wt[:, 0:N_GATES]
    cq_w = wt[:, N_GATES:N_GATES + 256]
    ckv_w = wt[:, N_GATES + 256:N_GATES + 384]
    kpe_w = wt[:, N_GATES + 384:N_GATES + 448]
    wtail = jnp.concatenate([cq_w, ckv_w, kpe_w, kpe_w[:, 0:32], gates_w,
                             jnp.zeros((d, LANES - 96 - N_GATES), BF16)], axis=1)
    wuqT = w_uq.T.astype(BF16)
    ukv = w_ukv.reshape(w_ukv.shape[0], A_HEADS, A_NOPE + A_DV)
    wuk = ukv[:, :, :A_NOPE].reshape(w_ukv.shape[0], A_HEADS * A_NOPE).astype(BF16)
    wuvT = ukv[:, :, A_NOPE:].reshape(w_ukv.shape[0], A_HEADS * A_DV).T.astype(BF16)
    gqcol = g_q.reshape(A_DH, 1)
    gk = jnp.concatenate([g_k, g_k[A_NOPE + 32:], g_k[A_NOPE:A_NOPE + 32]]).reshape(1, -1)

    qk, v, o, sf, b3f, sb, b3b, qT, k, vT = _proj_call(
        xf, pos3, g_mix_norm.reshape(1, -1), wmain, wtail, g_cq.reshape(1, -1),
        g_ckv.reshape(1, -1), wuqT, wuk, wuvT, gqcol, gk, invf, b_gates.reshape(2, 8, 1), batch, seq)

    qkc, kT = _conv_call(qk, conv_w, conv_b.reshape(1, -1), batch, seq)
    esel = jnp.tile(jnp.repeat(jnp.eye(8, M_HEADS, dtype=BF16), LANES, axis=1), (4, 1))
    hf, hb = _mlstm_call(qkc.reshape(batch, seq, 512), kT, v.reshape(batch, seq, 512),
                         sf, b3f, sb, b3b, esel)
    score_bound_log2 = (A_DH ** 0.5 * LOG2E * 1.01) * jnp.max(jnp.abs(g_q)) * jnp.max(jnp.abs(g_k))
    yb = _attn_call(qT, k, vT, score_bound_log2)
    return _out_call(xf, hf.reshape(batch * seq, -1), hb.reshape(batch * seq, -1), o,
                     yb.reshape(batch * seq, -1), g_mlstm_out.reshape(1, -1), w_out.astype(BF16),
                     g_ffn_norm.reshape(1, -1), w_ff1.astype(BF16), w_ff2.astype(BF16))


def kernel(x, positions, g_mix_norm, w_in, conv_w, conv_b, b_gates, g_mlstm_out, g_cq, g_ckv,
           w_uq, w_ukv, g_q, g_k, w_out, g_ffn_norm, w_ff1, w_ff2):
    batch, seq, d = x.shape
    assert seq % TOKEN_TILE == 0 and seq % M_CHUNK == 0
    inv_freq = ROPE_THETA ** (-jnp.arange(0, A_ROPE, 2, dtype=F32) / A_ROPE)
    invf = inv_freq.reshape(A_ROPE // 2, 1)
    pos3 = positions.reshape(batch * seq // TOKEN_TILE, 1, TOKEN_TILE)
    xf = x.reshape(batch * seq, d)
    for l in range(w_in.shape[0]):
        xf = _layer(xf, pos3, invf, batch, seq, g_mix_norm[l], w_in[l], conv_w[l], conv_b[l],
                    b_gates[l], g_mlstm_out[l], g_cq[l], g_ckv[l], w_uq[l], w_ukv[l], g_q[l],
                    g_k[l], w_out[l], g_ffn_norm[l], w_ff1[l], w_ff2[l])
    return xf.reshape(batch, seq, d)
```

```python
import functools
import math

import jax
import jax.numpy as jnp
from jax import lax
from jax.experimental import pallas as pl
from jax.experimental.pallas import tpu as pltpu

F32 = jnp.float32
BF16 = jnp.bfloat16

EPS = 1e-6
ROPE_THETA = 10000.0
M_HEADS = 4
M_DQK = 64
M_DV = 128
M_CONV = 5
M_CHUNK = 128
A_HEADS = 4
A_NOPE = 128
A_ROPE = 64
A_DV = 128
A_DH = A_NOPE + A_ROPE
N_GATES = 4 * M_HEADS

LANES = 128
TOKEN_TILE = 1024
ROW_TILE = 512
CONV_ROWS = 128
KV_TILE = 1024
Q_TILE = 1024
PROJ_PARTS = 4
MLSTM_CHUNKS_PER_STEP = 8
SAFE_SCORE_LOG2 = 64.0
FF_CHUNK = 1024
VMEM_LIMIT = 56 * 1024 * 1024

LOG2E = math.log2(math.e)


def _dot(a, b):
    return jnp.dot(a, b, preferred_element_type=F32)


def _dot_nt(a, b):
    return lax.dot_general(a, b, (((1,), (1,)), ((), ())), preferred_element_type=F32)


def _dot_tn(a, b):
    return lax.dot_general(a, b, (((0,), (0,)), ((), ())), preferred_element_type=F32)


def _rms(x, g):
    ms = jnp.mean(x * x, axis=-1, keepdims=True)
    return x * lax.rsqrt(ms + EPS) * g


def _split3(y):
    hi = y.astype(BF16).astype(F32)
    mid = (y - hi).astype(BF16).astype(F32)
    lo = y - hi - mid
    return jnp.concatenate([hi, mid, lo], axis=0).astype(BF16)


def _chunk_decay(x, *, backward):
    L = M_CHUNK
    row = lax.broadcasted_iota(jnp.int32, (L, L), 0)
    col = lax.broadcasted_iota(jnp.int32, (L, L), 1)
    umat = jnp.where((row >= col) if backward else (row <= col), 1.0, 0.0).astype(BF16)
    lf = jax.nn.log_sigmoid(x)
    cum3 = _dot(_split3(lf), umat)
    cum = cum3[0:8] + cum3[8:16] + cum3[16:24]
    a = jnp.broadcast_to(jnp.sum(lf, axis=1, keepdims=True), (8, L))
    b = pltpu.roll(cum, 4, axis=0)
    a = pltpu.roll(a, 4, axis=0)
    w_log = a - b + x
    g = jnp.broadcast_to(jnp.max(w_log, axis=1, keepdims=True), (8, L))
    stats = jnp.concatenate([jnp.exp(w_log - g), (x - b) * LOG2E, a * LOG2E, g * LOG2E], axis=0)
    b3 = jnp.concatenate([_split3(b * LOG2E), jnp.zeros((8, L), BF16)], axis=0)
    return stats, b3


def _proj_kernel(x_ref, pos_ref, gmix_ref, wmain_ref, wtail_ref, gcq_ref, gckv_ref, wuqT_ref,
                 wuk_ref, wuvT_ref, gqcol_ref, gk_ref, invf_ref, bg_ref,
                 qk_ref, v_ref, o_ref, sf_ref, b3f_ref, sb_ref, b3b_ref, qT_ref, k_ref, vT_ref):
    tm = x_ref.shape[0]
    parts = [slice(i * tm // PROJ_PARTS, (i + 1) * tm // PROJ_PARTS) for i in range(PROJ_PARTS)]

    def project(rows):
        hn = _rms(x_ref[rows, :], gmix_ref[...]).astype(BF16)
        z = _dot(hn, wmain_ref[:, 0:1536].astype(BF16))
        qk_ref[rows, :] = z[:, 0:512]
        v_ref[rows, :] = z[:, 512:1024].astype(BF16)
        o_ref[rows, :] = z[:, 1024:1536]
        return _dot(hn, wtail_ref[...])

    def latents(zt, rows):
        cq = zt[:, 0:256]
        ckv = zt[:, 256:384]
        tail = zt[:, 384:512]

        tailT = tail.T
        for d, (s_ref, b3_ref) in enumerate(((sf_ref, b3f_ref), (sb_ref, b3b_ref))):
            gx = tailT[96 + 8 * d:104 + 8 * d] + bg_ref[d]
            for j in range(gx.shape[1] // M_CHUNK):
                sl = slice(j * M_CHUNK, (j + 1) * M_CHUNK)
                dst = slice(rows.start + sl.start, rows.start + sl.stop)
                stats, b3 = _chunk_decay(gx[:, sl], backward=d == 1)
                s_ref[0, :, dst] = stats
                b3_ref[0, :, dst] = b3

        pos = pos_ref[0][:, rows].astype(F32)
        angT = invf_ref[...] * pos
        cosT = jnp.cos(angT)
        sinT = jnp.sin(angT)

        cqnT = _rms(cq, gcq_ref[...]).T.astype(BF16)
        qT = _dot(wuqT_ref[...], cqnT)
        gq = gqcol_ref[...] * (A_DH ** -0.5 * LOG2E)
        for h in range(A_HEADS):
            qh = qT[h * A_DH:(h + 1) * A_DH]
            ms = jnp.sum(qh * qh, axis=0, keepdims=True) * (1.0 / A_DH)
            qn = qh * lax.rsqrt(ms + EPS) * gq
            x1 = qn[A_NOPE:A_NOPE + 32]
            x2 = qn[A_NOPE + 32:A_DH]
            qT_ref[0, h, 0, 0:A_NOPE, rows] = qn[0:A_NOPE].astype(BF16)
            qT_ref[0, h, 0, A_NOPE:A_NOPE + 32, rows] = (x1 * cosT - x2 * sinT).astype(BF16)
            qT_ref[0, h, 0, A_NOPE + 32:A_DH, rows] = (x2 * cosT + x1 * sinT).astype(BF16)

        ckvn = _rms(ckv, gckv_ref[...])
        knope = _dot(ckvn.astype(BF16), wuk_ref[...])
        vT = _dot(wuvT_ref[...], ckvn.T.astype(BF16))
        for h in range(A_HEADS):
            vT_ref[0, h, 0, :, rows] = vT[h * A_DV:(h + 1) * A_DV].astype(BF16)

        tab = jnp.concatenate([cosT, cosT, -sinT, sinT], axis=0).T
        cc = tab[:, 0:A_ROPE]
        ss = tab[:, A_ROPE:2 * A_ROPE]
        gk = gk_ref[...]
        kpe = tail[:, 0:A_ROPE]
        kpe_sw = pltpu.roll(tail, LANES - 32, axis=1)[:, 0:A_ROPE]
        ss_pe = jnp.sum(kpe * kpe, axis=-1, keepdims=True)
        krope = (kpe * gk[:, A_NOPE:A_DH] * cc + kpe_sw * gk[:, A_DH:A_DH + A_ROPE] * ss)
        for h in range(A_HEADS):
            kn = knope[:, h * A_NOPE:(h + 1) * A_NOPE]
            ms = (jnp.sum(kn * kn, axis=-1, keepdims=True) + ss_pe) * (1.0 / A_DH)
            r = lax.rsqrt(ms + EPS)
            k_ref[0, h, rows, 0:A_NOPE] = (kn * r * gk[:, 0:A_NOPE]).astype(BF16)
            k_ref[0, h, rows, A_NOPE:A_DH] = (krope * r).astype(BF16)

    zt = project(parts[0])
    for i, rows in enumerate(parts):
        zt_next = project(parts[i + 1]) if i + 1 < len(parts) else None
        latents(zt, rows)
        zt = zt_next


def _proj_call(xf, pos3, gmix, wmain, wtail, gcq, gckv, wuqT, wuk, wuvT, gqcol, gk, invf, bg,
               batch, seq):
    n, d = xf.shape
    tm = TOKEN_TILE
    nt = seq // tm
    steps = n // tm

    def full(a):
        return pl.BlockSpec(a.shape, lambda i: (0,) * a.ndim)

    tok = lambda w: pl.BlockSpec((tm, w), lambda i: (i, 0))
    in_specs = [tok(d), pl.BlockSpec((1, 1, tm), lambda i: (i, 0, 0)), full(gmix),
                pl.BlockSpec(wmain.shape, lambda i: (0, 0), pipeline_mode=pl.Buffered(1)),
                full(wtail), full(gcq), full(gckv), full(wuqT), full(wuk), full(wuvT), full(gqcol),
                full(gk), full(invf), full(bg)]
    stat_spec = pl.BlockSpec((1, 32, tm), lambda i: (i // nt, 0, i % nt))
    out_shape = [
        jax.ShapeDtypeStruct((n, 512), F32),
        jax.ShapeDtypeStruct((n, 512), BF16),
        jax.ShapeDtypeStruct((n, 512), F32),
        jax.ShapeDtypeStruct((batch, 32, seq), F32),
        jax.ShapeDtypeStruct((batch, 32, seq), BF16),
        jax.ShapeDtypeStruct((batch, 32, seq), F32),
        jax.ShapeDtypeStruct((batch, 32, seq), BF16),
        jax.ShapeDtypeStruct((batch, A_HEADS, nt, A_DH, tm), BF16),
        jax.ShapeDtypeStruct((batch, A_HEADS, seq, A_DH), BF16),
        jax.ShapeDtypeStruct((batch, A_HEADS, nt, A_DV, tm), BF16),
    ]
    out_specs = [
        tok(512), tok(512), tok(512), stat_spec, stat_spec, stat_spec, stat_spec,
        pl.BlockSpec((1, A_HEADS, 1, A_DH, tm), lambda i: (i // nt, 0, i % nt, 0, 0)),
        pl.BlockSpec((1, A_HEADS, tm, A_DH), lambda i: (i // nt, 0, i % nt, 0)),
        pl.BlockSpec((1, A_HEADS, 1, A_DV, tm), lambda i: (i // nt, 0, i % nt, 0, 0)),
    ]
    return pl.pallas_call(
        _proj_kernel, grid=(steps,), in_specs=in_specs, out_specs=out_specs, out_shape=out_shape,
        compiler_params=pltpu.CompilerParams(dimension_semantics=("arbitrary",),
                                             vmem_limit_bytes=VMEM_LIMIT),
        name="proj",
    )(xf, pos3, gmix, wmain, wtail, gcq, gckv, wuqT, wuk, wuvT, gqcol, gk, invf, bg)


def _conv_kernel(cur_ref, prev_ref, next_ref, w_ref, b_ref, qk_ref, kT_ref, *, tiles_per_seq):
    t = pl.program_id(0) % tiles_per_seq
    tm = cur_ref.shape[0]
    prev = jnp.where(t == 0, 0.0, prev_ref[...])
    nxt = jnp.where(t == tiles_per_seq - 1, 0.0, next_ref[...])
    cw = w_ref[...]
    rb = CONV_ROWS
    for blk in range(tm // rb):
        lo = blk * rb
        top = prev if blk == 0 else cur_ref[lo - 8:lo, :]
        bot = nxt if lo + rb == tm else cur_ref[lo + rb:lo + rb + 8, :]
        ext = jnp.concatenate([top, cur_ref[lo:lo + rb, :], bot], axis=0)
        u = b_ref[...]
        for j in range(M_CONV):
            shifted = (ext if j == M_CONV // 2
                       else pltpu.roll(ext, (M_CONV // 2 - j) % (rb + 16), axis=0))
            u = u + cw[j:j + 1, :] * shifted[8:8 + rb, :]
        qk = u * jax.nn.sigmoid(u)
        k = qk[:, 256:512]
        qk_ref[lo:lo + rb, 0:256] = (qk[:, 0:256] * (M_DQK ** -0.5)).astype(BF16)
        qk_ref[lo:lo + rb, 256:512] = k.astype(BF16)
        kT_ref[0, :, lo:lo + rb] = k.T.astype(BF16)


def _conv_call(qk, convw, convb, batch, seq):
    n = qk.shape[0]
    tm = ROW_TILE
    nt = seq // tm
    r8 = tm // 8
    return pl.pallas_call(
        functools.partial(_conv_kernel, tiles_per_seq=nt),
        grid=(n // tm,),
        in_specs=[
            pl.BlockSpec((tm, 512), lambda i: (i, 0)),
            pl.BlockSpec((8, 512), lambda i: (jnp.maximum(i * r8 - 1, 0), 0)),
            pl.BlockSpec((8, 512), lambda i: (jnp.minimum((i + 1) * r8, n // 8 - 1), 0)),
            pl.BlockSpec(convw.shape, lambda i: (0, 0)),
            pl.BlockSpec(convb.shape, lambda i: (0, 0)),
        ],
        out_specs=[pl.BlockSpec((tm, 512), lambda i: (i, 0)),
                   pl.BlockSpec((1, 256, tm), lambda i: (i // nt, 0, i % nt))],
        out_shape=[jax.ShapeDtypeStruct((n, 512), BF16),
                   jax.ShapeDtypeStruct((batch, 256, seq), BF16)],
        compiler_params=pltpu.CompilerParams(dimension_semantics=("arbitrary",),
                                             vmem_limit_bytes=VMEM_LIMIT),
        name="conv",
    )(qk, qk, qk, convw, convb)


def _mlstm_kernel(qkf_ref, kTf_ref, vf_ref, sf_ref, b3f_ref, qkb_ref, kTb_ref, vb_ref, sb_ref,
                  b3b_ref, esel_ref, hf_ref, hb_ref, cf_sc, cb_sc, mf_sc, mb_sc):
    L = M_CHUNK
    n_sub = qkf_ref.shape[1] // L

    @pl.when(pl.program_id(1) == 0)
    def _():
        cf_sc[...] = jnp.zeros_like(cf_sc)
        cb_sc[...] = jnp.zeros_like(cb_sc)
        mf_sc[...] = jnp.zeros_like(mf_sc)
        mb_sc[...] = jnp.zeros_like(mb_sc)

    esel = esel_ref[...]
    row = lax.broadcasted_iota(jnp.int32, (L, L), 0)
    col = lax.broadcasted_iota(jnp.int32, (L, L), 1)
    ones_blk = jnp.ones((L, M_DV), BF16)
    fw = dict(qk=qkf_ref, kT=kTf_ref, v=vf_ref, s=sf_ref, b3=b3f_ref, out=hf_ref, c_sc=cf_sc,
              m_sc=mf_sc, visible=col <= row)
    bw = dict(qk=qkb_ref, kT=kTb_ref, v=vb_ref, s=sb_ref, b3=b3b_ref, out=hb_ref, c_sc=cb_sc,
              m_sc=mb_sc, visible=col >= row)

    work = []
    for d in (fw, bw):
        m = d["m_sc"][...]
        order = range(n_sub) if d is fw else range(n_sub - 1, -1, -1)
        for i in order:
            sl = slice(i * L, (i + 1) * L)
            stats = d["s"][0, :, sl]
            a, g = stats[16:24], stats[24:32]
            m_new = jnp.maximum(a + m, g)
            work.append((d, sl, dict(
                w=stats[0:8], r=stats[8:16], m_prev=m,
                dec=jnp.exp2(a + m - m_new), add=jnp.exp2(g - m_new),
                brep=_dot_tn(d["b3"][0, :, sl], esel))))
            m = m_new
        d["m_sc"][...] = m
    work = [work[j + k * n_sub] for j in range(n_sub) for k in range(2)]
    units = [(d, sl, gt, h) for d, sl, gt in work for h in range(M_HEADS)]

    grams, v_augs, s_augs = [], [], []
    for d, sl, gt, h in units:
        q_h = d["qk"][0, sl, h * M_DQK:(h + 1) * M_DQK]
        k_h = d["qk"][0, sl, 256 + h * M_DQK:256 + (h + 1) * M_DQK]
        grams.append(_dot_nt(q_h, k_h))
        v_augs.append(jnp.concatenate([d["v"][0, sl, h * M_DV:(h + 1) * M_DV], ones_blk], axis=1))
    for (d, sl, gt, h), v_aug in zip(units, v_augs):
        kTw = d["kT"][0, h * M_DQK:(h + 1) * M_DQK, sl].astype(F32) * gt["w"][h:h + 1, :]
        s_augs.append(_dot(kTw.astype(BF16), v_aug))

    states = {id(fw): [cf_sc[h] for h in range(M_HEADS)],
              id(bw): [cb_sc[h] for h in range(M_HEADS)]}
    for (d, sl, gt, h), gram, v_aug, s_aug in zip(units, grams, v_augs, s_augs):
        q_h = d["qk"][0, sl, h * M_DQK:(h + 1) * M_DQK]
        brep = gt["brep"][:, h * LANES:(h + 1) * LANES]
        bm = brep + gt["m_prev"][h:h + 1, :]
        dmat = jnp.where(d["visible"], brep + gt["r"][h:h + 1, :], -jnp.inf)
        m_t = jnp.maximum(bm, jnp.max(dmat, axis=1, keepdims=True))
        p = jnp.exp2(dmat - m_t) * gram
        inter = jnp.exp2(bm - m_t)
        c_prev = states[id(d)][h]
        lhs = jnp.concatenate(
            [p.astype(BF16), (inter[:, 0:M_DQK] * q_h.astype(F32)).astype(BF16)], axis=1)
        rhs = jnp.concatenate([v_aug, c_prev.astype(BF16)], axis=0)
        res = _dot(lhs, rhs)
        den = jnp.maximum(jnp.abs(res[:, M_DV:2 * M_DV]), jnp.exp2(-m_t))
        d["out"][0, sl, h * M_DV:(h + 1) * M_DV] = res[:, 0:M_DV] / den
        dec_h = jnp.concatenate([gt["dec"][h:h + 1, :]] * 2, axis=1)
        add_h = jnp.concatenate([gt["add"][h:h + 1, :]] * 2, axis=1)
        states[id(d)][h] = dec_h * c_prev + add_h * s_aug
    for h in range(M_HEADS):
        cf_sc[h] = states[id(fw)][h]
        cb_sc[h] = states[id(bw)][h]


def _mlstm_call(qk3, kT3, v3, sf, b3f, sb, b3b, esel):
    batch, seq, _ = qk3.shape
    L = M_CHUNK * MLSTM_CHUNKS_PER_STEP
    nc = seq // L
    fw_rows = lambda w: pl.BlockSpec((1, L, w), lambda b, c: (b, c, 0))
    bw_rows = lambda w: pl.BlockSpec((1, L, w), lambda b, c: (b, nc - 1 - c, 0))
    fw_lanes = lambda r: pl.BlockSpec((1, r, L), lambda b, c: (b, 0, c))
    bw_lanes = lambda r: pl.BlockSpec((1, r, L), lambda b, c: (b, 0, nc - 1 - c))
    state = pltpu.VMEM((M_HEADS, M_DQK, 2 * M_DV), F32)
    return pl.pallas_call(
        _mlstm_kernel,
        grid=(batch, nc),
        in_specs=[fw_rows(512), fw_lanes(256), fw_rows(512), fw_lanes(32), fw_lanes(32),
                  bw_rows(512), bw_lanes(256), bw_rows(512), bw_lanes(32), bw_lanes(32),
                  pl.BlockSpec(esel.shape, lambda b, c: (0, 0))],
        out_specs=[fw_rows(512), bw_rows(512)],
        out_shape=[jax.ShapeDtypeStruct((batch, seq, 512), F32)] * 2,
        scratch_shapes=[state, state, pltpu.VMEM((8, M_CHUNK), F32), pltpu.VMEM((8, M_CHUNK), F32)],
        compiler_params=pltpu.CompilerParams(dimension_semantics=("arbitrary", "arbitrary"),
                                             vmem_limit_bytes=VMEM_LIMIT),
        name="mlstm",
    )(qk3, kT3, v3, sf, b3f, qk3, kT3, v3, sb, b3b, esel)


def _attn_kernel(qT_ref, k_ref, vT_ref, o_ref, s_sc, cm_sc, acc_sc, *, n_kv):
    tk = KV_TILE
    sub = tk // TOKEN_TILE
    qT = jnp.concatenate([qT_ref[0, 0, i] for i in range(qT_ref.shape[2])], axis=1)
    tq = qT.shape[1]

    def scores(j, slot):
        sT = _dot(k_ref[0, 0, j * tk:(j + 1) * tk, :], qT)
        s_sc[slot] = sT
        cm_sc[slot] = jnp.max(sT, axis=0, keepdims=True)

    m = jnp.full((1, tq), -jnp.inf, F32)
    l = jnp.zeros((1, tq), F32)
    scores(0, 0)
    for j in range(n_kv):
        slot = j % 2
        if j + 1 < n_kv:
            scores(j + 1, 1 - slot)
        m_new = jnp.maximum(m, cm_sc[slot])
        alpha = jnp.exp2(m - m_new)
        p = jnp.exp2(s_sc[slot] - m_new)
        l = alpha * l + jnp.sum(p, axis=0, keepdims=True)
        vt = jnp.concatenate([vT_ref[0, 0, j * sub + i] for i in range(sub)], axis=1)
        pv = _dot(vt, p.astype(BF16))
        if j == 0:
            acc_sc[...] = pv
        else:
            acc_sc[...] = alpha * acc_sc[...] + pv
        m = m_new
    o_ref[0] = (acc_sc[...] / l).T.astype(o_ref.dtype)


def _attn_bounded_kernel(qT_ref, k_ref, vT_ref, o_ref, acc_sc, *, n_kv):
    tk = KV_TILE
    sub = tk // TOKEN_TILE
    qT = jnp.concatenate([qT_ref[0, 0, i] for i in range(qT_ref.shape[2])], axis=1)
    l = jnp.zeros((1, qT.shape[1]), F32)
    for j in range(n_kv):
        p = jnp.exp2(_dot(k_ref[0, 0, j * tk:(j + 1) * tk, :], qT))
        l = l + jnp.sum(p, axis=0, keepdims=True)
        vt = jnp.concatenate([vT_ref[0, 0, j * sub + i] for i in range(sub)], axis=1)
        pv = _dot(vt, p.astype(BF16))
        if j == 0:
            acc_sc[...] = pv
        else:
            acc_sc[...] += pv
    o_ref[0] = (acc_sc[...] / l).T.astype(o_ref.dtype)


def _attn_call(qT, k, vT, score_bound_log2):
    batch, heads, nt, _, tm = qT.shape
    seq = k.shape[2]
    tq = Q_TILE
    n_kv = seq // KV_TILE
    slabs = tq // tm

    def body(safe_ref, qT_ref, k_ref, vT_ref, o_ref, s_sc, cm_sc, acc_sc):
        @pl.when(safe_ref[0] != 0)
        def _():
            _attn_bounded_kernel(qT_ref, k_ref, vT_ref, o_ref, acc_sc, n_kv=n_kv)

        @pl.when(safe_ref[0] == 0)
        def _():
            _attn_kernel(qT_ref, k_ref, vT_ref, o_ref, s_sc, cm_sc, acc_sc, n_kv=n_kv)

    safe = (score_bound_log2 <= SAFE_SCORE_LOG2).astype(jnp.int32).reshape(1)
    return pl.pallas_call(
        body,
        grid_spec=pltpu.PrefetchScalarGridSpec(
            num_scalar_prefetch=1,
            grid=(batch, heads, seq // tq),
            in_specs=[
                pl.BlockSpec((1, 1, slabs, A_DH, tm), lambda b, h, i, safe: (b, h, i, 0, 0)),
                pl.BlockSpec((1, 1, seq, A_DH), lambda b, h, i, safe: (b, h, 0, 0)),
                pl.BlockSpec((1, 1, nt, A_DV, tm), lambda b, h, i, safe: (b, h, 0, 0, 0)),
            ],
            out_specs=pl.BlockSpec((1, tq, A_DV), lambda b, h, i, safe: (b, i, h)),
            scratch_shapes=[pltpu.VMEM((2, KV_TILE, tq), F32), pltpu.VMEM((2, 1, tq), F32),
                            pltpu.VMEM((A_DV, tq), F32)]),
        out_shape=jax.ShapeDtypeStruct((batch, seq, heads * A_DV), BF16),
        compiler_params=pltpu.CompilerParams(
            dimension_semantics=("arbitrary", "arbitrary", "arbitrary"),
            vmem_limit_bytes=VMEM_LIMIT),
        name="attn",
    )(safe, qT, k, vT)


def _out_kernel(x_ref, hf_ref, hb_ref, o_ref, yb_ref, gml_ref, wout_ref, gffn_ref, w1_ref, w2_ref,
                y_ref):
    x = x_ref[...]
    hsum = hf_ref[...] + hb_ref[...]
    gate = jax.nn.sigmoid(o_ref[...])
    gml = gml_ref[...]
    parts = []
    for h in range(M_HEADS):
        sl = slice(h * M_DV, (h + 1) * M_DV)
        parts.append((gate[:, sl] * _rms(hsum[:, sl], gml[:, sl])).astype(BF16))
    y_a = jnp.concatenate(parts, axis=1)
    n_a = y_a.shape[1]
    x1 = (x + _dot(y_a, wout_ref[0:n_a, :].astype(BF16))
          + _dot(yb_ref[...], wout_ref[n_a:, :].astype(BF16)))
    hn = _rms(x1, gffn_ref[...]).astype(BF16)
    y_ref[...] = x1
    d_ff = w1_ref.shape[1]
    for j in range(d_ff // FF_CHUNK):
        sl = slice(j * FF_CHUNK, (j + 1) * FF_CHUNK)
        t = jnp.maximum(_dot(hn, w1_ref[:, sl].astype(BF16)), 0.0)
        y_ref[...] += _dot((t * t).astype(BF16), w2_ref[sl, :].astype(BF16))


def _out_call(xf, hf, hb, o, yb, gml, wout, gffn, w1, w2):
    n, d = xf.shape
    tm = ROW_TILE

    def full(a):
        return pl.BlockSpec(a.shape, lambda i: (0,) * a.ndim)

    tok = lambda w: pl.BlockSpec((tm, w), lambda i: (i, 0))
    in_specs = [tok(d), tok(512), tok(512), tok(512), tok(512), full(gml),
                pl.BlockSpec(wout.shape, lambda i: (0, 0), pipeline_mode=pl.Buffered(1)), full(gffn),
                pl.BlockSpec(w1.shape, lambda i: (0, 0), pipeline_mode=pl.Buffered(1)),
                pl.BlockSpec(w2.shape, lambda i: (0, 0), pipeline_mode=pl.Buffered(1))]
    return pl.pallas_call(
        _out_kernel, grid=(n // tm,), in_specs=in_specs, out_specs=tok(d),
        out_shape=jax.ShapeDtypeStruct((n, d), F32),
        compiler_params=pltpu.CompilerParams(dimension_semantics=("arbitrary",),
                                             vmem_limit_bytes=VMEM_LIMIT),
        name="out",
    )(xf, hf, hb, o, yb, gml, wout, gffn, w1, w2)


def _layer(xf, pos3, invf, batch, seq, g_mix_norm, w_in, conv_w, conv_b, b_gates, g_mlstm_out,
           g_cq, g_ckv, w_uq, w_ukv, g_q, g_k, w_out, g_ffn_norm, w_ff1, w_ff2):
    d = xf.shape[1]
    c0 = 2 * 256 + 2 * 512
    wmain = w_in
    wt = w_in[:, c0:].astype(BF16)
    gates_w = wt[:, 0:N_GATES]
    cq_w = wt[:, N_GATES:N_GATES + 256]
    ckv_w = wt[:, N_GATES + 256:N_GATES + 384]
    kpe_w = wt[:, N_GATES + 384:N_GATES + 448]
    wtail = jnp.concatenate([cq_w, ckv_w, kpe_w, kpe_w[:, 0:32], gates_w,
                             jnp.zeros((d, LANES - 96 - N_GATES), BF16)], axis=1)
    wuqT = w_uq.T.astype(BF16)
    ukv = w_ukv.reshape(w_ukv.shape[0], A_HEADS, A_NOPE + A_DV)
    wuk = ukv[:, :, :A_NOPE].reshape(w_ukv.shape[0], A_HEADS * A_NOPE).astype(BF16)
    wuvT = ukv[:, :, A_NOPE:].reshape(w_ukv.shape[0], A_HEADS * A_DV).T.astype(BF16)
    gqcol = g_q.reshape(A_DH, 1)
    gk = jnp.concatenate([g_k, g_k[A_NOPE + 32:], g_k[A_NOPE:A_NOPE + 32]]).reshape(1, -1)

    qk, v, o, sf, b3f, sb, b3b, qT, k, vT = _proj_call(
        xf, pos3, g_mix_norm.reshape(1, -1), wmain, wtail, g_cq.reshape(1, -1),
        g_ckv.reshape(1, -1), wuqT, wuk, wuvT, gqcol, gk, invf, b_gates.reshape(2, 8, 1), batch, seq)

    qkc, kT = _conv_call(qk, conv_w, conv_b.reshape(1, -1), batch, seq)
    esel = jnp.tile(jnp.repeat(jnp.eye(8, M_HEADS, dtype=BF16), LANES, axis=1), (4, 1))
    hf, hb = _mlstm_call(qkc.reshape(batch, seq, 512), kT, v.reshape(batch, seq, 512),
                         sf, b3f, sb, b3b, esel)
    score_bound_log2 = (A_DH ** 0.5 * LOG2E * 1.01) * jnp.max(jnp.abs(g_q)) * jnp.max(jnp.abs(g_k))
    yb = _attn_call(qT, k, vT, score_bound_log2)
    return _out_call(xf, hf.reshape(batch * seq, -1), hb.reshape(batch * seq, -1), o,
                     yb.reshape(batch * seq, -1), g_mlstm_out.reshape(1, -1), w_out,
                     g_ffn_norm.reshape(1, -1), w_ff1, w_ff2)


def kernel(x, positions, g_mix_norm, w_in, conv_w, conv_b, b_gates, g_mlstm_out, g_cq, g_ckv,
           w_uq, w_ukv, g_q, g_k, w_out, g_ffn_norm, w_ff1, w_ff2):
    batch, seq, d = x.shape
    assert seq % TOKEN_TILE == 0 and seq % M_CHUNK == 0
    inv_freq = ROPE_THETA ** (-jnp.arange(0, A_ROPE, 2, dtype=F32) / A_ROPE)
    invf = inv_freq.reshape(A_ROPE // 2, 1)
    pos3 = positions.reshape(batch * seq // TOKEN_TILE, 1, TOKEN_TILE)
    xf = x.reshape(batch * seq, d)
    for l in range(w_in.shape[0]):
        xf = _layer(xf, pos3, invf, batch, seq, g_mix_norm[l], w_in[l], conv_w[l], conv_b[l],
                    b_gates[l], g_mlstm_out[l], g_cq[l], g_ckv[l], w_uq[l], w_ukv[l], g_q[l],
                    g_k[l], w_out[l], g_ffn_norm[l], w_ff1[l], w_ff2[l])
    return xf.reshape(batch, seq, d)
```

```python
import functools
import math

import jax
import jax.numpy as jnp
from jax import lax
from jax.experimental import pallas as pl
from jax.experimental.pallas import tpu as pltpu

F32 = jnp.float32
BF16 = jnp.bfloat16

EPS = 1e-6
ROPE_THETA = 10000.0
M_HEADS = 4
M_DQK = 64
M_DV = 128
M_CONV = 5
M_CHUNK = 128
A_HEADS = 4
A_NOPE = 128
A_ROPE = 64
A_DV = 128
A_DH = A_NOPE + A_ROPE
N_GATES = 4 * M_HEADS

LANES = 128
TOKEN_TILE = 1024
ROW_TILE = 512
CONV_ROWS = 128
KV_TILE = 1024
Q_TILE = 1024
PROJ_PARTS = 4
MLSTM_CHUNKS_PER_STEP = 8
SAFE_SCORE_LOG2 = 64.0
FF_CHUNK = 1024
VMEM_LIMIT = 56 * 1024 * 1024

LOG2E = math.log2(math.e)


def _dot(a, b):
    return jnp.dot(a, b, preferred_element_type=F32)


def _dot_nt(a, b):
    return lax.dot_general(a, b, (((1,), (1,)), ((), ())), preferred_element_type=F32)


def _dot_tn(a, b):
    return lax.dot_general(a, b, (((0,), (0,)), ((), ())), preferred_element_type=F32)


def _rms(x, g):
    ms = jnp.mean(x * x, axis=-1, keepdims=True)
    return x * lax.rsqrt(ms + EPS) * g


def _split3(y):
    hi = y.astype(BF16).astype(F32)
    mid = (y - hi).astype(BF16).astype(F32)
    lo = y - hi - mid
    return jnp.concatenate([hi, mid, lo], axis=0).astype(BF16)


def _chunk_decay(x, *, backward):
    L = M_CHUNK
    row = lax.broadcasted_iota(jnp.int32, (L, L), 0)
    col = lax.broadcasted_iota(jnp.int32, (L, L), 1)
    umat = jnp.where((row >= col) if backward else (row <= col), 1.0, 0.0).astype(BF16)
    lf = jax.nn.log_sigmoid(x)
    cum3 = _dot(_split3(lf), umat)
    cum = cum3[0:8] + cum3[8:16] + cum3[16:24]
    a = jnp.broadcast_to(jnp.sum(lf, axis=1, keepdims=True), (8, L))
    b = pltpu.roll(cum, 4, axis=0)
    a = pltpu.roll(a, 4, axis=0)
    w_log = a - b + x
    g = jnp.broadcast_to(jnp.max(w_log, axis=1, keepdims=True), (8, L))
    stats = jnp.concatenate([jnp.exp(w_log - g), (x - b) * LOG2E, a * LOG2E, g * LOG2E], axis=0)
    b3 = jnp.concatenate([_split3(b * LOG2E), jnp.zeros((8, L), BF16)], axis=0)
    return stats, b3


def _proj_kernel(x_ref, xprev_ref, xnext_ref, pos_ref, gmix_ref, wmain_ref, wtail_ref, gcq_ref,
                 gckv_ref, wuqT_ref, wuk_ref, wuvT_ref, gqcol_ref, gk_ref, invf_ref, bg_ref,
                 cw_ref, cb_ref,
                 qk_ref, v_ref, o_ref, sf_ref, b3f_ref, sb_ref, b3b_ref, qT_ref, k_ref, vT_ref,
                 kT_ref, *, tiles_per_seq):
    tm = x_ref.shape[0]
    parts = [slice(i * tm // PROJ_PARTS, (i + 1) * tm // PROJ_PARTS) for i in range(PROJ_PARTS)]

    t = pl.program_id(0) % tiles_per_seq
    x_halo = jnp.concatenate([xprev_ref[...], xnext_ref[...]], axis=0)
    z_halo = _dot(_rms(x_halo, gmix_ref[...]).astype(BF16), wmain_ref[:, 0:512])
    top_halo = jnp.where(t == 0, 0.0, z_halo[0:8])
    bot_halo = jnp.where(t == tiles_per_seq - 1, 0.0, z_halo[8:16])
    zqks = []

    def project(rows):
        hn = _rms(x_ref[rows, :], gmix_ref[...]).astype(BF16)
        z = _dot(hn, wmain_ref[...])
        zqks.append(z[:, 0:512])
        v_ref[rows, :] = z[:, 512:1024].astype(BF16)
        o_ref[rows, :] = z[:, 1024:1536]
        return _dot(hn, wtail_ref[...])

    def conv(i, rows):
        top = top_halo if i == 0 else zqks[i - 1][-8:, :]
        bot = bot_halo if i == PROJ_PARTS - 1 else zqks[i + 1][0:8, :]
        ext_part = jnp.concatenate([top, zqks[i], bot], axis=0)
        cw = cw_ref[...]
        rb = CONV_ROWS
        for blk in range((rows.stop - rows.start) // rb):
            ext = ext_part[blk * rb:blk * rb + rb + 16, :]
            u = cb_ref[...]
            for j in range(M_CONV):
                shifted = (ext if j == M_CONV // 2
                           else pltpu.roll(ext, (M_CONV // 2 - j) % (rb + 16), axis=0))
                u = u + cw[j:j + 1, :] * shifted[8:8 + rb, :]
            qk = u * jax.nn.sigmoid(u)
            k = qk[:, 256:512]
            dst = slice(rows.start + blk * rb, rows.start + (blk + 1) * rb)
            qk_ref[dst, 0:256] = (qk[:, 0:256] * (M_DQK ** -0.5)).astype(BF16)
            qk_ref[dst, 256:512] = k.astype(BF16)
            kT_ref[0, :, dst] = k.T.astype(BF16)

    def latents(zt, rows):
        cq = zt[:, 0:256]
        ckv = zt[:, 256:384]
        tail = zt[:, 384:512]

        tailT = tail.T
        for d, (s_ref, b3_ref) in enumerate(((sf_ref, b3f_ref), (sb_ref, b3b_ref))):
            gx = tailT[96 + 8 * d:104 + 8 * d] + bg_ref[d]
            for j in range(gx.shape[1] // M_CHUNK):
                sl = slice(j * M_CHUNK, (j + 1) * M_CHUNK)
                dst = slice(rows.start + sl.start, rows.start + sl.stop)
                stats, b3 = _chunk_decay(gx[:, sl], backward=d == 1)
                s_ref[0, :, dst] = stats
                b3_ref[0, :, dst] = b3

        pos = pos_ref[0][:, rows].astype(F32)
        angT = invf_ref[...] * pos
        cosT = jnp.cos(angT)
        sinT = jnp.sin(angT)

        cqnT = _rms(cq, gcq_ref[...]).T.astype(BF16)
        qT = _dot(wuqT_ref[...], cqnT)
        gq = gqcol_ref[...] * (A_DH ** -0.5 * LOG2E)
        for h in range(A_HEADS):
            qh = qT[h * A_DH:(h + 1) * A_DH]
            ms = jnp.sum(qh * qh, axis=0, keepdims=True) * (1.0 / A_DH)
            qn = qh * lax.rsqrt(ms + EPS) * gq
            x1 = qn[A_NOPE:A_NOPE + 32]
            x2 = qn[A_NOPE + 32:A_DH]
            qT_ref[0, h, 0, 0:A_NOPE, rows] = qn[0:A_NOPE].astype(BF16)
            qT_ref[0, h, 0, A_NOPE:A_NOPE + 32, rows] = (x1 * cosT - x2 * sinT).astype(BF16)
            qT_ref[0, h, 0, A_NOPE + 32:A_DH, rows] = (x2 * cosT + x1 * sinT).astype(BF16)

        ckvn = _rms(ckv, gckv_ref[...])
        knope = _dot(ckvn.astype(BF16), wuk_ref[...])
        vT = _dot(wuvT_ref[...], ckvn.T.astype(BF16))
        for h in range(A_HEADS):
            vT_ref[0, h, 0, :, rows] = vT[h * A_DV:(h + 1) * A_DV].astype(BF16)

        tab = jnp.concatenate([cosT, cosT, -sinT, sinT], axis=0).T
        cc = tab[:, 0:A_ROPE]
        ss = tab[:, A_ROPE:2 * A_ROPE]
        gk = gk_ref[...]
        kpe = tail[:, 0:A_ROPE]
        kpe_sw = pltpu.roll(tail, LANES - 32, axis=1)[:, 0:A_ROPE]
        ss_pe = jnp.sum(kpe * kpe, axis=-1, keepdims=True)
        krope = (kpe * gk[:, A_NOPE:A_DH] * cc + kpe_sw * gk[:, A_DH:A_DH + A_ROPE] * ss)
        for h in range(A_HEADS):
            kn = knope[:, h * A_NOPE:(h + 1) * A_NOPE]
            ms = (jnp.sum(kn * kn, axis=-1, keepdims=True) + ss_pe) * (1.0 / A_DH)
            r = lax.rsqrt(ms + EPS)
            k_ref[0, h, rows, 0:A_NOPE] = (kn * r * gk[:, 0:A_NOPE]).astype(BF16)
            k_ref[0, h, rows, A_NOPE:A_DH] = (krope * r).astype(BF16)

    zt = project(parts[0])
    for i, rows in enumerate(parts):
        zt_next = project(parts[i + 1]) if i + 1 < len(parts) else None
        latents(zt, rows)
        conv(i, rows)
        zt = zt_next


def _proj_call(xf, pos3, gmix, wmain, wtail, gcq, gckv, wuqT, wuk, wuvT, gqcol, gk, invf, bg,
               convw, convb, batch, seq):
    n, d = xf.shape
    tm = TOKEN_TILE
    nt = seq // tm
    steps = n // tm
    r8 = tm // 8

    def full(a):
        return pl.BlockSpec(a.shape, lambda i: (0,) * a.ndim)

    tok = lambda w: pl.BlockSpec((tm, w), lambda i: (i, 0))
    in_specs = [tok(d),
                pl.BlockSpec((8, d), lambda i: (jnp.maximum(i * r8 - 1, 0), 0)),
                pl.BlockSpec((8, d), lambda i: (jnp.minimum((i + 1) * r8, n // 8 - 1), 0)),
                pl.BlockSpec((1, 1, tm), lambda i: (i, 0, 0)), full(gmix), full(wmain),
                full(wtail), full(gcq), full(gckv), full(wuqT), full(wuk), full(wuvT), full(gqcol),
                full(gk), full(invf), full(bg), full(convw), full(convb)]
    stat_spec = pl.BlockSpec((1, 32, tm), lambda i: (i // nt, 0, i % nt))
    out_shape = [
        jax.ShapeDtypeStruct((n, 512), BF16),
        jax.ShapeDtypeStruct((n, 512), BF16),
        jax.ShapeDtypeStruct((n, 512), F32),
        jax.ShapeDtypeStruct((batch, 32, seq), F32),
        jax.ShapeDtypeStruct((batch, 32, seq), BF16),
        jax.ShapeDtypeStruct((batch, 32, seq), F32),
        jax.ShapeDtypeStruct((batch, 32, seq), BF16),
        jax.ShapeDtypeStruct((batch, A_HEADS, nt, A_DH, tm), BF16),
        jax.ShapeDtypeStruct((batch, A_HEADS, seq, A_DH), BF16),
        jax.ShapeDtypeStruct((batch, A_HEADS, nt, A_DV, tm), BF16),
        jax.ShapeDtypeStruct((batch, 256, seq), BF16),
    ]
    out_specs = [
        tok(512), tok(512), tok(512), stat_spec, stat_spec, stat_spec, stat_spec,
        pl.BlockSpec((1, A_HEADS, 1, A_DH, tm), lambda i: (i // nt, 0, i % nt, 0, 0)),
        pl.BlockSpec((1, A_HEADS, tm, A_DH), lambda i: (i // nt, 0, i % nt, 0)),
        pl.BlockSpec((1, A_HEADS, 1, A_DV, tm), lambda i: (i // nt, 0, i % nt, 0, 0)),
        pl.BlockSpec((1, 256, tm), lambda i: (i // nt, 0, i % nt)),
    ]
    return pl.pallas_call(
        functools.partial(_proj_kernel, tiles_per_seq=nt), grid=(steps,), in_specs=in_specs,
        out_specs=out_specs, out_shape=out_shape,
        compiler_params=pltpu.CompilerParams(dimension_semantics=("arbitrary",),
                                             vmem_limit_bytes=VMEM_LIMIT),
        name="proj",
    )(xf, xf, xf, pos3, gmix, wmain, wtail, gcq, gckv, wuqT, wuk, wuvT, gqcol, gk, invf, bg,
      convw, convb)


def _mlstm_kernel(qkf_ref, kTf_ref, vf_ref, sf_ref, b3f_ref, qkb_ref, kTb_ref, vb_ref, sb_ref,
                  b3b_ref, esel_ref, hf_ref, hb_ref, cf_sc, cb_sc, mf_sc, mb_sc):
    L = M_CHUNK
    n_sub = qkf_ref.shape[1] // L

    @pl.when(pl.program_id(1) == 0)
    def _():
        cf_sc[...] = jnp.zeros_like(cf_sc)
        cb_sc[...] = jnp.zeros_like(cb_sc)
        mf_sc[...] = jnp.zeros_like(mf_sc)
        mb_sc[...] = jnp.zeros_like(mb_sc)

    esel = esel_ref[...]
    row = lax.broadcasted_iota(jnp.int32, (L, L), 0)
    col = lax.broadcasted_iota(jnp.int32, (L, L), 1)
    ones_blk = jnp.ones((L, M_DV), BF16)
    fw = dict(qk=qkf_ref, kT=kTf_ref, v=vf_ref, s=sf_ref, b3=b3f_ref, out=hf_ref, c_sc=cf_sc,
              m_sc=mf_sc, visible=col <= row)
    bw = dict(qk=qkb_ref, kT=kTb_ref, v=vb_ref, s=sb_ref, b3=b3b_ref, out=hb_ref, c_sc=cb_sc,
              m_sc=mb_sc, visible=col >= row)

    work = []
    for d in (fw, bw):
        m = d["m_sc"][...]
        order = range(n_sub) if d is fw else range(n_sub - 1, -1, -1)
        for i in order:
            sl = slice(i * L, (i + 1) * L)
            stats = d["s"][0, :, sl]
            a, g = stats[16:24], stats[24:32]
            m_new = jnp.maximum(a + m, g)
            work.append((d, sl, dict(
                w=stats[0:8], r=stats[8:16], m_prev=m,
                dec=jnp.exp2(a + m - m_new), add=jnp.exp2(g - m_new),
                brep=_dot_tn(d["b3"][0, :, sl], esel))))
            m = m_new
        d["m_sc"][...] = m
    work = [work[j + k * n_sub] for j in range(n_sub) for k in range(2)]
    units = [(d, sl, gt, h) for d, sl, gt in work for h in range(M_HEADS)]

    grams, v_augs, s_augs = [], [], []
    for d, sl, gt, h in units:
        q_h = d["qk"][0, sl, h * M_DQK:(h + 1) * M_DQK]
        k_h = d["qk"][0, sl, 256 + h * M_DQK:256 + (h + 1) * M_DQK]
        grams.append(_dot_nt(q_h, k_h))
        v_augs.append(jnp.concatenate([d["v"][0, sl, h * M_DV:(h + 1) * M_DV], ones_blk], axis=1))
    for (d, sl, gt, h), v_aug in zip(units, v_augs):
        kTw = d["kT"][0, h * M_DQK:(h + 1) * M_DQK, sl].astype(F32) * gt["w"][h:h + 1, :]
        s_augs.append(_dot(kTw.astype(BF16), v_aug))

    states = {id(fw): [cf_sc[h] for h in range(M_HEADS)],
              id(bw): [cb_sc[h] for h in range(M_HEADS)]}
    for (d, sl, gt, h), gram, v_aug, s_aug in zip(units, grams, v_augs, s_augs):
        q_h = d["qk"][0, sl, h * M_DQK:(h + 1) * M_DQK]
        brep = gt["brep"][:, h * LANES:(h + 1) * LANES]
        bm = brep + gt["m_prev"][h:h + 1, :]
        dmat = jnp.where(d["visible"], brep + gt["r"][h:h + 1, :], -jnp.inf)
        m_t = jnp.maximum(bm, jnp.max(dmat, axis=1, keepdims=True))
        p = jnp.exp2(dmat - m_t) * gram
        inter = jnp.exp2(bm - m_t)
        c_prev = states[id(d)][h]
        lhs = jnp.concatenate(
            [p.astype(BF16), (inter[:, 0:M_DQK] * q_h.astype(F32)).astype(BF16)], axis=1)
        rhs = jnp.concatenate([v_aug, c_prev.astype(BF16)], axis=0)
        res = _dot(lhs, rhs)
        den = jnp.maximum(jnp.abs(res[:, M_DV:2 * M_DV]), jnp.exp2(-m_t))
        d["out"][0, sl, h * M_DV:(h + 1) * M_DV] = res[:, 0:M_DV] / den
        dec_h = jnp.concatenate([gt["dec"][h:h + 1, :]] * 2, axis=1)
        add_h = jnp.concatenate([gt["add"][h:h + 1, :]] * 2, axis=1)
        states[id(d)][h] = dec_h * c_prev + add_h * s_aug
    for h in range(M_HEADS):
        cf_sc[h] = states[id(fw)][h]
        cb_sc[h] = states[id(bw)][h]


def _mlstm_call(qk3, kT3, v3, sf, b3f, sb, b3b, esel):
    batch, seq, _ = qk3.shape
    L = M_CHUNK * MLSTM_CHUNKS_PER_STEP
    nc = seq // L
    fw_rows = lambda w: pl.BlockSpec((1, L, w), lambda b, c: (b, c, 0))
    bw_rows = lambda w: pl.BlockSpec((1, L, w), lambda b, c: (b, nc - 1 - c, 0))
    fw_lanes = lambda r: pl.BlockSpec((1, r, L), lambda b, c: (b, 0, c))
    bw_lanes = lambda r: pl.BlockSpec((1, r, L), lambda b, c: (b, 0, nc - 1 - c))
    state = pltpu.VMEM((M_HEADS, M_DQK, 2 * M_DV), F32)
    return pl.pallas_call(
        _mlstm_kernel,
        grid=(batch, nc),
        in_specs=[fw_rows(512), fw_lanes(256), fw_rows(512), fw_lanes(32), fw_lanes(32),
                  bw_rows(512), bw_lanes(256), bw_rows(512), bw_lanes(32), bw_lanes(32),
                  pl.BlockSpec(esel.shape, lambda b, c: (0, 0))],
        out_specs=[fw_rows(512), bw_rows(512)],
        out_shape=[jax.ShapeDtypeStruct((batch, seq, 512), F32)] * 2,
        scratch_shapes=[state, state, pltpu.VMEM((8, M_CHUNK), F32), pltpu.VMEM((8, M_CHUNK), F32)],
        compiler_params=pltpu.CompilerParams(dimension_semantics=("arbitrary", "arbitrary"),
                                             vmem_limit_bytes=VMEM_LIMIT),
        name="mlstm",
    )(qk3, kT3, v3, sf, b3f, qk3, kT3, v3, sb, b3b, esel)


def _attn_kernel(qT_ref, k_ref, vT_ref, o_ref, s_sc, cm_sc, acc_sc, *, n_kv):
    tk = KV_TILE
    sub = tk // TOKEN_TILE
    qT = jnp.concatenate([qT_ref[0, 0, i] for i in range(qT_ref.shape[2])], axis=1)
    tq = qT.shape[1]

    def scores(j, slot):
        sT = _dot(k_ref[0, 0, j * tk:(j + 1) * tk, :], qT)
        s_sc[slot] = sT
        cm_sc[slot] = jnp.max(sT, axis=0, keepdims=True)

    m = jnp.full((1, tq), -jnp.inf, F32)
    l = jnp.zeros((1, tq), F32)
    scores(0, 0)
    for j in range(n_kv):
        slot = j % 2
        if j + 1 < n_kv:
            scores(j + 1, 1 - slot)
        m_new = jnp.maximum(m, cm_sc[slot])
        alpha = jnp.exp2(m - m_new)
        p = jnp.exp2(s_sc[slot] - m_new)
        l = alpha * l + jnp.sum(p, axis=0, keepdims=True)
        vt = jnp.concatenate([vT_ref[0, 0, j * sub + i] for i in range(sub)], axis=1)
        pv = _dot(vt, p.astype(BF16))
        if j == 0:
            acc_sc[...] = pv
        else:
            acc_sc[...] = alpha * acc_sc[...] + pv
        m = m_new
    o_ref[0] = (acc_sc[...] / l).T.astype(o_ref.dtype)


def _attn_bounded_kernel(qT_ref, k_ref, vT_ref, o_ref, acc_sc, *, n_kv):
    tk = KV_TILE
    sub = tk // TOKEN_TILE
    qT = jnp.concatenate([qT_ref[0, 0, i] for i in range(qT_ref.shape[2])], axis=1)
    l = jnp.zeros((1, qT.shape[1]), F32)
    for j in range(n_kv):
        p = jnp.exp2(_dot(k_ref[0, 0, j * tk:(j + 1) * tk, :], qT))
        l = l + jnp.sum(p, axis=0, keepdims=True)
        vt = jnp.concatenate([vT_ref[0, 0, j * sub + i] for i in range(sub)], axis=1)
        pv = _dot(vt, p.astype(BF16))
        if j == 0:
            acc_sc[...] = pv
        else:
            acc_sc[...] += pv
    o_ref[0] = (acc_sc[...] / l).T.astype(o_ref.dtype)


def _attn_call(qT, k, vT, score_bound_log2):
    batch, heads, nt, _, tm = qT.shape
    seq = k.shape[2]
    tq = Q_TILE
    n_kv = seq // KV_TILE
    slabs = tq // tm

    def body(safe_ref, qT_ref, k_ref, vT_ref, o_ref, s_sc, cm_sc, acc_sc):
        @pl.when(safe_ref[0] != 0)
        def _():
            _attn_bounded_kernel(qT_ref, k_ref, vT_ref, o_ref, acc_sc, n_kv=n_kv)

        @pl.when(safe_ref[0] == 0)
        def _():
            _attn_kernel(qT_ref, k_ref, vT_ref, o_ref, s_sc, cm_sc, acc_sc, n_kv=n_kv)

    safe = (score_bound_log2 <= SAFE_SCORE_LOG2).astype(jnp.int32).reshape(1)
    return pl.pallas_call(
        body,
        grid_spec=pltpu.PrefetchScalarGridSpec(
            num_scalar_prefetch=1,
            grid=(batch, heads, seq // tq),
            in_specs=[
                pl.BlockSpec((1, 1, slabs, A_DH, tm), lambda b, h, i, safe: (b, h, i, 0, 0)),
                pl.BlockSpec((1, 1, seq, A_DH), lambda b, h, i, safe: (b, h, 0, 0)),
                pl.BlockSpec((1, 1, nt, A_DV, tm), lambda b, h, i, safe: (b, h, 0, 0, 0)),
            ],
            out_specs=pl.BlockSpec((1, tq, A_DV), lambda b, h, i, safe: (b, i, h)),
            scratch_shapes=[pltpu.VMEM((2, KV_TILE, tq), F32), pltpu.VMEM((2, 1, tq), F32),
                            pltpu.VMEM((A_DV, tq), F32)]),
        out_shape=jax.ShapeDtypeStruct((batch, seq, heads * A_DV), BF16),
        compiler_params=pltpu.CompilerParams(
            dimension_semantics=("arbitrary", "arbitrary", "arbitrary"),
            vmem_limit_bytes=VMEM_LIMIT),
        name="attn",
    )(safe, qT, k, vT)


def _out_kernel(x_ref, hf_ref, hb_ref, o_ref, yb_ref, gml_ref, wout_ref, gffn_ref, w1_ref, w2_ref,
                y_ref):
    x = x_ref[...]
    hsum = hf_ref[...] + hb_ref[...]
    gate = jax.nn.sigmoid(o_ref[...])
    gml = gml_ref[...]
    parts = []
    for h in range(M_HEADS):
        sl = slice(h * M_DV, (h + 1) * M_DV)
        parts.append((gate[:, sl] * _rms(hsum[:, sl], gml[:, sl])).astype(BF16))
    y_a = jnp.concatenate(parts, axis=1)
    n_a = y_a.shape[1]
    x1 = x + _dot(y_a, wout_ref[0:n_a, :]) + _dot(yb_ref[...], wout_ref[n_a:, :])
    hn = _rms(x1, gffn_ref[...]).astype(BF16)
    y_ref[...] = x1
    d_ff = w1_ref.shape[1]
    for j in range(d_ff // FF_CHUNK):
        sl = slice(j * FF_CHUNK, (j + 1) * FF_CHUNK)
        t = jnp.maximum(_dot(hn, w1_ref[:, sl].astype(BF16)), 0.0)
        y_ref[...] += _dot((t * t).astype(BF16), w2_ref[sl, :].astype(BF16))


def _out_call(xf, hf, hb, o, yb, gml, wout, gffn, w1, w2):
    n, d = xf.shape
    tm = ROW_TILE

    def full(a):
        return pl.BlockSpec(a.shape, lambda i: (0,) * a.ndim)

    tok = lambda w: pl.BlockSpec((tm, w), lambda i: (i, 0))
    in_specs = [tok(d), tok(512), tok(512), tok(512), tok(512), full(gml), full(wout), full(gffn),
                pl.BlockSpec(w1.shape, lambda i: (0, 0), pipeline_mode=pl.Buffered(1)),
                pl.BlockSpec(w2.shape, lambda i: (0, 0), pipeline_mode=pl.Buffered(1))]
    return pl.pallas_call(
        _out_kernel, grid=(n // tm,), in_specs=in_specs, out_specs=tok(d),
        out_shape=jax.ShapeDtypeStruct((n, d), F32),
        compiler_params=pltpu.CompilerParams(dimension_semantics=("arbitrary",),
                                             vmem_limit_bytes=VMEM_LIMIT),
        name="out",
    )(xf, hf, hb, o, yb, gml, wout, gffn, w1, w2)


def _layer(xf, pos3, invf, batch, seq, g_mix_norm, w_in, conv_w, conv_b, b_gates, g_mlstm_out,
           g_cq, g_ckv, w_uq, w_ukv, g_q, g_k, w_out, g_ffn_norm, w_ff1, w_ff2):
    d = xf.shape[1]
    c0 = 2 * 256 + 2 * 512
    wmain = w_in[:, 0:c0].astype(BF16)
    wt = w_in[:, c0:].astype(BF16)
    gates_w = wt[:, 0:N_GATES]
    cq_w = wt[:, N_GATES:N_GATES + 256]
    ckv_w = wt[:, N_GATES + 256:N_GATES + 384]
    kpe_w = wt[:, N_GATES + 384:N_GATES + 448]
    wtail = jnp.concatenate([cq_w, ckv_w, kpe_w, kpe_w[:, 0:32], gates_w,
                             jnp.zeros((d, LANES - 96 - N_GATES), BF16)], axis=1)
    wuqT = w_uq.T.astype(BF16)
    ukv = w_ukv.reshape(w_ukv.shape[0], A_HEADS, A_NOPE + A_DV)
    wuk = ukv[:, :, :A_NOPE].reshape(w_ukv.shape[0], A_HEADS * A_NOPE).astype(BF16)
    wuvT = ukv[:, :, A_NOPE:].reshape(w_ukv.shape[0], A_HEADS * A_DV).T.astype(BF16)
    gqcol = g_q.reshape(A_DH, 1)
    gk = jnp.concatenate([g_k, g_k[A_NOPE + 32:], g_k[A_NOPE:A_NOPE + 32]]).reshape(1, -1)

    qkc, v, o, sf, b3f, sb, b3b, qT, k, vT, kT = _proj_call(
        xf, pos3, g_mix_norm.reshape(1, -1), wmain, wtail, g_cq.reshape(1, -1),
        g_ckv.reshape(1, -1), wuqT, wuk, wuvT, gqcol, gk, invf, b_gates.reshape(2, 8, 1),
        conv_w, conv_b.reshape(1, -1), batch, seq)

    esel = jnp.tile(jnp.repeat(jnp.eye(8, M_HEADS, dtype=BF16), LANES, axis=1), (4, 1))
    hf, hb = _mlstm_call(qkc.reshape(batch, seq, 512), kT, v.reshape(batch, seq, 512),
                         sf, b3f, sb, b3b, esel)
    score_bound_log2 = (A_DH ** 0.5 * LOG2E * 1.01) * jnp.max(jnp.abs(g_q)) * jnp.max(jnp.abs(g_k))
    yb = _attn_call(qT, k, vT, score_bound_log2)
    return _out_call(xf, hf.reshape(batch * seq, -1), hb.reshape(batch * seq, -1), o,
                     yb.reshape(batch * seq, -1), g_mlstm_out.reshape(1, -1), w_out.astype(BF16),
                     g_ffn_norm.reshape(1, -1), w_ff1, w_ff2)


def kernel(x, positions, g_mix_norm, w_in, conv_w, conv_b, b_gates, g_mlstm_out, g_cq, g_ckv,
           w_uq, w_ukv, g_q, g_k, w_out, g_ffn_norm, w_ff1, w_ff2):
    batch, seq, d = x.shape
    assert seq % TOKEN_TILE == 0 and seq % M_CHUNK == 0
    inv_freq = ROPE_THETA ** (-jnp.arange(0, A_ROPE, 2, dtype=F32) / A_ROPE)
    invf = inv_freq.reshape(A_ROPE // 2, 1)
    pos3 = positions.reshape(batch * seq // TOKEN_TILE, 1, TOKEN_TILE)
    xf = x.reshape(batch * seq, d)
    for l in range(w_in.shape[0]):
        xf = _layer(xf, pos3, invf, batch, seq, g_mix_norm[l], w_in[l], conv_w[l], conv_b[l],
                    b_gates[l], g_mlstm_out[l], g_cq[l], g_ckv[l], w_uq[l], w_ukv[l], g_q[l],
                    g_k[l], w_out[l], g_ffn_norm[l], w_ff1[l], w_ff2[l])
    return xf.reshape(batch, seq, d)
```

```python
import functools
import math

import jax
import jax.numpy as jnp
from jax import lax
from jax.experimental import pallas as pl
from jax.experimental.pallas import tpu as pltpu

F32 = jnp.float32
BF16 = jnp.bfloat16

EPS = 1e-6
ROPE_THETA = 10000.0
M_HEADS = 4
M_DQK = 64
M_DV = 128
M_CONV = 5
M_CHUNK = 128
A_HEADS = 4
A_NOPE = 128
A_ROPE = 64
A_DV = 128
A_DH = A_NOPE + A_ROPE
N_GATES = 4 * M_HEADS

LANES = 128
TOKEN_TILE = 1024
ROW_TILE = 512
CONV_ROWS = 128
KV_TILE = 1024
Q_TILE = 1024
PROJ_PARTS = 4
MLSTM_CHUNKS_PER_STEP = 8
SAFE_SCORE_LOG2 = 64.0
FF_CHUNK = 1024
VMEM_LIMIT = 56 * 1024 * 1024

LOG2E = math.log2(math.e)


def _dot(a, b):
    return jnp.dot(a, b, preferred_element_type=F32)


def _dot_nt(a, b):
    return lax.dot_general(a, b, (((1,), (1,)), ((), ())), preferred_element_type=F32)


def _dot_tn(a, b):
    return lax.dot_general(a, b, (((0,), (0,)), ((), ())), preferred_element_type=F32)


def _rms(x, g):
    ms = jnp.mean(x * x, axis=-1, keepdims=True)
    return x * lax.rsqrt(ms + EPS) * g


def _split3(y):
    hi = y.astype(BF16).astype(F32)
    mid = (y - hi).astype(BF16).astype(F32)
    lo = y - hi - mid
    return jnp.concatenate([hi, mid, lo], axis=0).astype(BF16)


def _chunk_decay(x, *, backward):
    L = M_CHUNK
    row = lax.broadcasted_iota(jnp.int32, (L, L), 0)
    col = lax.broadcasted_iota(jnp.int32, (L, L), 1)
    umat = jnp.where((row >= col) if backward else (row <= col), 1.0, 0.0).astype(BF16)
    lf = jax.nn.log_sigmoid(x)
    cum3 = _dot(_split3(lf), umat)
    cum = cum3[0:8] + cum3[8:16] + cum3[16:24]
    a = jnp.broadcast_to(jnp.sum(lf, axis=1, keepdims=True), (8, L))
    b = pltpu.roll(cum, 4, axis=0)
    a = pltpu.roll(a, 4, axis=0)
    w_log = a - b + x
    g = jnp.broadcast_to(jnp.max(w_log, axis=1, keepdims=True), (8, L))
    stats = jnp.concatenate([jnp.exp(w_log - g), (x - b) * LOG2E, a * LOG2E, g * LOG2E], axis=0)
    b3 = jnp.concatenate([_split3(b * LOG2E), jnp.zeros((8, L), BF16)], axis=0)
    return stats, b3


def _proj_kernel(x_ref, pos_ref, gmix_ref, wmain_ref, wtail_ref, gcq_ref, gckv_ref, wuqT_ref,
                 wuk_ref, wuvT_ref, gqcol_ref, gk_ref, invf_ref, bg_ref,
                 qk_ref, v_ref, o_ref, sf_ref, b3f_ref, sb_ref, b3b_ref, qT_ref, k_ref, vT_ref):
    tm = x_ref.shape[0]
    parts = [slice(i * tm // PROJ_PARTS, (i + 1) * tm // PROJ_PARTS) for i in range(PROJ_PARTS)]

    def project(rows):
        hn = _rms(x_ref[rows, :], gmix_ref[...]).astype(BF16)
        z = _dot(hn, wmain_ref[...])
        qk_ref[rows, :] = z[:, 0:512]
        v_ref[rows, :] = z[:, 512:1024].astype(BF16)
        o_ref[rows, :] = z[:, 1024:1536]
        return _dot(hn, wtail_ref[...])

    def latents(zt, rows):
        cq = zt[:, 0:256]
        ckv = zt[:, 256:384]
        tail = zt[:, 384:512]

        cqnT = _rms(cq, gcq_ref[...]).T.astype(BF16)
        qT = _dot(wuqT_ref[...], cqnT)
        ckvn = _rms(ckv, gckv_ref[...])
        knope = _dot(ckvn.astype(BF16), wuk_ref[...])
        vT = _dot(wuvT_ref[...], ckvn.T.astype(BF16))

        tailT = tail.T
        for d, (s_ref, b3_ref) in enumerate(((sf_ref, b3f_ref), (sb_ref, b3b_ref))):
            gx = tailT[96 + 8 * d:104 + 8 * d] + bg_ref[d]
            for j in range(gx.shape[1] // M_CHUNK):
                sl = slice(j * M_CHUNK, (j + 1) * M_CHUNK)
                dst = slice(rows.start + sl.start, rows.start + sl.stop)
                stats, b3 = _chunk_decay(gx[:, sl], backward=d == 1)
                s_ref[0, :, dst] = stats
                b3_ref[0, :, dst] = b3

        pos = pos_ref[0][:, rows].astype(F32)
        angT = invf_ref[...] * pos
        cosT = jnp.cos(angT)
        sinT = jnp.sin(angT)

        gq = gqcol_ref[...] * (A_DH ** -0.5 * LOG2E)
        for h in range(A_HEADS):
            qh = qT[h * A_DH:(h + 1) * A_DH]
            ms = jnp.sum(qh * qh, axis=0, keepdims=True) * (1.0 / A_DH)
            qn = qh * lax.rsqrt(ms + EPS) * gq
            x1 = qn[A_NOPE:A_NOPE + 32]
            x2 = qn[A_NOPE + 32:A_DH]
            qT_ref[0, h, 0, 0:A_NOPE, rows] = qn[0:A_NOPE].astype(BF16)
            qT_ref[0, h, 0, A_NOPE:A_NOPE + 32, rows] = (x1 * cosT - x2 * sinT).astype(BF16)
            qT_ref[0, h, 0, A_NOPE + 32:A_DH, rows] = (x2 * cosT + x1 * sinT).astype(BF16)

        for h in range(A_HEADS):
            vT_ref[0, h, 0, :, rows] = vT[h * A_DV:(h + 1) * A_DV].astype(BF16)

        tab = jnp.concatenate([cosT, cosT, -sinT, sinT], axis=0).T
        cc = tab[:, 0:A_ROPE]
        ss = tab[:, A_ROPE:2 * A_ROPE]
        gk = gk_ref[...]
        kpe = tail[:, 0:A_ROPE]
        kpe_sw = pltpu.roll(tail, LANES - 32, axis=1)[:, 0:A_ROPE]
        ss_pe = jnp.sum(kpe * kpe, axis=-1, keepdims=True)
        krope = (kpe * gk[:, A_NOPE:A_DH] * cc + kpe_sw * gk[:, A_DH:A_DH + A_ROPE] * ss)
        for h in range(A_HEADS):
            kn = knope[:, h * A_NOPE:(h + 1) * A_NOPE]
            ms = (jnp.sum(kn * kn, axis=-1, keepdims=True) + ss_pe) * (1.0 / A_DH)
            r = lax.rsqrt(ms + EPS)
            k_ref[0, h, rows, 0:A_NOPE] = (kn * r * gk[:, 0:A_NOPE]).astype(BF16)
            k_ref[0, h, rows, A_NOPE:A_DH] = (krope * r).astype(BF16)

    zt = project(parts[0])
    for i, rows in enumerate(parts):
        zt_next = project(parts[i + 1]) if i + 1 < len(parts) else None
        latents(zt, rows)
        zt = zt_next


def _proj_call(xf, pos3, gmix, wmain, wtail, gcq, gckv, wuqT, wuk, wuvT, gqcol, gk, invf, bg,
               batch, seq):
    n, d = xf.shape
    tm = TOKEN_TILE
    nt = seq // tm
    steps = n // tm

    def full(a):
        return pl.BlockSpec(a.shape, lambda i: (0,) * a.ndim)

    tok = lambda w: pl.BlockSpec((tm, w), lambda i: (i, 0))
    in_specs = [tok(d), pl.BlockSpec((1, 1, tm), lambda i: (i, 0, 0)), full(gmix), full(wmain),
                full(wtail), full(gcq), full(gckv), full(wuqT), full(wuk), full(wuvT), full(gqcol),
                full(gk), full(invf), full(bg)]
    stat_spec = pl.BlockSpec((1, 32, tm), lambda i: (i // nt, 0, i % nt))
    out_shape = [
        jax.ShapeDtypeStruct((n, 512), F32),
        jax.ShapeDtypeStruct((n, 512), BF16),
        jax.ShapeDtypeStruct((n, 512), F32),
        jax.ShapeDtypeStruct((batch, 32, seq), F32),
        jax.ShapeDtypeStruct((batch, 32, seq), BF16),
        jax.ShapeDtypeStruct((batch, 32, seq), F32),
        jax.ShapeDtypeStruct((batch, 32, seq), BF16),
        jax.ShapeDtypeStruct((batch, A_HEADS, nt, A_DH, tm), BF16),
        jax.ShapeDtypeStruct((batch, A_HEADS, seq, A_DH), BF16),
        jax.ShapeDtypeStruct((batch, A_HEADS, nt, A_DV, tm), BF16),
    ]
    out_specs = [
        tok(512), tok(512), tok(512), stat_spec, stat_spec, stat_spec, stat_spec,
        pl.BlockSpec((1, A_HEADS, 1, A_DH, tm), lambda i: (i // nt, 0, i % nt, 0, 0)),
        pl.BlockSpec((1, A_HEADS, tm, A_DH), lambda i: (i // nt, 0, i % nt, 0)),
        pl.BlockSpec((1, A_HEADS, 1, A_DV, tm), lambda i: (i // nt, 0, i % nt, 0, 0)),
    ]
    return pl.pallas_call(
        _proj_kernel, grid=(steps,), in_specs=in_specs, out_specs=out_specs, out_shape=out_shape,
        compiler_params=pltpu.CompilerParams(dimension_semantics=("arbitrary",),
                                             vmem_limit_bytes=VMEM_LIMIT),
        name="proj",
    )(xf, pos3, gmix, wmain, wtail, gcq, gckv, wuqT, wuk, wuvT, gqcol, gk, invf, bg)


def _conv_kernel(cur_ref, prev_ref, next_ref, w_ref, b_ref, qk_ref, kT_ref, *, tiles_per_seq):
    t = pl.program_id(0) % tiles_per_seq
    tm = cur_ref.shape[0]
    prev = jnp.where(t == 0, 0.0, prev_ref[...])
    nxt = jnp.where(t == tiles_per_seq - 1, 0.0, next_ref[...])
    cw = w_ref[...]
    rb = CONV_ROWS
    for blk in range(tm // rb):
        lo = blk * rb
        top = prev if blk == 0 else cur_ref[lo - 8:lo, :]
        bot = nxt if lo + rb == tm else cur_ref[lo + rb:lo + rb + 8, :]
        ext = jnp.concatenate([top, cur_ref[lo:lo + rb, :], bot], axis=0)
        u = b_ref[...]
        for j in range(M_CONV):
            shifted = (ext if j == M_CONV // 2
                       else pltpu.roll(ext, (M_CONV // 2 - j) % (rb + 16), axis=0))
            u = u + cw[j:j + 1, :] * shifted[8:8 + rb, :]
        qk = u * jax.nn.sigmoid(u)
        k = qk[:, 256:512]
        qk_ref[lo:lo + rb, 0:256] = (qk[:, 0:256] * (M_DQK ** -0.5)).astype(BF16)
        qk_ref[lo:lo + rb, 256:512] = k.astype(BF16)
        kT_ref[0, :, lo:lo + rb] = k.T.astype(BF16)


def _conv_call(qk, convw, convb, batch, seq):
    n = qk.shape[0]
    tm = ROW_TILE
    nt = seq // tm
    r8 = tm // 8
    return pl.pallas_call(
        functools.partial(_conv_kernel, tiles_per_seq=nt),
        grid=(n // tm,),
        in_specs=[
            pl.BlockSpec((tm, 512), lambda i: (i, 0)),
            pl.BlockSpec((8, 512), lambda i: (jnp.maximum(i * r8 - 1, 0), 0)),
            pl.BlockSpec((8, 512), lambda i: (jnp.minimum((i + 1) * r8, n // 8 - 1), 0)),
            pl.BlockSpec(convw.shape, lambda i: (0, 0)),
            pl.BlockSpec(convb.shape, lambda i: (0, 0)),
        ],
        out_specs=[pl.BlockSpec((tm, 512), lambda i: (i, 0)),
                   pl.BlockSpec((1, 256, tm), lambda i: (i // nt, 0, i % nt))],
        out_shape=[jax.ShapeDtypeStruct((n, 512), BF16),
                   jax.ShapeDtypeStruct((batch, 256, seq), BF16)],
        compiler_params=pltpu.CompilerParams(dimension_semantics=("arbitrary",),
                                             vmem_limit_bytes=VMEM_LIMIT),
        name="conv",
    )(qk, qk, qk, convw, convb)


def _mlstm_kernel(qkf_ref, kTf_ref, vf_ref, sf_ref, b3f_ref, qkb_ref, kTb_ref, vb_ref, sb_ref,
                  b3b_ref, esel_ref, hf_ref, hb_ref, cf_sc, cb_sc, mf_sc, mb_sc):
    L = M_CHUNK
    n_sub = qkf_ref.shape[1] // L

    @pl.when(pl.program_id(1) == 0)
    def _():
        cf_sc[...] = jnp.zeros_like(cf_sc)
        cb_sc[...] = jnp.zeros_like(cb_sc)
        mf_sc[...] = jnp.zeros_like(mf_sc)
        mb_sc[...] = jnp.zeros_like(mb_sc)

    esel = esel_ref[...]
    row = lax.broadcasted_iota(jnp.int32, (L, L), 0)
    col = lax.broadcasted_iota(jnp.int32, (L, L), 1)
    ones_blk = jnp.ones((L, M_DV), BF16)
    fw = dict(qk=qkf_ref, kT=kTf_ref, v=vf_ref, s=sf_ref, b3=b3f_ref, out=hf_ref, c_sc=cf_sc,
              m_sc=mf_sc, visible=col <= row)
    bw = dict(qk=qkb_ref, kT=kTb_ref, v=vb_ref, s=sb_ref, b3=b3b_ref, out=hb_ref, c_sc=cb_sc,
              m_sc=mb_sc, visible=col >= row)

    work = []
    for d in (fw, bw):
        m = d["m_sc"][...]
        order = range(n_sub) if d is fw else range(n_sub - 1, -1, -1)
        for i in order:
            sl = slice(i * L, (i + 1) * L)
            stats = d["s"][0, :, sl]
            a, g = stats[16:24], stats[24:32]
            m_new = jnp.maximum(a + m, g)
            work.append((d, sl, dict(
                w=stats[0:8], r=stats[8:16], m_prev=m,
                dec=jnp.exp2(a + m - m_new), add=jnp.exp2(g - m_new),
                brep=_dot_tn(d["b3"][0, :, sl], esel))))
            m = m_new
        d["m_sc"][...] = m
    work = [work[j + k * n_sub] for j in range(n_sub) for k in range(2)]
    units = [(d, sl, gt, h) for d, sl, gt in work for h in range(M_HEADS)]

    grams, v_augs, s_augs = [], [], []
    for d, sl, gt, h in units:
        q_h = d["qk"][0, sl, h * M_DQK:(h + 1) * M_DQK]
        k_h = d["qk"][0, sl, 256 + h * M_DQK:256 + (h + 1) * M_DQK]
        grams.append(_dot_nt(q_h, k_h))
        v_augs.append(jnp.concatenate([d["v"][0, sl, h * M_DV:(h + 1) * M_DV], ones_blk], axis=1))
    for (d, sl, gt, h), v_aug in zip(units, v_augs):
        kTw = d["kT"][0, h * M_DQK:(h + 1) * M_DQK, sl].astype(F32) * gt["w"][h:h + 1, :]
        s_augs.append(_dot(kTw.astype(BF16), v_aug))

    states = {id(fw): [cf_sc[h] for h in range(M_HEADS)],
              id(bw): [cb_sc[h] for h in range(M_HEADS)]}
    for (d, sl, gt, h), gram, v_aug, s_aug in zip(units, grams, v_augs, s_augs):
        q_h = d["qk"][0, sl, h * M_DQK:(h + 1) * M_DQK]
        brep = gt["brep"][:, h * LANES:(h + 1) * LANES]
        bm = brep + gt["m_prev"][h:h + 1, :]
        dmat = jnp.where(d["visible"], brep + gt["r"][h:h + 1, :], -jnp.inf)
        m_t = jnp.maximum(bm, jnp.max(dmat, axis=1, keepdims=True))
        p = jnp.exp2(dmat - m_t) * gram
        inter = jnp.exp2(bm - m_t)
        c_prev = states[id(d)][h]
        lhs = jnp.concatenate(
            [p.astype(BF16), inter[:, 0:M_DQK].astype(BF16) * q_h], axis=1)
        rhs = jnp.concatenate([v_aug, c_prev.astype(BF16)], axis=0)
        res = _dot(lhs, rhs)
        inv_den = jnp.minimum(1.0 / jnp.abs(res[:, M_DV:2 * M_DV]), jnp.exp2(m_t))
        d["out"][0, sl, h * M_DV:(h + 1) * M_DV] = res[:, 0:M_DV] * inv_den
        dec_h = jnp.concatenate([gt["dec"][h:h + 1, :]] * 2, axis=1)
        add_h = jnp.concatenate([gt["add"][h:h + 1, :]] * 2, axis=1)
        states[id(d)][h] = dec_h * c_prev + add_h * s_aug
    for h in range(M_HEADS):
        cf_sc[h] = states[id(fw)][h]
        cb_sc[h] = states[id(bw)][h]


def _mlstm_call(qk3, kT3, v3, sf, b3f, sb, b3b, esel):
    batch, seq, _ = qk3.shape
    L = M_CHUNK * MLSTM_CHUNKS_PER_STEP
    nc = seq // L
    fw_rows = lambda w: pl.BlockSpec((1, L, w), lambda b, c: (b, c, 0))
    bw_rows = lambda w: pl.BlockSpec((1, L, w), lambda b, c: (b, nc - 1 - c, 0))
    fw_lanes = lambda r: pl.BlockSpec((1, r, L), lambda b, c: (b, 0, c))
    bw_lanes = lambda r: pl.BlockSpec((1, r, L), lambda b, c: (b, 0, nc - 1 - c))
    state = pltpu.VMEM((M_HEADS, M_DQK, 2 * M_DV), F32)
    return pl.pallas_call(
        _mlstm_kernel,
        grid=(batch, nc),
        in_specs=[fw_rows(512), fw_lanes(256), fw_rows(512), fw_lanes(32), fw_lanes(32),
                  bw_rows(512), bw_lanes(256), bw_rows(512), bw_lanes(32), bw_lanes(32),
                  pl.BlockSpec(esel.shape, lambda b, c: (0, 0))],
        out_specs=[fw_rows(512), bw_rows(512)],
        out_shape=[jax.ShapeDtypeStruct((batch, seq, 512), F32)] * 2,
        scratch_shapes=[state, state, pltpu.VMEM((8, M_CHUNK), F32), pltpu.VMEM((8, M_CHUNK), F32)],
        compiler_params=pltpu.CompilerParams(dimension_semantics=("arbitrary", "arbitrary"),
                                             vmem_limit_bytes=VMEM_LIMIT),
        name="mlstm",
    )(qk3, kT3, v3, sf, b3f, qk3, kT3, v3, sb, b3b, esel)


def _attn_kernel(qT_ref, k_ref, vT_ref, o_ref, s_sc, cm_sc, acc_sc, *, n_kv):
    tk = KV_TILE
    sub = tk // TOKEN_TILE
    qT = jnp.concatenate([qT_ref[0, 0, i] for i in range(qT_ref.shape[2])], axis=1)
    tq = qT.shape[1]

    def scores(j, slot):
        sT = _dot(k_ref[0, 0, j * tk:(j + 1) * tk, :], qT)
        s_sc[slot] = sT
        cm_sc[slot] = jnp.max(sT, axis=0, keepdims=True)

    m = jnp.full((1, tq), -jnp.inf, F32)
    l = jnp.zeros((1, tq), F32)
    scores(0, 0)
    for j in range(n_kv):
        slot = j % 2
        if j + 1 < n_kv:
            scores(j + 1, 1 - slot)
        m_new = jnp.maximum(m, cm_sc[slot])
        alpha = jnp.exp2(m - m_new)
        p = jnp.exp2(s_sc[slot] - m_new)
        l = alpha * l + jnp.sum(p, axis=0, keepdims=True)
        vt = jnp.concatenate([vT_ref[0, 0, j * sub + i] for i in range(sub)], axis=1)
        pv = _dot(vt, p.astype(BF16))
        if j == 0:
            acc_sc[...] = pv
        else:
            acc_sc[...] = alpha * acc_sc[...] + pv
        m = m_new
    o_ref[0] = (acc_sc[...] / l).T.astype(o_ref.dtype)


def _attn_bounded_kernel(qT_ref, k_ref, vT_ref, o_ref, acc_sc, *, n_kv):
    tk = KV_TILE
    sub = tk // TOKEN_TILE
    qT = jnp.concatenate([qT_ref[0, 0, i] for i in range(qT_ref.shape[2])], axis=1)
    l = jnp.zeros((1, qT.shape[1]), F32)
    for j in range(n_kv):
        p = jnp.exp2(_dot(k_ref[0, 0, j * tk:(j + 1) * tk, :], qT))
        l = l + jnp.sum(p, axis=0, keepdims=True)
        vt = jnp.concatenate([vT_ref[0, 0, j * sub + i] for i in range(sub)], axis=1)
        pv = _dot(vt, p.astype(BF16))
        if j == 0:
            acc_sc[...] = pv
        else:
            acc_sc[...] += pv
    o_ref[0] = (acc_sc[...] / l).T.astype(o_ref.dtype)


def _attn_call(qT, k, vT, score_bound_log2):
    batch, heads, nt, _, tm = qT.shape
    seq = k.shape[2]
    tq = Q_TILE
    n_kv = seq // KV_TILE
    slabs = tq // tm

    def body(safe_ref, qT_ref, k_ref, vT_ref, o_ref, s_sc, cm_sc, acc_sc):
        @pl.when(safe_ref[0] != 0)
        def _():
            _attn_bounded_kernel(qT_ref, k_ref, vT_ref, o_ref, acc_sc, n_kv=n_kv)

        @pl.when(safe_ref[0] == 0)
        def _():
            _attn_kernel(qT_ref, k_ref, vT_ref, o_ref, s_sc, cm_sc, acc_sc, n_kv=n_kv)

    safe = (score_bound_log2 <= SAFE_SCORE_LOG2).astype(jnp.int32).reshape(1)
    return pl.pallas_call(
        body,
        grid_spec=pltpu.PrefetchScalarGridSpec(
            num_scalar_prefetch=1,
            grid=(batch, heads, seq // tq),
            in_specs=[
                pl.BlockSpec((1, 1, slabs, A_DH, tm), lambda b, h, i, safe: (b, h, i, 0, 0)),
                pl.BlockSpec((1, 1, seq, A_DH), lambda b, h, i, safe: (b, h, 0, 0)),
                pl.BlockSpec((1, 1, nt, A_DV, tm), lambda b, h, i, safe: (b, h, 0, 0, 0)),
            ],
            out_specs=pl.BlockSpec((1, tq, A_DV), lambda b, h, i, safe: (b, i, h)),
            scratch_shapes=[pltpu.VMEM((2, KV_TILE, tq), F32), pltpu.VMEM((2, 1, tq), F32),
                            pltpu.VMEM((A_DV, tq), F32)]),
        out_shape=jax.ShapeDtypeStruct((batch, seq, heads * A_DV), BF16),
        compiler_params=pltpu.CompilerParams(
            dimension_semantics=("arbitrary", "arbitrary", "arbitrary"),
            vmem_limit_bytes=VMEM_LIMIT),
        name="attn",
    )(safe, qT, k, vT)


def _out_kernel(x_ref, hf_ref, hb_ref, o_ref, yb_ref, gml_ref, wout_ref, gffn_ref, w1_ref, w2_ref,
                y_ref):
    x = x_ref[...]
    hsum = hf_ref[...] + hb_ref[...]
    gate = jax.nn.sigmoid(o_ref[...])
    gml = gml_ref[...]
    parts = []
    for h in range(M_HEADS):
        sl = slice(h * M_DV, (h + 1) * M_DV)
        parts.append((gate[:, sl] * _rms(hsum[:, sl], gml[:, sl])).astype(BF16))
    y_a = jnp.concatenate(parts, axis=1)
    n_a = y_a.shape[1]
    x1 = x + _dot(y_a, wout_ref[0:n_a, :]) + _dot(yb_ref[...], wout_ref[n_a:, :])
    hn = _rms(x1, gffn_ref[...]).astype(BF16)
    y_ref[...] = x1
    d_ff = w1_ref.shape[1]
    for j in range(d_ff // FF_CHUNK):
        sl = slice(j * FF_CHUNK, (j + 1) * FF_CHUNK)
        t = jnp.maximum(_dot(hn, w1_ref[:, sl].astype(BF16)), 0.0)
        y_ref[...] += _dot((t * t).astype(BF16), w2_ref[sl, :].astype(BF16))


def _out_call(xf, hf, hb, o, yb, gml, wout, gffn, w1, w2):
    n, d = xf.shape
    tm = ROW_TILE

    def full(a):
        return pl.BlockSpec(a.shape, lambda i: (0,) * a.ndim)

    tok = lambda w: pl.BlockSpec((tm, w), lambda i: (i, 0))
    in_specs = [tok(d), tok(512), tok(512), tok(512), tok(512), full(gml), full(wout), full(gffn),
                pl.BlockSpec(w1.shape, lambda i: (0, 0), pipeline_mode=pl.Buffered(1)),
                pl.BlockSpec(w2.shape, lambda i: (0, 0), pipeline_mode=pl.Buffered(1))]
    return pl.pallas_call(
        _out_kernel, grid=(n // tm,), in_specs=in_specs, out_specs=tok(d),
        out_shape=jax.ShapeDtypeStruct((n, d), F32),
        compiler_params=pltpu.CompilerParams(dimension_semantics=("arbitrary",),
                                             vmem_limit_bytes=VMEM_LIMIT),
        name="out",
    )(xf, hf, hb, o, yb, gml, wout, gffn, w1, w2)


def _layer(xf, pos3, invf, batch, seq, g_mix_norm, w_in, conv_w, conv_b, b_gates, g_mlstm_out,
           g_cq, g_ckv, w_uq, w_ukv, g_q, g_k, w_out, g_ffn_norm, w_ff1, w_ff2):
    d = xf.shape[1]
    c0 = 2 * 256 + 2 * 512
    wmain = w_in[:, 0:c0].astype(BF16)
    wt = w_in[:, c0:].astype(BF16)
    gates_w = wt[:, 0:N_GATES]
    cq_w = wt[:, N_GATES:N_GATES + 256]
    ckv_w = wt[:, N_GATES + 256:N_GATES + 384]
    kpe_w = wt[:, N_GATES + 384:N_GATES + 448]
    wtail = jnp.concatenate([cq_w, ckv_w, kpe_w, kpe_w[:, 0:32], gates_w,
                             jnp.zeros((d, LANES - 96 - N_GATES), BF16)], axis=1)
    wuqT = w_uq.T.astype(BF16)
    ukv = w_ukv.reshape(w_ukv.shape[0], A_HEADS, A_NOPE + A_DV)
    wuk = ukv[:, :, :A_NOPE].reshape(w_ukv.shape[0], A_HEADS * A_NOPE).astype(BF16)
    wuvT = ukv[:, :, A_NOPE:].reshape(w_ukv.shape[0], A_HEADS * A_DV).T.astype(BF16)
    gqcol = g_q.reshape(A_DH, 1)
    gk = jnp.concatenate([g_k, g_k[A_NOPE + 32:], g_k[A_NOPE:A_NOPE + 32]]).reshape(1, -1)

    qk, v, o, sf, b3f, sb, b3b, qT, k, vT = _proj_call(
        xf, pos3, g_mix_norm.reshape(1, -1), wmain, wtail, g_cq.reshape(1, -1),
        g_ckv.reshape(1, -1), wuqT, wuk, wuvT, gqcol, gk, invf, b_gates.reshape(2, 8, 1), batch, seq)

    qkc, kT = _conv_call(qk, conv_w, conv_b.reshape(1, -1), batch, seq)
    esel = jnp.tile(jnp.repeat(jnp.eye(8, M_HEADS, dtype=BF16), LANES, axis=1), (4, 1))
    hf, hb = _mlstm_call(qkc.reshape(batch, seq, 512), kT, v.reshape(batch, seq, 512),
                         sf, b3f, sb, b3b, esel)
    score_bound_log2 = (A_DH ** 0.5 * LOG2E * 1.01) * jnp.max(jnp.abs(g_q)) * jnp.max(jnp.abs(g_k))
    yb = _attn_call(qT, k, vT, score_bound_log2)
    return _out_call(xf, hf.reshape(batch * seq, -1), hb.reshape(batch * seq, -1), o,
                     yb.reshape(batch * seq, -1), g_mlstm_out.reshape(1, -1), w_out.astype(BF16),
                     g_ffn_norm.reshape(1, -1), w_ff1, w_ff2)


def kernel(x, positions, g_mix_norm, w_in, conv_w, conv_b, b_gates, g_mlstm_out, g_cq, g_ckv,
           w_uq, w_ukv, g_q, g_k, w_out, g_ffn_norm, w_ff1, w_ff2):
    batch, seq, d = x.shape
    assert seq % TOKEN_TILE == 0 and seq % M_CHUNK == 0
    inv_freq = ROPE_THETA ** (-jnp.arange(0, A_ROPE, 2, dtype=F32) / A_ROPE)
    invf = inv_freq.reshape(A_ROPE // 2, 1)
    pos3 = positions.reshape(batch * seq // TOKEN_TILE, 1, TOKEN_TILE)
    xf = x.reshape(batch * seq, d)
    for l in range(w_in.shape[0]):
        xf = _layer(xf, pos3, invf, batch, seq, g_mix_norm[l], w_in[l], conv_w[l], conv_b[l],
                    b_gates[l], g_mlstm_out[l], g_cq[l], g_ckv[l], w_uq[l], w_ukv[l], g_q[l],
                    g_k[l], w_out[l], g_ffn_norm[l], w_ff1[l], w_ff2[l])
    return xf.reshape(batch, seq, d)
```
